```python
import math
import jax
import jax.numpy as jnp
from jax import lax
import numpy as np

D_MODEL = 2048
BATCH = 2
SEQ = 4096
DEPTH = 2

GRID_W = 64
CTX_LEN = 256
NORM_EPS = 1e-6
N_BRANCH = 4
MIX_WIDTH = 768

S5_WIDTH = MIX_WIDTH
S5_GROUP = 16
S5_GROUPS = S5_WIDTH // S5_GROUP
S5_STATE = 64
SSD_WIDTH = MIX_WIDTH
SSD_HEAD_DIM = 64
SSD_HEADS = SSD_WIDTH // SSD_HEAD_DIM
SSD_GROUPS = 2
SSD_STATE = 64
SSD_CONV = 3
SSD_CHUNK = 128
SSD_CONV_CH = SSD_WIDTH + 2 * SSD_GROUPS * SSD_STATE
GLA_WIDTH = MIX_WIDTH
GLA_HEADS = 6
GLA_KEY_WIDTH = GLA_WIDTH // 2
GLA_DK = GLA_KEY_WIDTH // GLA_HEADS
GLA_DV = GLA_WIDTH // GLA_HEADS
GLA_GATE_RANK = 16
GLA_TAU = 16.0
GLA_CHUNK = 64
HY_WIDTH = MIX_WIDTH
HY_ORDER = 2
HY_SHORT = 3
HY_FILTER_DIM = 64
HY_BANDS = 16
HY_EMB = 2 * HY_BANDS + 1
HY_MAX_DECAY = math.log(1e-2) / 0.3
HY_MIN_DECAY = math.log(1e-2) / 1.5
MOE_GROUPS = 4
MOE_PER_GROUP = 4
MOE_EXPERTS = MOE_GROUPS * MOE_PER_GROUP
MOE_TOPK = 2
MOE_FF = 1024

IN_SIZES = (S5_WIDTH,
            SSD_WIDTH,
            SSD_CONV_CH,
            2 * SSD_HEADS,
            GLA_KEY_WIDTH, GLA_KEY_WIDTH, GLA_WIDTH, GLA_WIDTH,
            2 * GLA_GATE_RANK,
            (HY_ORDER + 1) * HY_WIDTH,
            N_BRANCH * D_MODEL)
IN_COLS = sum(IN_SIZES)

kernel_name = 'hybrid_s5_ssd_gla_hyena_hmoe_dit'

F32 = jnp.float32


def rmsnorm(x):
    xf = x.astype(F32)
    return xf * lax.rsqrt(jnp.mean(xf * xf, axis=-1, keepdims=True) + NORM_EPS)


def modulate(x, shift, scale):
    return rmsnorm(x) * (1.0 + scale) + shift


def split_cols(p, sizes):
    return jnp.split(p, np.cumsum(sizes)[:-1].tolist(), axis=-1)


def flip_seq(t, rev):
    return jnp.flip(t, axis=1) if rev else t


def dwconv(u, w, b):
    k = w.shape[0]
    y = lax.conv_general_dilated(u.astype(F32), w.astype(F32)[:, None, :], window_strides=(1,),
                                 padding=[(k // 2, k // 2)], dimension_numbers=('NWC', 'WIO', 'NWC'),
                                 feature_group_count=u.shape[-1])
    return y + b.astype(F32)


def to_colmajor(t, rows):
    b, n, ch = t.shape
    return t.reshape(b, rows, GRID_W, ch).transpose(0, 2, 1, 3).reshape(b, n, ch)


def from_colmajor(t, rows):
    b, n, ch = t.shape
    return t.reshape(b, GRID_W, rows, ch).transpose(0, 2, 1, 3).reshape(b, n, ch)


def cmul(ar, ai, br, bi):
    return ar * br - ai * bi, ar * bi + ai * br


def s5_combine(e1, e2):
    a1r, a1i, b1r, b1i = e1
    a2r, a2i, b2r, b2i = e2
    ar, ai = cmul(a2r, a2i, a1r, a1i)
    br, bi = cmul(a2r, a2i, b1r, b1i)
    return ar, ai, br + b2r, bi + b2i


def s5_scan(u, lam_re, lam_im, log_step, b_re, b_im, c_re, c_im, h0_re, h0_im, reverse):
    lam_re = lam_re.astype(F32)
    lam_im = lam_im.astype(F32)
    step = jnp.exp(log_step.astype(F32))[:, None]
    mag = jnp.exp(lam_re * step)
    ab_re = mag * jnp.cos(lam_im * step)
    ab_im = mag * jnp.sin(lam_im * step)
    den = lam_re * lam_re + lam_im * lam_im
    zr, zi = cmul(ab_re - 1.0, ab_im, lam_re, -lam_im)
    zr, zi = zr / den, zi / den
    bb_re, bb_im = cmul(zr[..., None], zi[..., None], b_re.astype(F32), b_im.astype(F32))
    bu_re = jnp.einsum('blgh,gph->blgp', u, bb_re)
    bu_im = jnp.einsum('blgh,gph->blgp', u, bb_im)
    first, final = (-1, 0) if reverse else (0, -1)
    ir, ii = cmul(ab_re, ab_im, h0_re, h0_im)
    bu_re = bu_re.at[:, first].add(ir)
    bu_im = bu_im.at[:, first].add(ii)
    elems = (jnp.broadcast_to(ab_re, bu_re.shape), jnp.broadcast_to(ab_im, bu_im.shape), bu_re, bu_im)
    _, _, h_re, h_im = lax.associative_scan(s5_combine, elems, reverse=reverse, axis=1)
    y = jnp.einsum('blgp,ghp->blgh', h_re, c_re.astype(F32)) - jnp.einsum('blgp,ghp->blgh', h_im, c_im.astype(F32))
    return y, h_re[:, final], h_im[:, final]


def s5_mixer(u_c, u_l, lam_re, lam_im, log_step, b_re, b_im, c_re, c_im, d_skip, glu_w):
    def grp(u):
        return u.astype(F32).reshape(u.shape[0], u.shape[1], S5_GROUPS, S5_GROUP)
    uc, ul = grp(u_c), grp(u_l)
    zero = jnp.zeros((uc.shape[0], S5_GROUPS, S5_STATE), F32)
    ys_c, ys_l = [], []
    for d, rev in enumerate((False, True)):
        prm = (lam_re[d], lam_im[d], log_step[d], b_re[d], b_im[d], c_re[d], c_im[d])
        yc, hr, hi = s5_scan(uc, *prm, zero, zero, rev)
        yl, _, _ = s5_scan(ul, *prm, hr, hi, rev)
        ys_c.append(yc)
        ys_l.append(yl)

    def out(y, u):
        y = y.reshape(u.shape[0], u.shape[1], S5_WIDTH) + d_skip.astype(F32) * u.astype(F32)
        g = jax.nn.gelu(y)
        return g * jax.nn.sigmoid(g @ glu_w)
    return out(ys_c[0] + ys_c[1], u_c), out(ys_l[0] + ys_l[1], u_l)


def segsum(a):
    t = a.shape[-1]
    cs = jnp.cumsum(a, axis=-1)
    diff = cs[..., :, None] - cs[..., None, :]
    return jnp.where(jnp.tril(jnp.ones((t, t), dtype=bool)), diff, -jnp.inf)


def ssd_chunked(xs, dt, a, bm, cm, init):
    b, n, h, p = xs.shape
    g, ns = bm.shape[2], bm.shape[3]
    q = SSD_CHUNK
    c = n // q
    xc = (xs * dt[..., None]).reshape(b, c, q, h, p)
    adt = jnp.moveaxis((a * dt).reshape(b, c, q, h), 3, 1)
    bc = jnp.repeat(bm.reshape(b, c, q, g, ns), h // g, axis=3)
    cc = jnp.repeat(cm.reshape(b, c, q, g, ns), h // g, axis=3)
    a_cs = jnp.cumsum(adt, axis=-1)
    lmat = jnp.exp(segsum(adt))
    y_diag = jnp.einsum('bclhn,bcshn,bhcls,bcshp->bclhp', cc, bc, lmat, xc)
    decay_states = jnp.exp(a_cs[..., -1:] - a_cs)
    states = jnp.einsum('bclhn,bhcl,bclhp->bchpn', bc, decay_states, xc)
    states = jnp.concatenate([init[:, None], states], axis=1)
    chunk_decay = jnp.exp(segsum(jnp.pad(a_cs[..., -1], ((0, 0), (0, 0), (1, 0)))))
    states = jnp.einsum('bhzc,bchpn->bzhpn', chunk_decay, states)
    y_off = jnp.einsum('bclhn,bchpn,bhcl->bclhp', cc, states[:, :-1], jnp.exp(a_cs))
    return (y_diag + y_off).reshape(b, n, h, p), states[:, -1]


def ssd_mixer(z_c, xbc_c, dt_c, z_l, xbc_l, dt_l, conv_w, conv_b, a_log, dt_bias, d_skip, norm_w, rows):
    z_l, xbc_l, dt_l = to_colmajor(z_l, rows), to_colmajor(xbc_l, rows), to_colmajor(dt_l, rows)

    def prep(xbc, dt_raw):
        xbc = jax.nn.silu(dwconv(xbc, conv_w, conv_b))
        xs, bm, cm = split_cols(xbc, (SSD_WIDTH, SSD_GROUPS * SSD_STATE, SSD_GROUPS * SSD_STATE))
        b, n = xs.shape[:2]
        dt = jax.nn.softplus(dt_raw.astype(F32).reshape(b, n, 2, SSD_HEADS) + dt_bias.astype(F32))
        return (xs.reshape(b, n, SSD_HEADS, SSD_HEAD_DIM), bm.reshape(b, n, SSD_GROUPS, SSD_STATE),
                cm.reshape(b, n, SSD_GROUPS, SSD_STATE), dt)
    xc, bc, cc, dtc = prep(xbc_c, dt_c)
    xl, bl, cl, dtl = prep(xbc_l, dt_l)
    dsk = d_skip.astype(F32)[:, None]
    y_c = dsk * xc
    y_l = dsk * xl
    zero = jnp.zeros((xc.shape[0], SSD_HEADS, SSD_HEAD_DIM, SSD_STATE), F32)
    for d in range(2):
        rev = d == 1
        a = -jnp.exp(a_log[d].astype(F32))
        yc, sc = ssd_chunked(flip_seq(xc, rev), flip_seq(dtc[:, :, d], rev), a, flip_seq(bc, rev), flip_seq(cc, rev), zero)
        yl, _ = ssd_chunked(flip_seq(xl, rev), flip_seq(dtl[:, :, d], rev), a, flip_seq(bl, rev), flip_seq(cl, rev), sc)
        y_c = y_c + flip_seq(yc, rev)
        y_l = y_l + flip_seq(yl, rev)

    def out(y, z):
        y = y.reshape(y.shape[0], y.shape[1], SSD_WIDTH) * jax.nn.silu(z.astype(F32))
        return rmsnorm(y) * norm_w
    return out(y_c, z_c), from_colmajor(out(y_l, z_l), rows)


def gla_chunked(q, k, v, la, init):
    b, n, h, dk = q.shape
    dv = v.shape[-1]
    qs = GLA_CHUNK
    c = n // qs
    q, k, la = (t.reshape(b, c, qs, h, dk) for t in (q, k, la))
    v = v.reshape(b, c, qs, h, dv)
    bcum = jnp.cumsum(la, axis=2)
    b_last = bcum[:, :, -1]
    q_dec = q * jnp.exp(bcum)
    k_inv = k * jnp.exp(-bcum)
    scores = jnp.einsum('bcthd,bcshd->bchts', q_dec, k_inv)
    scores = jnp.where(jnp.tril(jnp.ones((qs, qs), dtype=bool)), scores, 0.0)
    o_intra = jnp.einsum('bchts,bcshv->bcthv', scores, v)
    k_end = k * jnp.exp(b_last[:, :, None] - bcum)
    d_state = jnp.einsum('bcshd,bcshv->bchdv', k_end, v)
    gamma = jnp.exp(b_last)

    def step(s, inp):
        g_c, ds_c = inp
        return g_c[..., None] * s + ds_c, s
    final, s_prev = lax.scan(step, init, (jnp.moveaxis(gamma, 1, 0), jnp.moveaxis(d_state, 1, 0)))
    o_inter = jnp.einsum('bcthd,cbhdv->bcthv', q_dec, s_prev)
    return (o_intra + o_inter).reshape(b, n, h, dv), final


def gla_mixer(q_c, k_c, v_c, g_c, r_c, q_l, k_l, v_l, g_l, r_l, gate_w, gate_b, norm_w):
    def heads(t, dh):
        return t.astype(F32).reshape(t.shape[0], t.shape[1], GLA_HEADS, dh)

    def log_gates(r):
        r = r.astype(F32)
        return [heads(jax.nn.log_sigmoid(r[..., d * GLA_GATE_RANK:(d + 1) * GLA_GATE_RANK] @ gate_w[d] + gate_b[d]) / GLA_TAU, GLA_DK)
                for d in range(2)]
    qc, kc, vc, lac = heads(q_c, GLA_DK) * GLA_DK ** -0.5, heads(k_c, GLA_DK), heads(v_c, GLA_DV), log_gates(r_c)
    ql, kl, vl, lal = heads(q_l, GLA_DK) * GLA_DK ** -0.5, heads(k_l, GLA_DK), heads(v_l, GLA_DV), log_gates(r_l)
    zero = jnp.zeros((qc.shape[0], GLA_HEADS, GLA_DK, GLA_DV), F32)
    os_c, os_l = [], []
    for d in range(2):
        rev = d == 1
        oc, sc = gla_chunked(flip_seq(qc, rev), flip_seq(kc, rev), flip_seq(vc, rev), flip_seq(lac[d], rev), zero)
        ol, _ = gla_chunked(flip_seq(ql, rev), flip_seq(kl, rev), flip_seq(vl, rev), flip_seq(lal[d], rev), sc)
        os_c.append(flip_seq(oc, rev))
        os_l.append(flip_seq(ol, rev))

    def out(o, g):
        o = (rmsnorm(o) * norm_w).reshape(o.shape[0], o.shape[1], GLA_WIDTH)
        return o * jax.nn.silu(g.astype(F32))
    return out(os_c[0] + os_c[1], g_c), out(os_l[0] + os_l[1], g_l)


def hyena_filters(n, w1, b1, f1, w2, b2, f2, w3):
    t = jnp.linspace(0.0, 1.0, n, dtype=F32)[:, None]
    freqs = jnp.linspace(1e-4, HY_BANDS - 1, HY_BANDS, dtype=F32)
    ang = (2.0 * math.pi / n) * jnp.arange(n, dtype=F32)[:, None] * freqs[None, :]
    emb = jnp.concatenate([t, jnp.cos(ang), -jnp.sin(ang)], axis=-1)
    h = jnp.sin(f1.astype(F32) * (emb @ w1.astype(F32) + b1.astype(F32)))
    h = jnp.sin(f2.astype(F32) * (h @ w2.astype(F32) + b2.astype(F32)))
    h = (h @ w3.astype(F32)).reshape(n, HY_ORDER, 2, HY_WIDTH)
    deltas = jnp.abs(jnp.linspace(HY_MIN_DECAY, HY_MAX_DECAY, HY_WIDTH, dtype=F32))
    return h * jnp.exp(-t[:, :, None, None] * deltas)


def bidir_fftconv(u, h_f, h_b, bias):
    n = u.shape[1]
    circ = jnp.concatenate([h_f, jnp.zeros((1, h_f.shape[1]), F32), h_b[:0:-1]], axis=0)
    spec = jnp.fft.rfft(u, n=2 * n, axis=1) * jnp.fft.rfft(circ, axis=0)
    return jnp.fft.irfft(spec, n=2 * n, axis=1)[:, :n] + bias.astype(F32) * u


def hyena_stream(p, conv_w, conv_b, w1, b1, f1, w2, b2, f2, w3, bias):
    u = dwconv(p, conv_w, conv_b)
    v, x1, x2 = jnp.split(u, 3, axis=-1)
    filt = hyena_filters(p.shape[1], w1, b1, f1, w2, b2, f2, w3)
    y = v
    for o, gate in enumerate((x1, x2)):
        y = gate * bidir_fftconv(y, filt[:, o, 0], filt[:, o, 1], bias[o])
    return y


def merge_branches(ya, yb, yc, yd, gate_logits, w_branch, w_out):
    b, n, _ = gate_logits.shape
    gates = jax.nn.sigmoid(gate_logits.astype(F32)).reshape(b, n, N_BRANCH, D_MODEL)
    merged = gates[:, :, 0] * (ya @ w_branch[0])
    merged = merged + gates[:, :, 1] * (yb @ w_branch[1])
    merged = merged + gates[:, :, 2] * (yc @ w_branch[2])
    merged = merged + gates[:, :, 3] * (yd @ w_branch[3])
    return merged @ w_out


def hier_moe(h, group_w, group_b, expert_w, expert_b, w_gate, w_up, w_down):
    t = h.shape[0]
    glog = (h @ group_w + group_b).astype(F32)
    g_idx = jnp.argmax(glog, axis=-1)
    p_g = jnp.take_along_axis(jax.nn.softmax(glog, axis=-1), g_idx[:, None], axis=-1)
    elog = (h @ expert_w + expert_b).astype(F32).reshape(t, MOE_GROUPS, MOE_PER_GROUP)
    elog = jnp.take_along_axis(elog, g_idx[:, None, None], axis=1)[:, 0]
    top_v, top_i = lax.top_k(elog, MOE_TOPK)
    top_w = jax.nn.softmax(top_v, axis=-1) * p_g
    within = jnp.sum(jax.nn.one_hot(top_i, MOE_PER_GROUP, dtype=F32) * top_w[..., None], axis=1)
    combine = (jax.nn.one_hot(g_idx, MOE_GROUPS, dtype=F32)[:, :, None] * within[:, None, :]).reshape(t, MOE_EXPERTS)
    out = jnp.zeros((t, D_MODEL), F32)
    for e in range(MOE_EXPERTS):
        act = jax.nn.silu(h @ w_gate[e]) * (h @ w_up[e])
        out = out + combine[:, e:e + 1] * (act @ w_down[e])
    return out


def setup_inputs(seed: int = 0) -> dict:
    key = jax.random.key(seed)
    ks = iter(jax.random.split(key, 64))

    def nrm(shape, scale):
        return jax.random.normal(next(ks), shape, F32) * scale

    def uni(shape, lo, hi):
        return jax.random.uniform(next(ks), shape, F32, lo, hi)
    g5, p5, h5 = S5_GROUPS, S5_STATE, S5_GROUP
    dt0 = jnp.exp(uni((DEPTH, 2, SSD_HEADS), math.log(1e-3), math.log(1e-1)))
    return {
        'x': nrm((BATCH, SEQ, D_MODEL), 1.0),
        'c': nrm((BATCH, D_MODEL), 1.0),
        'ctx': nrm((BATCH, CTX_LEN, D_MODEL), 1.0),
        'c_ctx': nrm((D_MODEL,), 1.0),
        'ada_w': nrm((DEPTH, D_MODEL, 6 * D_MODEL), 0.5 * D_MODEL ** -0.5),
        'ada_b': nrm((DEPTH, 6 * D_MODEL), 0.02),
        'w_in': nrm((DEPTH, D_MODEL, IN_COLS), D_MODEL ** -0.5),
        's5_lambda_re': -0.5 + nrm((DEPTH, 2, g5, p5), 0.01),
        's5_lambda_im': math.pi * jnp.arange(p5, dtype=F32) + nrm((DEPTH, 2, g5, p5), 0.01),
        's5_log_step': uni((DEPTH, 2, g5), math.log(1e-3), math.log(1e-1)),
        's5_b_re': nrm((DEPTH, 2, g5, p5, h5), (2 * h5) ** -0.5),
        's5_b_im': nrm((DEPTH, 2, g5, p5, h5), (2 * h5) ** -0.5),
        's5_c_re': nrm((DEPTH, 2, g5, h5, p5), (2 * p5) ** -0.5 * 4.0),
        's5_c_im': nrm((DEPTH, 2, g5, h5, p5), (2 * p5) ** -0.5 * 4.0),
        's5_d': nrm((DEPTH, S5_WIDTH), 1.0),
        's5_glu_w': nrm((DEPTH, S5_WIDTH, S5_WIDTH), S5_WIDTH ** -0.5),
        'ssd_conv_w': nrm((DEPTH, SSD_CONV, SSD_CONV_CH), SSD_CONV ** -0.5),
        'ssd_conv_b': nrm((DEPTH, SSD_CONV_CH), 0.02),
        'ssd_a_log': jnp.log(uni((DEPTH, 2, SSD_HEADS), 1.0, 16.0)),
        'ssd_dt_bias': dt0 + jnp.log(-jnp.expm1(-dt0)),
        'ssd_d': 1.0 + nrm((DEPTH, SSD_HEADS), 0.1),
        'ssd_norm_w': 1.0 + nrm((DEPTH, SSD_WIDTH), 0.1),
        'gla_gate_w': nrm((DEPTH, 2, GLA_GATE_RANK, GLA_KEY_WIDTH), GLA_GATE_RANK ** -0.5),
        'gla_gate_b': nrm((DEPTH, 2, GLA_KEY_WIDTH), 0.1),
        'gla_norm_w': 1.0 + nrm((DEPTH, GLA_DV), 0.1),
        'hy_conv_w': nrm((DEPTH, HY_SHORT, (HY_ORDER + 1) * HY_WIDTH), HY_SHORT ** -0.5),
        'hy_conv_b': nrm((DEPTH, (HY_ORDER + 1) * HY_WIDTH), 0.02),
        'hy_w1': nrm((DEPTH, HY_EMB, HY_FILTER_DIM), HY_EMB ** -0.5),
        'hy_b1': nrm((DEPTH, HY_FILTER_DIM), 0.1),
        'hy_freq1': 1.0 + nrm((DEPTH, HY_FILTER_DIM), 0.1),
        'hy_w2': nrm((DEPTH, HY_FILTER_DIM, HY_FILTER_DIM), HY_FILTER_DIM ** -0.5),
        'hy_b2': nrm((DEPTH, HY_FILTER_DIM), 0.1),
        'hy_freq2': 1.0 + nrm((DEPTH, HY_FILTER_DIM), 0.1),
        'hy_w3': nrm((DEPTH, HY_FILTER_DIM, HY_ORDER * 2 * HY_WIDTH), 0.05 * HY_FILTER_DIM ** -0.5),
        'hy_bias': nrm((DEPTH, HY_ORDER, HY_WIDTH), 0.5),
        'w_branch': nrm((DEPTH, N_BRANCH, MIX_WIDTH, D_MODEL), MIX_WIDTH ** -0.5),
        'w_out': nrm((DEPTH, D_MODEL, D_MODEL), D_MODEL ** -0.5),
        'moe_group_w': nrm((DEPTH, D_MODEL, MOE_GROUPS), D_MODEL ** -0.5),
        'moe_group_b': nrm((DEPTH, MOE_GROUPS), 0.01),
        'moe_expert_w': nrm((DEPTH, D_MODEL, MOE_EXPERTS), D_MODEL ** -0.5),
        'moe_expert_b': nrm((DEPTH, MOE_EXPERTS), 0.01),
        'moe_w_gate': nrm((DEPTH, MOE_EXPERTS, D_MODEL, MOE_FF), D_MODEL ** -0.5),
        'moe_w_up': nrm((DEPTH, MOE_EXPERTS, D_MODEL, MOE_FF), D_MODEL ** -0.5),
        'moe_w_down': nrm((DEPTH, MOE_EXPERTS, MOE_FF, D_MODEL), MOE_FF ** -0.5),
        'final_norm_w': 1.0 + nrm((D_MODEL,), 0.1),
    }


def reference(x, c, ctx, c_ctx, ada_w, ada_b, w_in, s5_lambda_re, s5_lambda_im, s5_log_step, s5_b_re, s5_b_im,
              s5_c_re, s5_c_im, s5_d, s5_glu_w, ssd_conv_w, ssd_conv_b, ssd_a_log, ssd_dt_bias, ssd_d, ssd_norm_w,
              gla_gate_w, gla_gate_b, gla_norm_w, hy_conv_w, hy_conv_b, hy_w1, hy_b1, hy_freq1, hy_w2, hy_b2,
              hy_freq2, hy_w3, hy_bias, w_branch, w_out, moe_group_w, moe_group_b, moe_expert_w, moe_expert_b,
              moe_w_gate, moe_w_up, moe_w_down, final_norm_w):
    n_ctx = ctx.shape[1]
    rows = x.shape[1] // GRID_W
    x_lat = x
    x_ctx = ctx
    for li in range(DEPTH):
        last = li == DEPTH - 1
        mod = jax.nn.silu(c) @ ada_w[li] + ada_b[li]
        mod_c = jax.nn.silu(c_ctx) @ ada_w[li] + ada_b[li]
        sh1, sc1, g1, sh2, sc2, g2 = jnp.split(mod[:, None, :], 6, axis=-1)
        csh1, csc1, cg1, csh2, csc2, cg2 = jnp.split(mod_c, 6)
        h = jnp.concatenate([modulate(x_ctx, csh1, csc1), modulate(x_lat, sh1, sc1)], axis=1)
        p = h @ w_in[li]
        ac, zc, xbcc, dtc, qc, kc, vc, gc, rc, hc, gtc = split_cols(p[:, :n_ctx], IN_SIZES)
        al, zl, xbcl, dtl, ql, kl, vl, gl, rl, hl, gtl = split_cols(p[:, n_ctx:], IN_SIZES)
        ya_c, ya_l = s5_mixer(ac, al, s5_lambda_re[li], s5_lambda_im[li], s5_log_step[li], s5_b_re[li], s5_b_im[li],
                              s5_c_re[li], s5_c_im[li], s5_d[li], s5_glu_w[li])
        yb_c, yb_l = ssd_mixer(zc, xbcc, dtc, zl, xbcl, dtl, ssd_conv_w[li], ssd_conv_b[li], ssd_a_log[li],
                               ssd_dt_bias[li], ssd_d[li], ssd_norm_w[li], rows)
        yc_c, yc_l = gla_mixer(qc, kc, vc, gc, rc, ql, kl, vl, gl, rl, gla_gate_w[li], gla_gate_b[li], gla_norm_w[li])
        hy_args = (hy_conv_w[li], hy_conv_b[li], hy_w1[li], hy_b1[li], hy_freq1[li], hy_w2[li], hy_b2[li],
                   hy_freq2[li], hy_w3[li], hy_bias[li])
        yd_l = hyena_stream(hl, *hy_args)
        x_lat = x_lat + g1 * merge_branches(ya_l, yb_l, yc_l, yd_l, gtl, w_branch[li], w_out[li])
        if not last:
            yd_c = hyena_stream(hc, *hy_args)
            x_ctx = x_ctx + cg1 * merge_branches(ya_c, yb_c, yc_c, yd_c, gtc, w_branch[li], w_out[li])
            h2 = jnp.concatenate([modulate(x_ctx, csh2, csc2), modulate(x_lat, sh2, sc2)], axis=1)
        else:
            h2 = modulate(x_lat, sh2, sc2)
        y2 = hier_moe(h2.reshape(-1, D_MODEL), moe_group_w[li], moe_group_b[li], moe_expert_w[li], moe_expert_b[li],
                      moe_w_gate[li], moe_w_up[li], moe_w_down[li]).reshape(h2.shape)
        if not last:
            x_ctx = x_ctx + cg2 * y2[:, :n_ctx]
            x_lat = x_lat + g2 * y2[:, n_ctx:]
        else:
            x_lat = x_lat + g2 * y2
    return rmsnorm(x_lat) * final_norm_w
```

```python
import functools
import math

import numpy as np
import jax
import jax.numpy as jnp
from jax import lax
from jax.experimental import pallas as pl
from jax.experimental.pallas import tpu as pltpu

F32 = jnp.float32
BF16 = jnp.bfloat16
HI = lax.Precision.HIGHEST

D_MODEL = 2048
GRID_W = 64
NORM_EPS = 1e-6
N_BRANCH = 4
MIX_WIDTH = 768
S5_GROUP = 16
S5_GROUPS = MIX_WIDTH // S5_GROUP
S5_STATE = 64
SSD_HEAD_DIM = 64
SSD_HEADS = MIX_WIDTH // SSD_HEAD_DIM
SSD_GROUPS = 2
SSD_STATE = 64
SSD_CHUNK = 128
SSD_CONV_CH = MIX_WIDTH + 2 * SSD_GROUPS * SSD_STATE
GLA_HEADS = 6
GLA_KEY_WIDTH = MIX_WIDTH // 2
GLA_DK = GLA_KEY_WIDTH // GLA_HEADS
GLA_DV = MIX_WIDTH // GLA_HEADS
GLA_GATE_RANK = 16
GLA_TAU = 16.0
GLA_CHUNK = 64
HY_ORDER = 2
HY_FILTER_DIM = 64
HY_BANDS = 16
HY_EMB = 2 * HY_BANDS + 1
HY_MAX_DECAY = math.log(1e-2) / 0.3
HY_MIN_DECAY = math.log(1e-2) / 1.5
MOE_GROUPS = 4
MOE_PER_GROUP = 4
MOE_EXPERTS = MOE_GROUPS * MOE_PER_GROUP
MOE_FF = 1024

IN_SIZES = (MIX_WIDTH, MIX_WIDTH, SSD_CONV_CH, 2 * SSD_HEADS, GLA_KEY_WIDTH, GLA_KEY_WIDTH, MIX_WIDTH, MIX_WIDTH,
            2 * GLA_GATE_RANK, (HY_ORDER + 1) * MIX_WIDTH, N_BRANCH * D_MODEL)
IN_OFFS = tuple(int(v) for v in np.cumsum((0,) + IN_SIZES))

LANE = 128
ROW_TILE = 256
VMEM_LIMIT = 56 * 1024 * 1024

FFT_N1 = 64
FFT_N2 = 128
FFT_N = FFT_N1 * FFT_N2


def _cparams(sem):
    return pltpu.CompilerParams(dimension_semantics=sem, vmem_limit_bytes=VMEM_LIMIT)


def _pad_cols(a, width):
    return jnp.pad(a, [(0, 0)] * (a.ndim - 1) + [(0, width - a.shape[-1])])


def _mm_kernel(x_ref, w_ref, o_ref, *, precision):
    o_ref[...] = jnp.dot(x_ref[...], w_ref[...], preferred_element_type=F32,
                         precision=precision).astype(o_ref.dtype)


def matmul(x, w, tm, tn, out_dtype=F32, precision=None):
    m, k = x.shape
    n = w.shape[1]
    return pl.pallas_call(
        functools.partial(_mm_kernel, precision=precision),
        grid=(n // tn, m // tm),
        in_specs=[pl.BlockSpec((tm, k), lambda j, i: (i, 0)), pl.BlockSpec((k, tn), lambda j, i: (0, j))],
        out_specs=pl.BlockSpec((tm, tn), lambda j, i: (i, j)),
        out_shape=jax.ShapeDtypeStruct((m, n), out_dtype),
        compiler_params=_cparams(("parallel", "parallel")),
    )(x, w)


def _proj_kernel(x_ref, w_ref, *o_refs):
    acc = jnp.dot(x_ref[...], w_ref[...], preferred_element_type=F32)
    off = 0
    for o_ref in o_refs:
        wd = o_ref.shape[-1]
        o_ref[...] = acc[:, off:off + wd].astype(o_ref.dtype)
        off += wd


def proj_split(x, w, widths, tm=512):
    m, k = x.shape
    n = w.shape[1]
    assert sum(widths) == n
    return pl.pallas_call(
        _proj_kernel,
        grid=(m // tm,),
        in_specs=[pl.BlockSpec((tm, k), lambda i: (i, 0)), pl.BlockSpec((k, n), lambda i: (0, 0))],
        out_specs=[pl.BlockSpec((tm, wd), lambda i: (i, 0)) for wd in widths],
        out_shape=[jax.ShapeDtypeStruct((m, wd), F32) for wd in widths],
        compiler_params=_cparams(("parallel",)),
    )(x, w)


def _adaln_kernel(c_ref, w_ref, b_ref, o_ref):
    c = c_ref[...]
    s = c * jax.nn.sigmoid(c)
    o_ref[...] = jnp.dot(s, w_ref[...], preferred_element_type=F32, precision=HI) + b_ref[...]


def adaln(cc, ada_w, ada_b, tn=1024):
    depth, d, n = ada_w.shape
    return pl.pallas_call(
        _adaln_kernel,
        grid=(depth, n // tn),
        in_specs=[pl.BlockSpec((8, d), lambda l, j: (0, 0)),
                  pl.BlockSpec((None, d, tn), lambda l, j: (l, 0, j)),
                  pl.BlockSpec((None, 1, tn), lambda l, j: (l, 0, j))],
        out_specs=pl.BlockSpec((None, 8, tn), lambda l, j: (l, 0, j)),
        out_shape=jax.ShapeDtypeStruct((depth, 8, n), F32),
        compiler_params=_cparams(("parallel", "parallel")),
    )(cc, ada_w, ada_b.reshape(depth, 1, n))


def _seg_index(i, tiles_per_batch):
    b = i // tiles_per_batch
    return 2 * b + jnp.where(i % tiles_per_batch == 0, 0, 1)


def _modnorm_kernel(x_ref, sh_ref, sc_ref, o_ref):
    x = x_ref[...]
    ms = jnp.mean(x * x, axis=-1, keepdims=True)
    o_ref[...] = (x * lax.rsqrt(ms + NORM_EPS) * (1.0 + sc_ref[...]) + sh_ref[...]).astype(o_ref.dtype)


def modnorm(x, shift, scale, tiles_per_batch):
    t, d = x.shape
    seg = lambda i: (_seg_index(i, tiles_per_batch), 0, 0)
    return pl.pallas_call(
        _modnorm_kernel,
        grid=(t // ROW_TILE,),
        in_specs=[pl.BlockSpec((ROW_TILE, d), lambda i: (i, 0)),
                  pl.BlockSpec((None, 1, d), seg), pl.BlockSpec((None, 1, d), seg)],
        out_specs=pl.BlockSpec((ROW_TILE, d), lambda i: (i, 0)),
        out_shape=jax.ShapeDtypeStruct((t, d), BF16),
        compiler_params=_cparams(("parallel",)),
    )(x, shift, scale)


def _dwconv_kernel(x_ref, w_ref, b_ref, o_ref, *, n_ctx, silu):
    x = x_ref[...]
    t = x.shape[0]
    rows = lax.broadcasted_iota(jnp.int32, x.shape, 0)
    prev = jnp.where((rows == 0) | (rows == n_ctx), 0.0, pltpu.roll(x, 1, 0))
    nxt = jnp.where((rows == n_ctx - 1) | (rows == t - 1), 0.0, pltpu.roll(x, t - 1, 0))
    y = w_ref[0:1, :] * prev + w_ref[1:2, :] * x + w_ref[2:3, :] * nxt + b_ref[...]
    if silu:
        y = y * jax.nn.sigmoid(y)
    o_ref[...] = y


def dwconv_seq(x, w, b, n_ctx, silu, tc=LANE):
    bsz, t, ch = x.shape
    return pl.pallas_call(
        functools.partial(_dwconv_kernel, n_ctx=n_ctx, silu=silu),
        grid=(bsz, ch // tc),
        in_specs=[pl.BlockSpec((None, t, tc), lambda b_, j: (b_, 0, j)),
                  pl.BlockSpec((3, tc), lambda b_, j: (0, j)),
                  pl.BlockSpec((1, tc), lambda b_, j: (0, j))],
        out_specs=pl.BlockSpec((None, t, tc), lambda b_, j: (b_, 0, j)),
        out_shape=jax.ShapeDtypeStruct((bsz, t, ch), F32),
        compiler_params=_cparams(("parallel", "parallel")),
    )(x, w, b.reshape(1, ch))


S5_Q = 8
S5_TG = LANE // S5_GROUP
S5_TS = S5_TG * S5_STATE
S5_NT = MIX_WIDTH // LANE


def _cmul(ar, ai, br, bi):
    return ar * br - ai * bi, ar * bi + ai * br


def s5_weights(lam_re, lam_im, log_step, b_re, b_im, c_re, c_im, reverse):
    g, p, q = S5_GROUPS, S5_STATE, S5_Q
    step = jnp.exp(log_step)[:, None]
    mag = jnp.exp(lam_re * step)
    ab_re = mag * jnp.cos(lam_im * step)
    ab_im = mag * jnp.sin(lam_im * step)
    den = lam_re * lam_re + lam_im * lam_im
    zr, zi = _cmul(ab_re - 1.0, ab_im, lam_re, -lam_im)
    zr, zi = zr / den, zi / den
    bb_re, bb_im = _cmul(zr[..., None], zi[..., None], b_re, b_im)
    pw_re, pw_im = [jnp.ones_like(ab_re)], [jnp.zeros_like(ab_im)]
    for _ in range(q):
        r, i = _cmul(pw_re[-1], pw_im[-1], ab_re, ab_im)
        pw_re.append(r)
        pw_im.append(i)
    pw_re, pw_im = jnp.stack(pw_re), jnp.stack(pw_im)
    cp_re = c_re[None] * pw_re[:, :, None, :] - c_im[None] * pw_im[:, :, None, :]
    cp_im = c_re[None] * pw_im[:, :, None, :] + c_im[None] * pw_re[:, :, None, :]
    kd = (jnp.einsum('dghp,gpk->dghk', cp_re[:q], bb_re, precision=HI)
          - jnp.einsum('dghp,gpk->dghk', cp_im[:q], bb_im, precision=HI))
    s_idx = np.arange(q)[:, None]
    r_idx = np.arange(q)[None, :]
    lag = (s_idx - r_idx) if reverse else (r_idx - s_idx)
    valid = jnp.asarray(lag >= 0, F32)
    kt = kd[np.clip(lag, 0, q - 1)] * valid[:, :, None, None, None]
    kt = kt.reshape(q, q, S5_NT, S5_TG, S5_GROUP, S5_GROUP)
    eye = jnp.eye(S5_TG, dtype=F32)
    ktoep = jnp.einsum('srjghk,gG->jsgkrGh', kt, eye).reshape(S5_NT, q * LANE, q * LANE)
    e_of_s = (np.arange(q) if reverse else (q - 1 - np.arange(q)))
    pb_re = pw_re[e_of_s][..., None] * bb_re[None] - pw_im[e_of_s][..., None] * bb_im[None]
    pb_im = pw_re[e_of_s][..., None] * bb_im[None] + pw_im[e_of_s][..., None] * bb_re[None]
    pbc = jnp.stack([pb_re, pb_im]).reshape(2, q, S5_NT, S5_TG, p, S5_GROUP)
    bw = jnp.einsum('csjgpk,gG->jsgkcGp', pbc, eye).reshape(S5_NT, q * LANE, 2 * S5_TS)
    f_of_r = (q - np.arange(q)) if reverse else (np.arange(q) + 1)
    cpf = jnp.stack([cp_re[f_of_r], -cp_im[f_of_r]]).reshape(2, q, S5_NT, S5_TG, S5_GROUP, p)
    cw = jnp.einsum('crjghp,gG->jcgprGh', cpf, eye).reshape(S5_NT, 2 * S5_TS, q * LANE)
    ar, ai = pw_re[q], pw_im[q]
    lv = []
    for _ in range(16):
        lv.append(jnp.concatenate([ar.reshape(S5_NT, S5_TS), ai.reshape(S5_NT, S5_TS)], axis=-1))
        ar, ai = _cmul(ar, ai, ar, ai)
    a2 = jnp.stack(lv, axis=1)
    return ktoep.astype(BF16), bw.astype(BF16), cw.astype(BF16), a2


def _s5_seg_scan(xr, xi, a2_ref, reverse):
    n = xr.shape[0]
    rows = lax.broadcasted_iota(jnp.int32, xr.shape, 0)
    k, shift = 0, 1
    while shift < n:
        ar = a2_ref[k:k + 1, :S5_TS]
        ai = a2_ref[k:k + 1, S5_TS:]
        if reverse:
            keep = rows < n - shift
            sr, si = pltpu.roll(xr, n - shift, 0), pltpu.roll(xi, n - shift, 0)
        else:
            keep = rows >= shift
            sr, si = pltpu.roll(xr, shift, 0), pltpu.roll(xi, shift, 0)
        sr = jnp.where(keep, sr, 0.0)
        si = jnp.where(keep, si, 0.0)
        xr, xi = xr + ar * sr - ai * si, xi + ar * si + ai * sr
        k += 1
        shift *= 2
    return xr, xi


def _s5_entering(sr, si, carry, a2_ref, reverse):
    n = sr.shape[0]
    rows = lax.broadcasted_iota(jnp.int32, sr.shape, 0)
    first = (n - 1) if reverse else 0
    if carry is not None:
        cr, ci = carry
        ar, ai = a2_ref[0:1, :S5_TS], a2_ref[0:1, S5_TS:]
        ir, ii = _cmul(ar, ai, cr, ci)
        sr = jnp.where(rows == first, sr + ir, sr)
        si = jnp.where(rows == first, si + ii, si)
    hr, hi = _s5_seg_scan(sr, si, a2_ref, reverse)
    if reverse:
        out = (hr[0:1], hi[0:1])
        er, ei = pltpu.roll(hr, n - 1, 0), pltpu.roll(hi, n - 1, 0)
    else:
        out = (hr[n - 1:n], hi[n - 1:n])
        er, ei = pltpu.roll(hr, 1, 0), pltpu.roll(hi, 1, 0)
    if carry is None:
        er = jnp.where(rows == first, 0.0, er)
        ei = jnp.where(rows == first, 0.0, ei)
    else:
        er = jnp.where(rows == first, cr, er)
        ei = jnp.where(rows == first, ci, ei)
    return er, ei, out


def _s5_kernel(u_ref, kt_ref, bw_ref, cw_ref, a2_ref, y_ref, *, n_ctx_rows, reverse):
    n = u_ref.shape[0] // S5_Q
    x = jnp.concatenate([u_ref[pl.ds(s, n, stride=S5_Q), :] for s in range(S5_Q)], axis=1).astype(BF16)
    y = jnp.dot(x, kt_ref[...], preferred_element_type=F32)
    st = jnp.dot(x, bw_ref[...], preferred_element_type=F32)
    sr, si = st[:, :S5_TS], st[:, S5_TS:]
    cr, ci, carry = _s5_entering(sr[:n_ctx_rows], si[:n_ctx_rows], None, a2_ref, reverse)
    lr, li, _ = _s5_entering(sr[n_ctx_rows:], si[n_ctx_rows:], carry, a2_ref, reverse)
    h = jnp.concatenate([jnp.concatenate([cr, lr], axis=0), jnp.concatenate([ci, li], axis=0)], axis=1)
    y = y + jnp.dot(h.astype(BF16), cw_ref[...], preferred_element_type=F32)
    for r in range(S5_Q):
        y_ref[pl.ds(r, n, stride=S5_Q), :] = y[:, r * LANE:(r + 1) * LANE]


def s5_scan(u, weights, n_ctx, reverse):
    bsz, t, ch = u.shape
    ktoep, bw, cw, a2 = weights
    qk = S5_Q * LANE
    return pl.pallas_call(
        functools.partial(_s5_kernel, n_ctx_rows=n_ctx // S5_Q, reverse=reverse),
        grid=(S5_NT, bsz),
        in_specs=[pl.BlockSpec((None, t, LANE), lambda j, b: (b, 0, j)),
                  pl.BlockSpec((None, qk, qk), lambda j, b: (j, 0, 0)),
                  pl.BlockSpec((None, qk, 2 * S5_TS), lambda j, b: (j, 0, 0)),
                  pl.BlockSpec((None, 2 * S5_TS, qk), lambda j, b: (j, 0, 0)),
                  pl.BlockSpec((None, 16, 2 * S5_TS), lambda j, b: (j, 0, 0))],
        out_specs=pl.BlockSpec((None, t, LANE), lambda j, b: (b, 0, j)),
        out_shape=jax.ShapeDtypeStruct((bsz, t, ch), F32),
        compiler_params=_cparams(("parallel", "parallel")),
    )(u, ktoep, bw, cw, a2)


def _s5_out_kernel(yf_ref, yb_ref, u_ref, d_ref, w_ref, o_ref):
    y = yf_ref[...] + yb_ref[...] + d_ref[...] * u_ref[...]
    g = jax.nn.gelu(y)
    z = jnp.dot(g.astype(BF16), w_ref[...], preferred_element_type=F32)
    o_ref[...] = (g * jax.nn.sigmoid(z)).astype(o_ref.dtype)


def s5_out(yf, yb, u, d_skip, glu_w, tm=512):
    t, ch = u.shape
    row = pl.BlockSpec((tm, ch), lambda i: (i, 0))
    return pl.pallas_call(
        _s5_out_kernel,
        grid=(t // tm,),
        in_specs=[row, row, row, pl.BlockSpec((1, ch), lambda i: (0, 0)), pl.BlockSpec((ch, ch), lambda i: (0, 0))],
        out_specs=row,
        out_shape=jax.ShapeDtypeStruct((t, ch), BF16),
        compiler_params=_cparams(("parallel",)),
    )(yf, yb, u, d_skip.reshape(1, ch), glu_w.astype(BF16))


def _softplus(x):
    return jnp.maximum(x, 0.0) + jnp.log1p(jnp.exp(-jnp.abs(x)))


def _ssd_kernel(xbc_ref, dt_ref, bias_ref, a_ref, tri_ref, y_ref, st_ref, *, d, reverse):
    q = SSD_CHUNK
    hp = SSD_HEAD_DIM
    hg = SSD_HEADS // SSD_GROUPS
    gw = hg * hp

    @pl.when(pl.program_id(1) == 0)
    def _():
        st_ref[...] = jnp.zeros_like(st_ref)

    dt = _softplus(dt_ref[...] + bias_ref[...])
    adt = dt * a_ref[...]
    cs = jnp.dot(tri_ref[...], adt, preferred_element_type=F32, precision=HI)
    tot = jnp.sum(adt, axis=0, keepdims=True)
    cs_t = cs.T
    ecs = jnp.exp(cs)
    dec = jnp.exp(tot - cs)
    etot = jnp.exp(tot)
    xs = xbc_ref[:, :MIX_WIDTH]
    bm = xbc_ref[:, MIX_WIDTH:MIX_WIDTH + SSD_GROUPS * SSD_STATE]
    cm = xbc_ref[:, MIX_WIDTH + SSD_GROUPS * SSD_STATE:]
    bm_t = bm.T
    li = lax.broadcasted_iota(jnp.int32, (q, q), 0)
    si = lax.broadcasted_iota(jnp.int32, (q, q), 1)
    mask = (si >= li) if reverse else (si <= li)
    for g in range(SSD_GROUPS):
        cg = cm[:, g * SSD_STATE:(g + 1) * SSD_STATE].astype(BF16)
        bg = bm[:, g * SSD_STATE:(g + 1) * SSD_STATE].astype(BF16)
        gmat = lax.dot_general(cg, bg, (((1,), (1,)), ((), ())), preferred_element_type=F32)
        yoff = jnp.dot(cg, st_ref[g].astype(BF16), preferred_element_type=F32)
        xw = []
        for hl in range(hg):
            h = g * hg + hl
            col = d * SSD_HEADS + h
            seg = cs[:, col:col + 1] - cs_t[col:col + 1, :]
            lmat = jnp.exp(jnp.where(mask, seg, -jnp.inf))
            xdt = xs[:, h * hp:(h + 1) * hp] * dt[:, col:col + 1]
            yd = jnp.dot((gmat * lmat).astype(BF16), xdt.astype(BF16), preferred_element_type=F32)
            y_ref[:, h * hp:(h + 1) * hp] = yd + yoff[:, hl * hp:(hl + 1) * hp] * ecs[:, col:col + 1]
            xw.append(xdt * dec[:, col:col + 1])
        xw = jnp.concatenate(xw, axis=1).astype(BF16)
        new = jnp.dot(bm_t[g * SSD_STATE:(g + 1) * SSD_STATE, :].astype(BF16), xw, preferred_element_type=F32)
        for hl in range(hg):
            col = d * SSD_HEADS + g * hg + hl
            sl = slice(hl * hp, (hl + 1) * hp)
            st_ref[g, :, sl] = st_ref[g, :, sl] * etot[:, col:col + 1] + new[:, sl]


def ssd_scan(xbc, dt_raw, dt_bias_row, a_row, n_ctx, d):
    bsz, t, _ = xbc.shape
    reverse = d == 1
    q = SSD_CHUNK
    nchunk, nctx = t // q, n_ctx // q
    tri = np.tril(np.ones((q, q), np.float32))
    tri = jnp.asarray(tri.T if reverse else tri)
    if reverse:
        cidx = lambda i: jnp.where(i < nctx, nctx - 1 - i, nchunk + nctx - 1 - i)
    else:
        cidx = lambda i: i
    return pl.pallas_call(
        functools.partial(_ssd_kernel, d=d, reverse=reverse),
        grid=(bsz, nchunk),
        in_specs=[pl.BlockSpec((None, q, SSD_CONV_CH), lambda b, i: (b, cidx(i), 0)),
                  pl.BlockSpec((None, q, LANE), lambda b, i: (b, cidx(i), 0)),
                  pl.BlockSpec((1, LANE), lambda b, i: (0, 0)),
                  pl.BlockSpec((1, LANE), lambda b, i: (0, 0)),
                  pl.BlockSpec((q, q), lambda b, i: (0, 0))],
        out_specs=pl.BlockSpec((None, q, MIX_WIDTH), lambda b, i: (b, cidx(i), 0)),
        out_shape=jax.ShapeDtypeStruct((bsz, t, MIX_WIDTH), F32),
        scratch_shapes=[pltpu.VMEM((SSD_GROUPS, SSD_STATE, (SSD_HEADS // SSD_GROUPS) * SSD_HEAD_DIM), F32)],
        compiler_params=_cparams(("parallel", "arbitrary")),
    )(xbc, dt_raw, dt_bias_row, a_row, tri)


def _ssd_out_kernel(yf_ref, yb_ref, xbc_ref, z_ref, d_ref, nw_ref, o_ref):
    z = z_ref[...]
    y = (d_ref[...] * xbc_ref[...] + yf_ref[...] + yb_ref[...]) * (z * jax.nn.sigmoid(z))
    ms = jnp.mean(y * y, axis=-1, keepdims=True)
    o_ref[...] = (y * lax.rsqrt(ms + NORM_EPS) * nw_ref[...]).astype(o_ref.dtype)


def ssd_out(yf, yb, xbc, z, d_row, norm_w, tm=512):
    t, ch = z.shape
    row = pl.BlockSpec((tm, ch), lambda i: (i, 0))
    vec = pl.BlockSpec((1, ch), lambda i: (0, 0))
    return pl.pallas_call(
        _ssd_out_kernel,
        grid=(t // tm,),
        in_specs=[row, row, pl.BlockSpec((tm, ch), lambda i: (i, 0)), row, vec, vec],
        out_specs=row,
        out_shape=jax.ShapeDtypeStruct((t, ch), BF16),
        compiler_params=_cparams(("parallel",)),
    )(yf, yb, xbc, z, d_row, norm_w.reshape(1, ch))


def _gla_kernel(q_ref, k_ref, v_ref, r_ref, wg_ref, gb_ref, tri_ref, o_ref, st_ref, *, reverse):
    qs = GLA_CHUNK

    @pl.when(pl.program_id(1) == 0)
    def _():
        st_ref[...] = jnp.zeros_like(st_ref)

    zl = jnp.dot(r_ref[...], wg_ref[...], preferred_element_type=F32, precision=HI) + gb_ref[...]
    la = (jnp.minimum(zl, 0.0) - jnp.log1p(jnp.exp(-jnp.abs(zl)))) / GLA_TAU
    bc = jnp.dot(tri_ref[...], la, preferred_element_type=F32, precision=HI)
    tot = jnp.sum(la, axis=0, keepdims=True)
    qd = (q_ref[...] * (GLA_DK ** -0.5) * jnp.exp(bc)).astype(BF16)
    ki = (k_ref[...] * jnp.exp(-bc)).astype(BF16)
    ke = (k_ref[...] * jnp.exp(tot - bc)).astype(BF16)
    gam = jnp.exp(tot)
    ti = lax.broadcasted_iota(jnp.int32, (qs, qs), 0)
    si = lax.broadcasted_iota(jnp.int32, (qs, qs), 1)
    mask = (si >= ti) if reverse else (si <= ti)
    for h in range(GLA_HEADS):
        ks = slice(h * GLA_DK, (h + 1) * GLA_DK)
        vs = slice(h * GLA_DV, (h + 1) * GLA_DV)
        vh = v_ref[:, vs].astype(BF16)
        sc = lax.dot_general(qd[:, ks], ki[:, ks], (((1,), (1,)), ((), ())), preferred_element_type=F32)
        sc = jnp.where(mask, sc, 0.0).astype(BF16)
        o_intra = jnp.dot(sc, vh, preferred_element_type=F32)
        st = st_ref[:, ks]
        o_inter = lax.dot_general(qd[:, ks], st.astype(BF16), (((1,), (1,)), ((), ())), preferred_element_type=F32)
        o_ref[:, vs] = o_intra + o_inter
        ds = lax.dot_general(vh, ke[:, ks], (((0,), (0,)), ((), ())), preferred_element_type=F32)
        st_ref[:, ks] = st * gam[:, ks] + ds


def gla_scan(q, k, v, r, wg, gb, n_ctx, reverse):
    bsz, t, _ = q.shape
    qs = GLA_CHUNK
    nchunk, nctx = t // qs, n_ctx // qs
    tri = np.tril(np.ones((qs, qs), np.float32))
    tri = jnp.asarray(tri.T if reverse else tri)
    if reverse:
        cidx = lambda i: jnp.where(i < nctx, nctx - 1 - i, nchunk + nctx - 1 - i)
    else:
        cidx = lambda i: i
    blk = lambda w: pl.BlockSpec((None, qs, w), lambda b, i: (b, cidx(i), 0))
    return pl.pallas_call(
        functools.partial(_gla_kernel, reverse=reverse),
        grid=(bsz, nchunk),
        in_specs=[blk(GLA_KEY_WIDTH), blk(GLA_KEY_WIDTH), blk(MIX_WIDTH), blk(LANE),
                  pl.BlockSpec((LANE, GLA_KEY_WIDTH), lambda b, i: (0, 0)),
                  pl.BlockSpec((1, GLA_KEY_WIDTH), lambda b, i: (0, 0)),
                  pl.BlockSpec((qs, qs), lambda b, i: (0, 0))],
        out_specs=blk(MIX_WIDTH),
        out_shape=jax.ShapeDtypeStruct((bsz, t, MIX_WIDTH), F32),
        scratch_shapes=[pltpu.VMEM((GLA_DV, GLA_KEY_WIDTH), F32)],
        compiler_params=_cparams(("parallel", "arbitrary")),
    )(q, k, v, r, wg, gb, tri)


def _gla_out_kernel(of_ref, ob_ref, g_ref, nw_ref, o_ref):
    g = g_ref[...]
    sg = g * jax.nn.sigmoid(g)
    for h in range(GLA_HEADS):
        vs = slice(h * GLA_DV, (h + 1) * GLA_DV)
        o = of_ref[:, vs] + ob_ref[:, vs]
        ms = jnp.mean(o * o, axis=-1, keepdims=True)
        o_ref[:, vs] = (o * lax.rsqrt(ms + NORM_EPS) * nw_ref[...] * sg[:, vs]).astype(o_ref.dtype)


def gla_out(of, ob, g, norm_w, tm=512):
    t, ch = g.shape
    row = pl.BlockSpec((tm, ch), lambda i: (i, 0))
    return pl.pallas_call(
        _gla_out_kernel,
        grid=(t // tm,),
        in_specs=[row, row, row, pl.BlockSpec((1, GLA_DV), lambda i: (0, 0))],
        out_specs=row,
        out_shape=jax.ShapeDtypeStruct((t, ch), BF16),
        compiler_params=_cparams(("parallel",)),
    )(of, ob, g, norm_w.reshape(1, GLA_DV))


def _hy_emb(n):
    t = np.linspace(0.0, 1.0, n)[:, None]
    freqs = np.linspace(1e-4, HY_BANDS - 1, HY_BANDS)
    ang = (2.0 * math.pi / n) * np.arange(n)[:, None] * freqs[None, :]
    emb = np.concatenate([t, np.cos(ang), -np.sin(ang)], axis=-1)
    return np.pad(emb, ((0, 0), (0, LANE - HY_EMB))).astype(np.float32)


def _hy_filter_kernel(emb_ref, w1_ref, b1_ref, f1_ref, w2_ref, b2_ref, f2_ref, w3_ref, dl_ref, o_ref):
    emb = emb_ref[...]
    h = jnp.sin(f1_ref[...] * (jnp.dot(emb, w1_ref[...], preferred_element_type=F32, precision=HI) + b1_ref[...]))
    h = jnp.sin(f2_ref[...] * (jnp.dot(h, w2_ref[...], preferred_element_type=F32, precision=HI) + b2_ref[...]))
    h = jnp.dot(h, w3_ref[...], preferred_element_type=F32, precision=HI)
    o_ref[...] = h * jnp.exp(-emb[:, 0:1] * dl_ref[...])


def hyena_filters(n, w1, b1, f1, w2, b2, f2, w3, tm=256):
    nf = w3.shape[1]
    emb = jnp.asarray(_hy_emb(n))
    deltas = np.abs(np.linspace(HY_MIN_DECAY, HY_MAX_DECAY, MIX_WIDTH)).astype(np.float32)
    dl = jnp.asarray(np.tile(deltas, nf // MIX_WIDTH)[None, :])
    w1p = jnp.pad(w1, ((0, LANE - HY_EMB), (0, 0)))
    fd = HY_FILTER_DIM
    vec = lambda w: pl.BlockSpec((1, w), lambda i: (0, 0))
    return pl.pallas_call(
        _hy_filter_kernel,
        grid=(n // tm,),
        in_specs=[pl.BlockSpec((tm, LANE), lambda i: (i, 0)),
                  pl.BlockSpec((LANE, fd), lambda i: (0, 0)), vec(fd), vec(fd),
                  pl.BlockSpec((fd, fd), lambda i: (0, 0)), vec(fd), vec(fd),
                  pl.BlockSpec((fd, nf), lambda i: (0, 0)), vec(nf)],
        out_specs=pl.BlockSpec((tm, nf), lambda i: (i, 0)),
        out_shape=jax.ShapeDtypeStruct((n, nf), F32),
        compiler_params=_cparams(("parallel",)),
    )(emb, w1p, b1.reshape(1, fd), f1.reshape(1, fd), w2, b2.reshape(1, fd), f2.reshape(1, fd), w3, dl)


def _dft_consts():
    n1, n2, n = FFT_N1, FFT_N2, FFT_N
    k1 = np.arange(n1)
    f1 = np.exp(-2j * np.pi * np.outer(k1, k1) / n1)
    tw = np.exp(-2j * np.pi * np.outer(k1, np.arange(n2)) / n)
    k2 = np.arange(n2)
    f2 = np.exp(-2j * np.pi * np.outer(k2, k2) / n2)
    w2 = np.block([[f2.real, -f2.imag], [f2.imag, f2.real]])
    w2c = np.block([[f2.real, f2.imag], [-f2.imag, f2.real]])
    c = lambda a: np.ascontiguousarray(a, dtype=np.float32)
    return dict(f1_re=c(f1.real), f1_im=c(f1.imag), tw_re=c(tw.real[:, :, None]), tw_im=c(tw.imag[:, :, None]),
                w2=c(w2), w2c=c(w2c),
                inv_re=c(f1.real[:n1 // 2, :] / n), inv_im=c(f1.imag[:n1 // 2, :] / n))


_DFT = _dft_consts()


def _dft_a_kernel(x_ref, fre_ref, fim_ref, ore_ref, oim_ref):
    x = x_ref[...]
    ore_ref[...] = jnp.dot(fre_ref[...], x, preferred_element_type=F32, precision=HI)
    oim_ref[...] = jnp.dot(fim_ref[...], x, preferred_element_type=F32, precision=HI)


def dft_stage_a(x, tc=8192):
    s, k, cols = x.shape
    fre = jnp.asarray(_DFT['f1_re'][:, :k])
    fim = jnp.asarray(_DFT['f1_im'][:, :k])
    outs = pl.BlockSpec((None, FFT_N1, tc), lambda s_, j: (s_, 0, j))
    cst = pl.BlockSpec((FFT_N1, k), lambda s_, j: (0, 0))
    return pl.pallas_call(
        _dft_a_kernel,
        grid=(s, cols // tc),
        in_specs=[pl.BlockSpec((None, k, tc), lambda s_, j: (s_, 0, j)), cst, cst],
        out_specs=[outs, outs],
        out_shape=[jax.ShapeDtypeStruct((s, FFT_N1, cols), F32)] * 2,
        compiler_params=_cparams(("parallel", "parallel")),
    )(x, fre, fim)


def _dft_b_fwd(xr, xi, twr, twi, w2_ref):
    ar = xr * twr - xi * twi
    ai = xr * twi + xi * twr
    ys = jnp.dot(w2_ref[...], jnp.concatenate([ar, ai], axis=0).astype(BF16), preferred_element_type=F32)
    return ys[:FFT_N2], ys[FFT_N2:]


def _dft_b_spec_kernel(xr_ref, xi_ref, twr_ref, twi_ref, w2_ref, ore_ref, oim_ref):
    yr, yi = _dft_b_fwd(xr_ref[...], xi_ref[...], twr_ref[...], twi_ref[...], w2_ref)
    ore_ref[...] = yr
    oim_ref[...] = yi


def _dft_b_conv_kernel(xr_ref, xi_ref, hr_ref, hi_ref, twr_ref, twi_ref, w2_ref, w2c_ref, ore_ref, oim_ref):
    twr, twi = twr_ref[...], twi_ref[...]
    yr, yi = _dft_b_fwd(xr_ref[...], xi_ref[...], twr, twi, w2_ref)
    hr, hi = hr_ref[...], hi_ref[...]
    zr = yr * hr - yi * hi
    zi = yr * hi + yi * hr
    vs = jnp.dot(w2c_ref[...], jnp.concatenate([zr, zi], axis=0).astype(BF16), preferred_element_type=F32)
    vr, vi = vs[:FFT_N2], vs[FFT_N2:]
    ore_ref[...] = vr * twr + vi * twi
    oim_ref[...] = vi * twr - vr * twi


def _dft_b_common(ch):
    tw = pl.BlockSpec((None, FFT_N2, 1), lambda s, k: (k, 0, 0))
    mat = pl.BlockSpec((2 * FFT_N2, 2 * FFT_N2), lambda s, k: (0, 0))
    dat = pl.BlockSpec((None, None, FFT_N2, ch), lambda s, k: (s, k, 0, 0))
    return tw, mat, dat


def dft_stage_b_spec(xr, xi):
    s, _, _, ch = xr.shape
    tw, mat, dat = _dft_b_common(ch)
    return pl.pallas_call(
        _dft_b_spec_kernel,
        grid=(s, FFT_N1),
        in_specs=[dat, dat, tw, tw, mat],
        out_specs=[dat, dat],
        out_shape=[jax.ShapeDtypeStruct(xr.shape, F32)] * 2,
        compiler_params=_cparams(("parallel", "parallel")),
    )(xr, xi, jnp.asarray(_DFT['tw_re']), jnp.asarray(_DFT['tw_im']), jnp.asarray(_DFT['w2'], BF16))


def dft_stage_b_conv(xr, xi, hr, hi, order, seq_per_filter):
    s, _, _, ch = xr.shape
    tw, mat, dat = _dft_b_common(ch)
    hsp = pl.BlockSpec((None, None, FFT_N2, ch), lambda s_, k: (s_ // seq_per_filter, k, 0, order))
    return pl.pallas_call(
        _dft_b_conv_kernel,
        grid=(s, FFT_N1),
        in_specs=[dat, dat, hsp, hsp, tw, tw, mat, mat],
        out_specs=[dat, dat],
        out_shape=[jax.ShapeDtypeStruct(xr.shape, F32)] * 2,
        compiler_params=_cparams(("parallel", "parallel")),
    )(xr, xi, hr, hi, jnp.asarray(_DFT['tw_re']), jnp.asarray(_DFT['tw_im']),
      jnp.asarray(_DFT['w2'], BF16), jnp.asarray(_DFT['w2c'], BF16))


def _dft_inv_kernel(zr_ref, zi_ref, fre_ref, fim_ref, u_ref, gate_ref, bias_ref, o_ref):
    y = (jnp.dot(fre_ref[...], zr_ref[...], preferred_element_type=F32, precision=HI)
         + jnp.dot(fim_ref[...], zi_ref[...], preferred_element_type=F32, precision=HI))
    o_ref[...] = gate_ref[...] * (y + bias_ref[...] * u_ref[...])


def dft_stage_inv(zr, zi, u, gate, bias_row, tc=8192):
    s, _, cols = zr.shape
    half = FFT_N1 // 2
    zin = pl.BlockSpec((None, FFT_N1, tc), lambda s_, j: (s_, 0, j))
    cst = pl.BlockSpec((half, FFT_N1), lambda s_, j: (0, 0))
    dat = pl.BlockSpec((None, half, tc), lambda s_, j: (s_, 0, j))
    return pl.pallas_call(
        _dft_inv_kernel,
        grid=(s, cols // tc),
        in_specs=[zin, zin, cst, cst, dat, dat, pl.BlockSpec((1, tc), lambda s_, j: (0, j))],
        out_specs=dat,
        out_shape=jax.ShapeDtypeStruct((s, half, cols), F32),
        compiler_params=_cparams(("parallel", "parallel")),
    )(zr, zi, jnp.asarray(_DFT['inv_re']), jnp.asarray(_DFT['inv_im']), u, gate, bias_row)


def hyena_spectrum(filt_list):
    circs = []
    for filt in filt_list:
        n = filt.shape[0]
        f4 = filt.reshape(n, HY_ORDER, 2, MIX_WIDTH)
        hf = f4[:, :, 0].reshape(n, HY_ORDER * MIX_WIDTH)
        hb = f4[:, :, 1].reshape(n, HY_ORDER * MIX_WIDTH)
        mid = jnp.zeros((FFT_N - 2 * n + 1, hf.shape[1]), F32)
        circs.append(jnp.concatenate([hf, mid, hb[:0:-1]], axis=0))
    circ = jnp.stack(circs)
    s, _, ch = circ.shape
    xr, xi = dft_stage_a(circ.reshape(s, FFT_N1, FFT_N2 * ch))
    return dft_stage_b_spec(xr.reshape(s, FFT_N1, FFT_N2, ch), xi.reshape(s, FFT_N1, FFT_N2, ch))


def hyena_conv(u_seq, hr, hi, bias, seq_per_filter):
    s, n, _ = u_seq.shape
    half = FFT_N1 // 2
    cols = FFT_N2 * MIX_WIDTH
    y = u_seq[:, :, :MIX_WIDTH].reshape(s, half, cols)
    for o in range(HY_ORDER):
        gate = u_seq[:, :, (o + 1) * MIX_WIDTH:(o + 2) * MIX_WIDTH].reshape(s, half, cols)
        xr, xi = dft_stage_a(y)
        zr, zi = dft_stage_b_conv(xr.reshape(s, FFT_N1, FFT_N2, MIX_WIDTH), xi.reshape(s, FFT_N1, FFT_N2, MIX_WIDTH),
                                  hr, hi, o, seq_per_filter)
        bias_row = jnp.tile(bias[o], FFT_N2)[None, :]
        y = dft_stage_inv(zr.reshape(s, FFT_N1, cols), zi.reshape(s, FFT_N1, cols), y, gate, bias_row)
    return y.reshape(s, n, MIX_WIDTH)


def _merge_kernel(ya_ref, yb_ref, yc_ref, yd_ref, ga_ref, gb_ref, gc_ref, gd_ref, w_ref, o_ref):
    acc = None
    for b, (y_ref, g_ref) in enumerate(((ya_ref, ga_ref), (yb_ref, gb_ref), (yc_ref, gc_ref), (yd_ref, gd_ref))):
        t = jax.nn.sigmoid(g_ref[...]) * jnp.dot(y_ref[...], w_ref[b], preferred_element_type=F32)
        acc = t if acc is None else acc + t
    o_ref[...] = acc.astype(o_ref.dtype)


def merge_branches(ys, gates, w_branch, tm=512, tn=512):
    t, ch = ys[0].shape
    nj = D_MODEL // tn
    yspec = pl.BlockSpec((tm, ch), lambda j, i: (i, 0))
    gspecs = [pl.BlockSpec((tm, tn), functools.partial(lambda j, i, b: (i, b * nj + j), b=b)) for b in range(N_BRANCH)]
    return pl.pallas_call(
        _merge_kernel,
        grid=(nj, t // tm),
        in_specs=[yspec] * 4 + gspecs + [pl.BlockSpec((N_BRANCH, ch, tn), lambda j, i: (0, 0, j))],
        out_specs=pl.BlockSpec((tm, tn), lambda j, i: (i, j)),
        out_shape=jax.ShapeDtypeStruct((t, D_MODEL), BF16),
        compiler_params=_cparams(("parallel", "parallel")),
    )(*ys, gates, gates, gates, gates, w_branch)


def _mm_resid_kernel(a_ref, w_ref, x_ref, g_ref, o_ref):
    o_ref[...] = x_ref[...] + g_ref[...] * jnp.dot(a_ref[...], w_ref[...], preferred_element_type=F32)


def matmul_gated_residual(a, w, x, gate, tiles_per_batch, tn=512):
    t, k = a.shape
    n = w.shape[1]
    return pl.pallas_call(
        _mm_resid_kernel,
        grid=(n // tn, t // ROW_TILE),
        in_specs=[pl.BlockSpec((ROW_TILE, k), lambda j, i: (i, 0)),
                  pl.BlockSpec((k, tn), lambda j, i: (0, j)),
                  pl.BlockSpec((ROW_TILE, tn), lambda j, i: (i, j)),
                  pl.BlockSpec((None, 1, tn), lambda j, i: (_seg_index(i, tiles_per_batch), 0, j))],
        out_specs=pl.BlockSpec((ROW_TILE, tn), lambda j, i: (i, j)),
        out_shape=jax.ShapeDtypeStruct((t, n), F32),
        compiler_params=_cparams(("parallel", "parallel")),
    )(a, w, x, gate)


ROUTE_OFF = MOE_GROUPS


def _router_kernel(x_ref, sh_ref, sc_ref, wr_ref, br_ref, h_ref, comb_ref):
    x = x_ref[...]
    ms = jnp.mean(x * x, axis=-1, keepdims=True)
    h = x * lax.rsqrt(ms + NORM_EPS) * (1.0 + sc_ref[...]) + sh_ref[...]
    h_ref[...] = h.astype(h_ref.dtype)
    logits = jnp.dot(h, wr_ref[...], preferred_element_type=F32, precision=HI) + br_ref[...]
    lane = lax.broadcasted_iota(jnp.int32, logits.shape, 1).astype(F32)
    neg = -jnp.inf
    gmask = lane < MOE_GROUPS
    glog = jnp.where(gmask, logits, neg)
    gmax = jnp.max(glog, axis=-1, keepdims=True)
    g_idx = jnp.min(jnp.where(glog == gmax, lane, float(LANE)), axis=-1, keepdims=True)
    p_g = 1.0 / jnp.sum(jnp.exp(glog - gmax), axis=-1, keepdims=True)
    e_lo = ROUTE_OFF + g_idx * MOE_PER_GROUP
    emask = (lane >= e_lo) & (lane < e_lo + MOE_PER_GROUP)
    v1 = jnp.max(jnp.where(emask, logits, neg), axis=-1, keepdims=True)
    i1 = jnp.min(jnp.where(emask & (logits == v1), lane, float(LANE)), axis=-1, keepdims=True)
    emask2 = emask & (lane != i1)
    v2 = jnp.max(jnp.where(emask2, logits, neg), axis=-1, keepdims=True)
    i2 = jnp.min(jnp.where(emask2 & (logits == v2), lane, float(LANE)), axis=-1, keepdims=True)
    e21 = jnp.exp(v2 - v1)
    w1 = p_g / (1.0 + e21)
    w2 = p_g * e21 / (1.0 + e21)
    comb_ref[...] = jnp.where(lane == i1, w1, jnp.where(lane == i2, w2, 0.0))


def moe_router(x, shift, scale, group_w, group_b, expert_w, expert_b, tiles_per_batch):
    t, d = x.shape
    wr = _pad_cols(jnp.concatenate([group_w, expert_w], axis=1), LANE)
    br = _pad_cols(jnp.concatenate([group_b, expert_b])[None, :], LANE)
    seg = lambda i: (_seg_index(i, tiles_per_batch), 0, 0)
    return pl.pallas_call(
        _router_kernel,
        grid=(t // ROW_TILE,),
        in_specs=[pl.BlockSpec((ROW_TILE, d), lambda i: (i, 0)),
                  pl.BlockSpec((None, 1, d), seg), pl.BlockSpec((None, 1, d), seg),
                  pl.BlockSpec((d, LANE), lambda i: (0, 0)), pl.BlockSpec((1, LANE), lambda i: (0, 0))],
        out_specs=[pl.BlockSpec((ROW_TILE, d), lambda i: (i, 0)), pl.BlockSpec((ROW_TILE, LANE), lambda i: (i, 0))],
        out_shape=[jax.ShapeDtypeStruct((t, d), BF16), jax.ShapeDtypeStruct((t, LANE), F32)],
        compiler_params=_cparams(("parallel",)),
    )(x, shift, scale, wr, br)


def _moe_dense_kernel(h_ref, comb_ref, wg_ref, wu_ref, wd_ref, x_ref, g0_ref, g1_ref, o_ref, acc_ref):
    e, f = pl.program_id(1), pl.program_id(2)

    @pl.when((e == 0) & (f == 0))
    def _():
        acc_ref[...] = jnp.zeros_like(acc_ref)

    h = h_ref[...]
    a = jnp.dot(h, wg_ref[...], preferred_element_type=F32)
    u = jnp.dot(h, wu_ref[...], preferred_element_type=F32)
    act = (a * jax.nn.sigmoid(a) * u).astype(BF16)
    acc_ref[...] += comb_ref[...] * jnp.dot(act, wd_ref[...], preferred_element_type=F32)

    @pl.when((e == pl.num_programs(1) - 1) & (f == pl.num_programs(2) - 1))
    def _():
        o_ref[:ROW_TILE] = x_ref[:ROW_TILE] + g0_ref[...] * acc_ref[:ROW_TILE]
        o_ref[ROW_TILE:] = x_ref[ROW_TILE:] + g1_ref[...] * acc_ref[ROW_TILE:]


def moe_dense(h, comb_t, w_gate, w_up, w_down, x, gate, tiles_per_batch, tf=512):
    t, d = h.shape
    ne, _, ff = w_gate.shape
    tm = 2 * ROW_TILE
    gspec = lambda k: pl.BlockSpec((None, 1, d), lambda i, e, f: (_seg_index(2 * i + k, tiles_per_batch), 0, 0))
    return pl.pallas_call(
        _moe_dense_kernel,
        grid=(t // tm, ne, ff // tf),
        in_specs=[pl.BlockSpec((tm, d), lambda i, e, f: (i, 0)),
                  pl.BlockSpec((None, tm, 1), lambda i, e, f: (e, i, 0)),
                  pl.BlockSpec((None, d, tf), lambda i, e, f: (e, 0, f)),
                  pl.BlockSpec((None, d, tf), lambda i, e, f: (e, 0, f)),
                  pl.BlockSpec((None, tf, d), lambda i, e, f: (e, f, 0)),
                  pl.BlockSpec((tm, d), lambda i, e, f: (i, 0)),
                  gspec(0), gspec(1)],
        out_specs=pl.BlockSpec((tm, d), lambda i, e, f: (i, 0)),
        out_shape=jax.ShapeDtypeStruct((t, d), F32),
        scratch_shapes=[pltpu.VMEM((tm, d), F32)],
        compiler_params=_cparams(("parallel", "arbitrary", "arbitrary")),
    )(h, comb_t, w_gate, w_up, w_down, x, gate, gate)


def _final_norm_kernel(x_ref, w_ref, o_ref):
    x = x_ref[...]
    ms = jnp.mean(x * x, axis=-1, keepdims=True)
    o_ref[...] = x * lax.rsqrt(ms + NORM_EPS) * w_ref[...]


def final_norm(x, w, n_ctx):
    bsz, t, d = x.shape
    off = n_ctx // ROW_TILE
    return pl.pallas_call(
        _final_norm_kernel,
        grid=(bsz, (t - n_ctx) // ROW_TILE),
        in_specs=[pl.BlockSpec((None, ROW_TILE, d), lambda b, i: (b, i + off, 0)),
                  pl.BlockSpec((1, d), lambda b, i: (0, 0))],
        out_specs=pl.BlockSpec((None, ROW_TILE, d), lambda b, i: (b, i, 0)),
        out_shape=jax.ShapeDtypeStruct((bsz, t - n_ctx, d), F32),
        compiler_params=_cparams(("parallel", "parallel")),
    )(x, w.reshape(1, d))


def _lat_colmajor(t, n_ctx, inverse=False):
    bsz, _, ch = t.shape
    lat = t[:, n_ctx:]
    rows = lat.shape[1] // GRID_W
    shp = (bsz, GRID_W, rows, ch) if inverse else (bsz, rows, GRID_W, ch)
    lat = lat.reshape(shp).transpose(0, 2, 1, 3).reshape(bsz, -1, ch)
    return jnp.concatenate([t[:, :n_ctx], lat], axis=1)


def _seg_table(ctx_vec, lat_mat):
    bsz = lat_mat.shape[0]
    tab = jnp.stack([jnp.broadcast_to(ctx_vec[None, :], lat_mat.shape), lat_mat], axis=1)
    return tab.reshape(2 * bsz, 1, -1)


def kernel(x, c, ctx, c_ctx, ada_w, ada_b, w_in, s5_lambda_re, s5_lambda_im, s5_log_step, s5_b_re, s5_b_im, s5_c_re, s5_c_im, s5_d, s5_glu_w, ssd_conv_w, ssd_conv_b, ssd_a_log, ssd_dt_bias, ssd_d, ssd_norm_w, gla_gate_w, gla_gate_b, gla_norm_w, hy_conv_w, hy_conv_b, hy_w1, hy_b1, hy_freq1, hy_w2, hy_b2, hy_freq2, hy_w3, hy_bias, w_branch, w_out, moe_group_w, moe_group_b, moe_expert_w, moe_expert_b, moe_w_gate, moe_w_up, moe_w_down, final_norm_w):
    bsz, n_lat, d = x.shape
    n_ctx = ctx.shape[1]
    depth = ada_w.shape[0]
    t_b = n_ctx + n_lat
    t_all = bsz * t_b
    tiles_pb = t_b // ROW_TILE
    assert n_ctx == ROW_TILE and n_lat % ROW_TILE == 0 and n_lat == FFT_N // 2 and bsz + 1 <= 8

    xa = jnp.concatenate([ctx, x], axis=1).reshape(t_all, d)
    cc = jnp.pad(jnp.concatenate([c, c_ctx[None, :]], axis=0), ((0, 8 - bsz - 1), (0, 0)))
    mod = adaln(cc, ada_w, ada_b)

    for li in range(depth):
        last = li == depth - 1
        m6 = mod[li].reshape(8, 6, d)
        tabs = [_seg_table(m6[bsz, k], m6[:bsz, k]) for k in range(6)]
        h = modnorm(xa, tabs[0], tabs[1], tiles_pb)

        wl = w_in[li].astype(BF16)
        seg = lambda k0, k1: wl[:, IN_OFFS[k0]:IN_OFFS[k1]]
        (u_s5,) = proj_split(h, seg(0, 1), (MIX_WIDTH,))
        w_ssd = jnp.concatenate([seg(1, 3), _pad_cols(seg(3, 4), LANE)], axis=1)
        z_ssd, xbc_raw, dt_raw = proj_split(h, w_ssd, (MIX_WIDTH, SSD_CONV_CH, LANE))
        w_gla = jnp.concatenate([seg(4, 8), _pad_cols(seg(8, 9), LANE)], axis=1)
        q_g, k_g, v_g, g_g, r_g = proj_split(h, w_gla, (GLA_KEY_WIDTH, GLA_KEY_WIDTH, MIX_WIDTH, MIX_WIDTH, LANE))
        (p_hy,) = proj_split(h, seg(9, 10), (3 * MIX_WIDTH,))
        gates = matmul(h, seg(10, 11), 512, 1024)

        b3 = lambda a: a.reshape(bsz, t_b, a.shape[-1])

        ys = []
        for dr in range(2):
            wts = s5_weights(s5_lambda_re[li, dr], s5_lambda_im[li, dr], s5_log_step[li, dr], s5_b_re[li, dr],
                             s5_b_im[li, dr], s5_c_re[li, dr], s5_c_im[li, dr], dr == 1)
            ys.append(s5_scan(b3(u_s5), wts, n_ctx, dr == 1).reshape(t_all, MIX_WIDTH))
        ya = s5_out(ys[0], ys[1], u_s5, s5_d[li], s5_glu_w[li])

        z_cm = _lat_colmajor(b3(z_ssd), n_ctx)
        xbc_cm = _lat_colmajor(b3(xbc_raw), n_ctx)
        dt_cm = _lat_colmajor(b3(dt_raw), n_ctx)
        xbc_act = dwconv_seq(xbc_cm, ssd_conv_w[li], ssd_conv_b[li], n_ctx, silu=True)
        bias_row = _pad_cols(ssd_dt_bias[li].reshape(1, -1), LANE)
        a_row = _pad_cols(-jnp.exp(ssd_a_log[li]).reshape(1, -1), LANE)
        yd_ssd = [ssd_scan(xbc_act, dt_cm, bias_row, a_row, n_ctx, dr).reshape(t_all, MIX_WIDTH) for dr in range(2)]
        d_row = jnp.repeat(ssd_d[li], SSD_HEAD_DIM)[None, :]
        yb_cm = ssd_out(yd_ssd[0], yd_ssd[1], xbc_act.reshape(t_all, SSD_CONV_CH), z_cm.reshape(t_all, MIX_WIDTH),
                        d_row, ssd_norm_w[li])
        yb = _lat_colmajor(b3(yb_cm), n_ctx, inverse=True).reshape(t_all, MIX_WIDTH)

        os_ = []
        for dr in range(2):
            wg = jnp.zeros((LANE, GLA_KEY_WIDTH), F32).at[dr * GLA_GATE_RANK:(dr + 1) * GLA_GATE_RANK].set(
                gla_gate_w[li, dr])
            os_.append(gla_scan(b3(q_g), b3(k_g), b3(v_g), b3(r_g), wg, gla_gate_b[li, dr][None, :], n_ctx,
                                dr == 1).reshape(t_all, MIX_WIDTH))
        yc = gla_out(os_[0], os_[1], g_g, gla_norm_w[li])

        u_hy = dwconv_seq(b3(p_hy), hy_conv_w[li], hy_conv_b[li], n_ctx, silu=False)
        hy_p = (hy_w1[li], hy_b1[li], hy_freq1[li], hy_w2[li], hy_b2[li], hy_freq2[li], hy_w3[li])
        seqs = [u_hy[:, n_ctx:]]
        filts = [hyena_filters(n_lat, *hy_p)]
        if not last:
            seqs.append(jnp.pad(u_hy[:, :n_ctx], ((0, 0), (0, n_lat - n_ctx), (0, 0))))
            filts.append(hyena_filters(n_ctx, *hy_p))
        hr, hi = hyena_spectrum(filts)
        y_hy = hyena_conv(jnp.concatenate(seqs, axis=0), hr, hi, hy_bias[li], bsz)
        if not last:
            yd = jnp.concatenate([y_hy[bsz:, :n_ctx], y_hy[:bsz]], axis=1)
        else:
            yd = jnp.concatenate([jnp.zeros((bsz, n_ctx, MIX_WIDTH), F32), y_hy], axis=1)
        yd = yd.reshape(t_all, MIX_WIDTH).astype(BF16)

        merged = merge_branches((ya, yb, yc, yd), gates, w_branch[li].astype(BF16))
        xa = matmul_gated_residual(merged, w_out[li].astype(BF16), xa, tabs[2], tiles_pb)

        h2, comb = moe_router(xa, tabs[3], tabs[4], moe_group_w[li], moe_group_b[li], moe_expert_w[li],
                              moe_expert_b[li], tiles_pb)
        comb_t = comb[:, ROUTE_OFF:ROUTE_OFF + MOE_EXPERTS].T[:, :, None]
        xa = moe_dense(h2, comb_t, moe_w_gate[li].astype(BF16), moe_w_up[li].astype(BF16),
                       moe_w_down[li].astype(BF16), xa, tabs[5], tiles_pb)

    return final_norm(xa.reshape(bsz, t_b, d), final_norm_w, n_ctx)
```

```python
import functools
import math

import numpy as np
import jax
import jax.numpy as jnp
from jax import lax
from jax.experimental import pallas as pl
from jax.experimental.pallas import tpu as pltpu

F32 = jnp.float32
BF16 = jnp.bfloat16
HI = lax.Precision.HIGHEST

D_MODEL = 2048
GRID_W = 64
NORM_EPS = 1e-6
N_BRANCH = 4
MIX_WIDTH = 768
S5_GROUP = 16
S5_GROUPS = MIX_WIDTH // S5_GROUP
S5_STATE = 64
SSD_HEAD_DIM = 64
SSD_HEADS = MIX_WIDTH // SSD_HEAD_DIM
SSD_GROUPS = 2
SSD_STATE = 64
SSD_CHUNK = 128
SSD_CONV_CH = MIX_WIDTH + 2 * SSD_GROUPS * SSD_STATE
GLA_HEADS = 6
GLA_KEY_WIDTH = MIX_WIDTH // 2
GLA_DK = GLA_KEY_WIDTH // GLA_HEADS
GLA_DV = MIX_WIDTH // GLA_HEADS
GLA_GATE_RANK = 16
GLA_TAU = 16.0
GLA_CHUNK = 64
HY_ORDER = 2
HY_FILTER_DIM = 64
HY_BANDS = 16
HY_EMB = 2 * HY_BANDS + 1
HY_MAX_DECAY = math.log(1e-2) / 0.3
HY_MIN_DECAY = math.log(1e-2) / 1.5
MOE_GROUPS = 4
MOE_PER_GROUP = 4
MOE_EXPERTS = MOE_GROUPS * MOE_PER_GROUP
MOE_FF = 1024

IN_SIZES = (MIX_WIDTH, MIX_WIDTH, SSD_CONV_CH, 2 * SSD_HEADS, GLA_KEY_WIDTH, GLA_KEY_WIDTH, MIX_WIDTH, MIX_WIDTH,
            2 * GLA_GATE_RANK, (HY_ORDER + 1) * MIX_WIDTH, N_BRANCH * D_MODEL)
IN_OFFS = tuple(int(v) for v in np.cumsum((0,) + IN_SIZES))

LANE = 128
ROW_TILE = 256
VMEM_LIMIT = 56 * 1024 * 1024

FFT_N1 = 64
FFT_N2 = 128
FFT_N = FFT_N1 * FFT_N2


def _cparams(sem):
    return pltpu.CompilerParams(dimension_semantics=sem, vmem_limit_bytes=VMEM_LIMIT)


def _pad_cols(a, width):
    return jnp.pad(a, [(0, 0)] * (a.ndim - 1) + [(0, width - a.shape[-1])])


def _mm_kernel(x_ref, w_ref, o_ref, *, precision):
    o_ref[...] = jnp.dot(x_ref[...], w_ref[...], preferred_element_type=F32,
                         precision=precision).astype(o_ref.dtype)


def matmul(x, w, tm, tn, out_dtype=F32, precision=None):
    m, k = x.shape
    n = w.shape[1]
    return pl.pallas_call(
        functools.partial(_mm_kernel, precision=precision),
        grid=(n // tn, m // tm),
        in_specs=[pl.BlockSpec((tm, k), lambda j, i: (i, 0)), pl.BlockSpec((k, tn), lambda j, i: (0, j))],
        out_specs=pl.BlockSpec((tm, tn), lambda j, i: (i, j)),
        out_shape=jax.ShapeDtypeStruct((m, n), out_dtype),
        compiler_params=_cparams(("parallel", "parallel")),
    )(x, w)


def _proj_kernel(x_ref, w_ref, *o_refs):
    acc = jnp.dot(x_ref[...], w_ref[...], preferred_element_type=F32)
    off = 0
    for o_ref in o_refs:
        wd = o_ref.shape[-1]
        o_ref[...] = acc[:, off:off + wd].astype(o_ref.dtype)
        off += wd


def proj_split(x, w, widths, tm=512):
    m, k = x.shape
    n = w.shape[1]
    assert sum(widths) == n
    return pl.pallas_call(
        _proj_kernel,
        grid=(m // tm,),
        in_specs=[pl.BlockSpec((tm, k), lambda i: (i, 0)), pl.BlockSpec((k, n), lambda i: (0, 0))],
        out_specs=[pl.BlockSpec((tm, wd), lambda i: (i, 0)) for wd in widths],
        out_shape=[jax.ShapeDtypeStruct((m, wd), F32) for wd in widths],
        compiler_params=_cparams(("parallel",)),
    )(x, w)


def _adaln_kernel(c_ref, w_ref, b_ref, o_ref):
    c = c_ref[...]
    s = c * jax.nn.sigmoid(c)
    o_ref[...] = jnp.dot(s, w_ref[...], preferred_element_type=F32, precision=HI) + b_ref[...]


def adaln(cc, ada_w, ada_b, tn=1024):
    depth, d, n = ada_w.shape
    return pl.pallas_call(
        _adaln_kernel,
        grid=(depth, n // tn),
        in_specs=[pl.BlockSpec((8, d), lambda l, j: (0, 0)),
                  pl.BlockSpec((None, d, tn), lambda l, j: (l, 0, j)),
                  pl.BlockSpec((None, 1, tn), lambda l, j: (l, 0, j))],
        out_specs=pl.BlockSpec((None, 8, tn), lambda l, j: (l, 0, j)),
        out_shape=jax.ShapeDtypeStruct((depth, 8, n), F32),
        compiler_params=_cparams(("parallel", "parallel")),
    )(cc, ada_w, ada_b.reshape(depth, 1, n))


def _seg_index(i, tiles_per_batch):
    b = i // tiles_per_batch
    return 2 * b + jnp.where(i % tiles_per_batch == 0, 0, 1)


def _modnorm_kernel(x_ref, sh_ref, sc_ref, o_ref):
    x = x_ref[...]
    ms = jnp.mean(x * x, axis=-1, keepdims=True)
    o_ref[...] = (x * lax.rsqrt(ms + NORM_EPS) * (1.0 + sc_ref[...]) + sh_ref[...]).astype(o_ref.dtype)


def modnorm(x, shift, scale, tiles_per_batch):
    t, d = x.shape
    seg = lambda i: (_seg_index(i, tiles_per_batch), 0, 0)
    return pl.pallas_call(
        _modnorm_kernel,
        grid=(t // ROW_TILE,),
        in_specs=[pl.BlockSpec((ROW_TILE, d), lambda i: (i, 0)),
                  pl.BlockSpec((None, 1, d), seg), pl.BlockSpec((None, 1, d), seg)],
        out_specs=pl.BlockSpec((ROW_TILE, d), lambda i: (i, 0)),
        out_shape=jax.ShapeDtypeStruct((t, d), BF16),
        compiler_params=_cparams(("parallel",)),
    )(x, shift, scale)


def _dwconv_kernel(x_ref, w_ref, b_ref, o_ref, *, n_ctx, silu):
    x = x_ref[...]
    t = x.shape[0]
    rows = lax.broadcasted_iota(jnp.int32, x.shape, 0)
    prev = jnp.where((rows == 0) | (rows == n_ctx), 0.0, pltpu.roll(x, 1, 0))
    nxt = jnp.where((rows == n_ctx - 1) | (rows == t - 1), 0.0, pltpu.roll(x, t - 1, 0))
    y = w_ref[0:1, :] * prev + w_ref[1:2, :] * x + w_ref[2:3, :] * nxt + b_ref[...]
    if silu:
        y = y * jax.nn.sigmoid(y)
    o_ref[...] = y


def dwconv_seq(x, w, b, n_ctx, silu, tc=LANE):
    bsz, t, ch = x.shape
    return pl.pallas_call(
        functools.partial(_dwconv_kernel, n_ctx=n_ctx, silu=silu),
        grid=(bsz, ch // tc),
        in_specs=[pl.BlockSpec((None, t, tc), lambda b_, j: (b_, 0, j)),
                  pl.BlockSpec((3, tc), lambda b_, j: (0, j)),
                  pl.BlockSpec((1, tc), lambda b_, j: (0, j))],
        out_specs=pl.BlockSpec((None, t, tc), lambda b_, j: (b_, 0, j)),
        out_shape=jax.ShapeDtypeStruct((bsz, t, ch), F32),
        compiler_params=_cparams(("parallel", "parallel")),
    )(x, w, b.reshape(1, ch))


S5_Q = 8
S5_TG = LANE // S5_GROUP
S5_TS = S5_TG * S5_STATE
S5_NT = MIX_WIDTH // LANE


def _cmul(ar, ai, br, bi):
    return ar * br - ai * bi, ar * bi + ai * br


def s5_weights(lam_re, lam_im, log_step, b_re, b_im, c_re, c_im, reverse):
    g, p, q = S5_GROUPS, S5_STATE, S5_Q
    step = jnp.exp(log_step)[:, None]
    mag = jnp.exp(lam_re * step)
    ab_re = mag * jnp.cos(lam_im * step)
    ab_im = mag * jnp.sin(lam_im * step)
    den = lam_re * lam_re + lam_im * lam_im
    zr, zi = _cmul(ab_re - 1.0, ab_im, lam_re, -lam_im)
    zr, zi = zr / den, zi / den
    bb_re, bb_im = _cmul(zr[..., None], zi[..., None], b_re, b_im)
    pw_re, pw_im = [jnp.ones_like(ab_re)], [jnp.zeros_like(ab_im)]
    for _ in range(q):
        r, i = _cmul(pw_re[-1], pw_im[-1], ab_re, ab_im)
        pw_re.append(r)
        pw_im.append(i)
    pw_re, pw_im = jnp.stack(pw_re), jnp.stack(pw_im)
    cp_re = c_re[None] * pw_re[:, :, None, :] - c_im[None] * pw_im[:, :, None, :]
    cp_im = c_re[None] * pw_im[:, :, None, :] + c_im[None] * pw_re[:, :, None, :]
    kd = (jnp.einsum('dghp,gpk->dghk', cp_re[:q], bb_re, precision=HI)
          - jnp.einsum('dghp,gpk->dghk', cp_im[:q], bb_im, precision=HI))
    s_idx = np.arange(q)[:, None]
    r_idx = np.arange(q)[None, :]
    lag = (s_idx - r_idx) if reverse else (r_idx - s_idx)
    valid = jnp.asarray(lag >= 0, F32)
    kt = kd[np.clip(lag, 0, q - 1)] * valid[:, :, None, None, None]
    kt = kt.reshape(q, q, S5_NT, S5_TG, S5_GROUP, S5_GROUP)
    eye = jnp.eye(S5_TG, dtype=F32)
    ktoep = jnp.einsum('srjghk,gG->jsgkrGh', kt, eye).reshape(S5_NT, q * LANE, q * LANE)
    e_of_s = (np.arange(q) if reverse else (q - 1 - np.arange(q)))
    pb_re = pw_re[e_of_s][..., None] * bb_re[None] - pw_im[e_of_s][..., None] * bb_im[None]
    pb_im = pw_re[e_of_s][..., None] * bb_im[None] + pw_im[e_of_s][..., None] * bb_re[None]
    pbc = jnp.stack([pb_re, pb_im]).reshape(2, q, S5_NT, S5_TG, p, S5_GROUP)
    bw = jnp.einsum('csjgpk,gG->jsgkcGp', pbc, eye).reshape(S5_NT, q * LANE, 2 * S5_TS)
    f_of_r = (q - np.arange(q)) if reverse else (np.arange(q) + 1)
    cpf = jnp.stack([cp_re[f_of_r], -cp_im[f_of_r]]).reshape(2, q, S5_NT, S5_TG, S5_GROUP, p)
    cw = jnp.einsum('crjghp,gG->jcgprGh', cpf, eye).reshape(S5_NT, 2 * S5_TS, q * LANE)
    ar, ai = pw_re[q], pw_im[q]
    lv = []
    for _ in range(16):
        lv.append(jnp.concatenate([ar.reshape(S5_NT, S5_TS), ai.reshape(S5_NT, S5_TS)], axis=-1))
        ar, ai = _cmul(ar, ai, ar, ai)
    a2 = jnp.stack(lv, axis=1)
    return ktoep.astype(BF16), bw.astype(BF16), cw.astype(BF16), a2


def _s5_seg_scan(xr, xi, a2_ref, reverse):
    n = xr.shape[0]
    rows = lax.broadcasted_iota(jnp.int32, xr.shape, 0)
    k, shift = 0, 1
    while shift < n:
        ar = a2_ref[k:k + 1, :S5_TS]
        ai = a2_ref[k:k + 1, S5_TS:]
        if reverse:
            keep = rows < n - shift
            sr, si = pltpu.roll(xr, n - shift, 0), pltpu.roll(xi, n - shift, 0)
        else:
            keep = rows >= shift
            sr, si = pltpu.roll(xr, shift, 0), pltpu.roll(xi, shift, 0)
        sr = jnp.where(keep, sr, 0.0)
        si = jnp.where(keep, si, 0.0)
        xr, xi = xr + ar * sr - ai * si, xi + ar * si + ai * sr
        k += 1
        shift *= 2
    return xr, xi


def _s5_entering(sr, si, carry, a2_ref, reverse):
    n = sr.shape[0]
    rows = lax.broadcasted_iota(jnp.int32, sr.shape, 0)
    first = (n - 1) if reverse else 0
    if carry is not None:
        cr, ci = carry
        ar, ai = a2_ref[0:1, :S5_TS], a2_ref[0:1, S5_TS:]
        ir, ii = _cmul(ar, ai, cr, ci)
        sr = jnp.where(rows == first, sr + ir, sr)
        si = jnp.where(rows == first, si + ii, si)
    hr, hi = _s5_seg_scan(sr, si, a2_ref, reverse)
    if reverse:
        out = (hr[0:1], hi[0:1])
        er, ei = pltpu.roll(hr, n - 1, 0), pltpu.roll(hi, n - 1, 0)
    else:
        out = (hr[n - 1:n], hi[n - 1:n])
        er, ei = pltpu.roll(hr, 1, 0), pltpu.roll(hi, 1, 0)
    if carry is None:
        er = jnp.where(rows == first, 0.0, er)
        ei = jnp.where(rows == first, 0.0, ei)
    else:
        er = jnp.where(rows == first, cr, er)
        ei = jnp.where(rows == first, ci, ei)
    return er, ei, out


def _s5_kernel(u_ref, kt_ref, bw_ref, cw_ref, a2_ref, y_ref, *, n_ctx_rows, reverse):
    n = u_ref.shape[0] // S5_Q
    x = jnp.concatenate([u_ref[pl.ds(s, n, stride=S5_Q), :] for s in range(S5_Q)], axis=1).astype(BF16)
    y = jnp.dot(x, kt_ref[...], preferred_element_type=F32)
    st = jnp.dot(x, bw_ref[...], preferred_element_type=F32)
    sr, si = st[:, :S5_TS], st[:, S5_TS:]
    cr, ci, carry = _s5_entering(sr[:n_ctx_rows], si[:n_ctx_rows], None, a2_ref, reverse)
    lr, li, _ = _s5_entering(sr[n_ctx_rows:], si[n_ctx_rows:], carry, a2_ref, reverse)
    h = jnp.concatenate([jnp.concatenate([cr, lr], axis=0), jnp.concatenate([ci, li], axis=0)], axis=1)
    y = y + jnp.dot(h.astype(BF16), cw_ref[...], preferred_element_type=F32)
    for r in range(S5_Q):
        y_ref[pl.ds(r, n, stride=S5_Q), :] = y[:, r * LANE:(r + 1) * LANE]


def s5_scan(u, weights, n_ctx, reverse):
    bsz, t, ch = u.shape
    ktoep, bw, cw, a2 = weights
    qk = S5_Q * LANE
    return pl.pallas_call(
        functools.partial(_s5_kernel, n_ctx_rows=n_ctx // S5_Q, reverse=reverse),
        grid=(S5_NT, bsz),
        in_specs=[pl.BlockSpec((None, t, LANE), lambda j, b: (b, 0, j)),
                  pl.BlockSpec((None, qk, qk), lambda j, b: (j, 0, 0)),
                  pl.BlockSpec((None, qk, 2 * S5_TS), lambda j, b: (j, 0, 0)),
                  pl.BlockSpec((None, 2 * S5_TS, qk), lambda j, b: (j, 0, 0)),
                  pl.BlockSpec((None, 16, 2 * S5_TS), lambda j, b: (j, 0, 0))],
        out_specs=pl.BlockSpec((None, t, LANE), lambda j, b: (b, 0, j)),
        out_shape=jax.ShapeDtypeStruct((bsz, t, ch), F32),
        compiler_params=_cparams(("parallel", "parallel")),
    )(u, ktoep, bw, cw, a2)


def _s5_out_kernel(yf_ref, yb_ref, u_ref, d_ref, w_ref, o_ref):
    y = yf_ref[...] + yb_ref[...] + d_ref[...] * u_ref[...]
    g = jax.nn.gelu(y)
    z = jnp.dot(g.astype(BF16), w_ref[...], preferred_element_type=F32)
    o_ref[...] = (g * jax.nn.sigmoid(z)).astype(o_ref.dtype)


def s5_out(yf, yb, u, d_skip, glu_w, tm=512):
    t, ch = u.shape
    row = pl.BlockSpec((tm, ch), lambda i: (i, 0))
    return pl.pallas_call(
        _s5_out_kernel,
        grid=(t // tm,),
        in_specs=[row, row, row, pl.BlockSpec((1, ch), lambda i: (0, 0)), pl.BlockSpec((ch, ch), lambda i: (0, 0))],
        out_specs=row,
        out_shape=jax.ShapeDtypeStruct((t, ch), BF16),
        compiler_params=_cparams(("parallel",)),
    )(yf, yb, u, d_skip.reshape(1, ch), glu_w.astype(BF16))


def _softplus(x):
    return jnp.maximum(x, 0.0) + jnp.log1p(jnp.exp(-jnp.abs(x)))


def _ssd_kernel(xbc_ref, dt_ref, bias_ref, a_ref, tri_ref, y_ref, st_ref, *, d, reverse):
    q = SSD_CHUNK
    hp = SSD_HEAD_DIM
    hg = SSD_HEADS // SSD_GROUPS
    gw = hg * hp

    @pl.when(pl.program_id(1) == 0)
    def _():
        st_ref[...] = jnp.zeros_like(st_ref)

    dt = _softplus(dt_ref[...] + bias_ref[...])
    adt = dt * a_ref[...]
    cs = jnp.dot(tri_ref[...], adt, preferred_element_type=F32, precision=HI)
    tot = jnp.sum(adt, axis=0, keepdims=True)
    cs_t = cs.T
    ecs = jnp.exp(cs)
    dec = jnp.exp(tot - cs)
    etot = jnp.exp(tot)
    xs = xbc_ref[:, :MIX_WIDTH]
    bm = xbc_ref[:, MIX_WIDTH:MIX_WIDTH + SSD_GROUPS * SSD_STATE]
    cm = xbc_ref[:, MIX_WIDTH + SSD_GROUPS * SSD_STATE:]
    bm_t = bm.T
    li = lax.broadcasted_iota(jnp.int32, (q, q), 0)
    si = lax.broadcasted_iota(jnp.int32, (q, q), 1)
    mask = (si >= li) if reverse else (si <= li)
    for g in range(SSD_GROUPS):
        cg = cm[:, g * SSD_STATE:(g + 1) * SSD_STATE].astype(BF16)
        bg = bm[:, g * SSD_STATE:(g + 1) * SSD_STATE].astype(BF16)
        gmat = lax.dot_general(cg, bg, (((1,), (1,)), ((), ())), preferred_element_type=F32)
        yoff = jnp.dot(cg, st_ref[g].astype(BF16), preferred_element_type=F32)
        xw = []
        for hl in range(hg):
            h = g * hg + hl
            col = d * SSD_HEADS + h
            seg = cs[:, col:col + 1] - cs_t[col:col + 1, :]
            lmat = jnp.exp(jnp.where(mask, seg, -jnp.inf))
            xdt = xs[:, h * hp:(h + 1) * hp] * dt[:, col:col + 1]
            yd = jnp.dot((gmat * lmat).astype(BF16), xdt.astype(BF16), preferred_element_type=F32)
            y_ref[:, h * hp:(h + 1) * hp] = yd + yoff[:, hl * hp:(hl + 1) * hp] * ecs[:, col:col + 1]
            xw.append(xdt * dec[:, col:col + 1])
        xw = jnp.concatenate(xw, axis=1).astype(BF16)
        new = jnp.dot(bm_t[g * SSD_STATE:(g + 1) * SSD_STATE, :].astype(BF16), xw, preferred_element_type=F32)
        for hl in range(hg):
            col = d * SSD_HEADS + g * hg + hl
            sl = slice(hl * hp, (hl + 1) * hp)
            st_ref[g, :, sl] = st_ref[g, :, sl] * etot[:, col:col + 1] + new[:, sl]


def ssd_scan(xbc, dt_raw, dt_bias_row, a_row, n_ctx, d):
    bsz, t, _ = xbc.shape
    reverse = d == 1
    q = SSD_CHUNK
    nchunk, nctx = t // q, n_ctx // q
    tri = np.tril(np.ones((q, q), np.float32))
    tri = jnp.asarray(tri.T if reverse else tri)
    if reverse:
        cidx = lambda i: jnp.where(i < nctx, nctx - 1 - i, nchunk + nctx - 1 - i)
    else:
        cidx = lambda i: i
    return pl.pallas_call(
        functools.partial(_ssd_kernel, d=d, reverse=reverse),
        grid=(bsz, nchunk),
        in_specs=[pl.BlockSpec((None, q, SSD_CONV_CH), lambda b, i: (b, cidx(i), 0)),
                  pl.BlockSpec((None, q, LANE), lambda b, i: (b, cidx(i), 0)),
                  pl.BlockSpec((1, LANE), lambda b, i: (0, 0)),
                  pl.BlockSpec((1, LANE), lambda b, i: (0, 0)),
                  pl.BlockSpec((q, q), lambda b, i: (0, 0))],
        out_specs=pl.BlockSpec((None, q, MIX_WIDTH), lambda b, i: (b, cidx(i), 0)),
        out_shape=jax.ShapeDtypeStruct((bsz, t, MIX_WIDTH), F32),
        scratch_shapes=[pltpu.VMEM((SSD_GROUPS, SSD_STATE, (SSD_HEADS // SSD_GROUPS) * SSD_HEAD_DIM), F32)],
        compiler_params=_cparams(("parallel", "arbitrary")),
    )(xbc, dt_raw, dt_bias_row, a_row, tri)


def _ssd_out_kernel(yf_ref, yb_ref, xbc_ref, z_ref, d_ref, nw_ref, o_ref):
    z = z_ref[...]
    y = (d_ref[...] * xbc_ref[...] + yf_ref[...] + yb_ref[...]) * (z * jax.nn.sigmoid(z))
    ms = jnp.mean(y * y, axis=-1, keepdims=True)
    o_ref[...] = (y * lax.rsqrt(ms + NORM_EPS) * nw_ref[...]).astype(o_ref.dtype)


def ssd_out(yf, yb, xbc, z, d_row, norm_w, tm=512):
    t, ch = z.shape
    row = pl.BlockSpec((tm, ch), lambda i: (i, 0))
    vec = pl.BlockSpec((1, ch), lambda i: (0, 0))
    return pl.pallas_call(
        _ssd_out_kernel,
        grid=(t // tm,),
        in_specs=[row, row, pl.BlockSpec((tm, ch), lambda i: (i, 0)), row, vec, vec],
        out_specs=row,
        out_shape=jax.ShapeDtypeStruct((t, ch), BF16),
        compiler_params=_cparams(("parallel",)),
    )(yf, yb, xbc, z, d_row, norm_w.reshape(1, ch))


def _gla_kernel(q_ref, k_ref, v_ref, r_ref, wg_ref, gb_ref, tri_ref, o_ref, st_ref, *, reverse):
    qs = GLA_CHUNK

    @pl.when(pl.program_id(1) == 0)
    def _():
        st_ref[...] = jnp.zeros_like(st_ref)

    zl = jnp.dot(r_ref[...], wg_ref[...], preferred_element_type=F32, precision=HI) + gb_ref[...]
    la = (jnp.minimum(zl, 0.0) - jnp.log1p(jnp.exp(-jnp.abs(zl)))) / GLA_TAU
    bc = jnp.dot(tri_ref[...], la, preferred_element_type=F32, precision=HI)
    tot = jnp.sum(la, axis=0, keepdims=True)
    qd = (q_ref[...] * (GLA_DK ** -0.5) * jnp.exp(bc)).astype(BF16)
    ki = (k_ref[...] * jnp.exp(-bc)).astype(BF16)
    ke = (k_ref[...] * jnp.exp(tot - bc)).astype(BF16)
    gam = jnp.exp(tot)
    ti = lax.broadcasted_iota(jnp.int32, (qs, qs), 0)
    si = lax.broadcasted_iota(jnp.int32, (qs, qs), 1)
    mask = (si >= ti) if reverse else (si <= ti)
    for h in range(GLA_HEADS):
        ks = slice(h * GLA_DK, (h + 1) * GLA_DK)
        vs = slice(h * GLA_DV, (h + 1) * GLA_DV)
        vh = v_ref[:, vs].astype(BF16)
        sc = lax.dot_general(qd[:, ks], ki[:, ks], (((1,), (1,)), ((), ())), preferred_element_type=F32)
        sc = jnp.where(mask, sc, 0.0).astype(BF16)
        o_intra = jnp.dot(sc, vh, preferred_element_type=F32)
        st = st_ref[:, ks]
        o_inter = lax.dot_general(qd[:, ks], st.astype(BF16), (((1,), (1,)), ((), ())), preferred_element_type=F32)
        o_ref[:, vs] = o_intra + o_inter
        ds = lax.dot_general(vh, ke[:, ks], (((0,), (0,)), ((), ())), preferred_element_type=F32)
        st_ref[:, ks] = st * gam[:, ks] + ds


def gla_scan(q, k, v, r, wg, gb, n_ctx, reverse):
    bsz, t, _ = q.shape
    qs = GLA_CHUNK
    nchunk, nctx = t // qs, n_ctx // qs
    tri = np.tril(np.ones((qs, qs), np.float32))
    tri = jnp.asarray(tri.T if reverse else tri)
    if reverse:
        cidx = lambda i: jnp.where(i < nctx, nctx - 1 - i, nchunk + nctx - 1 - i)
    else:
        cidx = lambda i: i
    blk = lambda w: pl.BlockSpec((None, qs, w), lambda b, i: (b, cidx(i), 0))
    return pl.pallas_call(
        functools.partial(_gla_kernel, reverse=reverse),
        grid=(bsz, nchunk),
        in_specs=[blk(GLA_KEY_WIDTH), blk(GLA_KEY_WIDTH), blk(MIX_WIDTH), blk(LANE),
                  pl.BlockSpec((LANE, GLA_KEY_WIDTH), lambda b, i: (0, 0)),
                  pl.BlockSpec((1, GLA_KEY_WIDTH), lambda b, i: (0, 0)),
                  pl.BlockSpec((qs, qs), lambda b, i: (0, 0))],
        out_specs=blk(MIX_WIDTH),
        out_shape=jax.ShapeDtypeStruct((bsz, t, MIX_WIDTH), F32),
        scratch_shapes=[pltpu.VMEM((GLA_DV, GLA_KEY_WIDTH), F32)],
        compiler_params=_cparams(("parallel", "arbitrary")),
    )(q, k, v, r, wg, gb, tri)


def _gla_out_kernel(of_ref, ob_ref, g_ref, nw_ref, o_ref):
    g = g_ref[...]
    sg = g * jax.nn.sigmoid(g)
    for h in range(GLA_HEADS):
        vs = slice(h * GLA_DV, (h + 1) * GLA_DV)
        o = of_ref[:, vs] + ob_ref[:, vs]
        ms = jnp.mean(o * o, axis=-1, keepdims=True)
        o_ref[:, vs] = (o * lax.rsqrt(ms + NORM_EPS) * nw_ref[...] * sg[:, vs]).astype(o_ref.dtype)


def gla_out(of, ob, g, norm_w, tm=512):
    t, ch = g.shape
    row = pl.BlockSpec((tm, ch), lambda i: (i, 0))
    return pl.pallas_call(
        _gla_out_kernel,
        grid=(t // tm,),
        in_specs=[row, row, row, pl.BlockSpec((1, GLA_DV), lambda i: (0, 0))],
        out_specs=row,
        out_shape=jax.ShapeDtypeStruct((t, ch), BF16),
        compiler_params=_cparams(("parallel",)),
    )(of, ob, g, norm_w.reshape(1, GLA_DV))


def _hy_emb(n):
    t = np.linspace(0.0, 1.0, n)[:, None]
    freqs = np.linspace(1e-4, HY_BANDS - 1, HY_BANDS)
    ang = (2.0 * math.pi / n) * np.arange(n)[:, None] * freqs[None, :]
    emb = np.concatenate([t, np.cos(ang), -np.sin(ang)], axis=-1)
    return np.pad(emb, ((0, 0), (0, LANE - HY_EMB))).astype(np.float32)


def _hy_filter_kernel(emb_ref, w1_ref, b1_ref, f1_ref, w2_ref, b2_ref, f2_ref, w3_ref, dl_ref, o_ref):
    emb = emb_ref[...]
    h = jnp.sin(f1_ref[...] * (jnp.dot(emb, w1_ref[...], preferred_element_type=F32, precision=HI) + b1_ref[...]))
    h = jnp.sin(f2_ref[...] * (jnp.dot(h, w2_ref[...], preferred_element_type=F32, precision=HI) + b2_ref[...]))
    h = jnp.dot(h, w3_ref[...], preferred_element_type=F32, precision=HI)
    o_ref[...] = h * jnp.exp(-emb[:, 0:1] * dl_ref[...])


def hyena_filters(n, w1, b1, f1, w2, b2, f2, w3, tm=256):
    nf = w3.shape[1]
    emb = jnp.asarray(_hy_emb(n))
    deltas = np.abs(np.linspace(HY_MIN_DECAY, HY_MAX_DECAY, MIX_WIDTH)).astype(np.float32)
    dl = jnp.asarray(np.tile(deltas, nf // MIX_WIDTH)[None, :])
    w1p = jnp.pad(w1, ((0, LANE - HY_EMB), (0, 0)))
    fd = HY_FILTER_DIM
    vec = lambda w: pl.BlockSpec((1, w), lambda i: (0, 0))
    return pl.pallas_call(
        _hy_filter_kernel,
        grid=(n // tm,),
        in_specs=[pl.BlockSpec((tm, LANE), lambda i: (i, 0)),
                  pl.BlockSpec((LANE, fd), lambda i: (0, 0)), vec(fd), vec(fd),
                  pl.BlockSpec((fd, fd), lambda i: (0, 0)), vec(fd), vec(fd),
                  pl.BlockSpec((fd, nf), lambda i: (0, 0)), vec(nf)],
        out_specs=pl.BlockSpec((tm, nf), lambda i: (i, 0)),
        out_shape=jax.ShapeDtypeStruct((n, nf), F32),
        compiler_params=_cparams(("parallel",)),
    )(emb, w1p, b1.reshape(1, fd), f1.reshape(1, fd), w2, b2.reshape(1, fd), f2.reshape(1, fd), w3, dl)


class _HyDft:
    def __init__(self, n):
        big = 2 * n
        self.n1 = big // FFT_N2
        self.rows = n // FFT_N2
        self.kh = self.n1 // 2 + 1
        self.khp = -(-self.kh // 8) * 8
        k1 = np.arange(self.kh)
        ang = 2.0 * np.pi * np.outer(k1, np.arange(self.rows)) / self.n1
        fwd = np.zeros((2 * self.khp, self.rows))
        fwd[:self.kh] = np.cos(ang)
        fwd[self.khp:self.khp + self.kh] = -np.sin(ang)
        wgt = np.full(self.kh, 2.0)
        wgt[0] = wgt[-1] = 1.0
        inv = np.zeros((self.rows, 2 * self.khp))
        inv[:, :self.kh] = (wgt[:, None] * np.cos(ang)).T / big
        inv[:, self.khp:self.khp + self.kh] = -(wgt[:, None] * np.sin(ang)).T / big
        tw = 2.0 * np.pi * np.outer(k1, np.arange(FFT_N2)) / big
        f32 = lambda a: np.ascontiguousarray(a, dtype=np.float32)
        self.fwd, self.inv = f32(fwd), f32(inv)
        self.tw_re, self.tw_im = f32(np.cos(tw)[:, :, None]), f32(-np.sin(tw)[:, :, None])
        self.small = self.rows < 8


def _inner_dft_mats():
    k2 = np.arange(FFT_N2)
    f2 = np.exp(-2j * np.pi * np.outer(k2, k2) / FFT_N2)
    w2 = np.block([[f2.real, -f2.imag], [f2.imag, f2.real]])
    w2c = np.block([[f2.real, f2.imag], [-f2.imag, f2.real]])
    return w2.astype(np.float32), w2c.astype(np.float32)


_W2, _W2C = _inner_dft_mats()
HY_UNROLL = 4


def _sc_load(ref, idx):
    return jnp.concatenate([ref[w, idx, :] for w in range(ref.shape[0])], axis=1)


def _sc_store(ref, idx, val):
    for w in range(ref.shape[0]):
        ref[w, idx, :] = val[:, w * LANE:(w + 1) * LANE]


def _hy_outer_fwd(load_rows, cst, fwd_ref, xr_ref, xi_ref):
    if cst.small:
        blocks = [load_rows(n1 * FFT_N2, FFT_N2, None) for n1 in range(cst.rows)]
        for k1 in range(cst.kh):
            xr = sum(float(cst.fwd[k1, n1]) * blocks[n1] for n1 in range(cst.rows))
            xi = sum(float(cst.fwd[cst.khp + k1, n1]) * blocks[n1] for n1 in range(cst.rows))
            _sc_store(xr_ref, pl.ds(k1 * FFT_N2, FFT_N2), xr)
            _sc_store(xi_ref, pl.ds(k1 * FFT_N2, FFT_N2), xi)
        return

    def body(n2, c):
        res = jnp.dot(fwd_ref[...], load_rows(n2, cst.rows, FFT_N2), preferred_element_type=F32, precision=HI)
        _sc_store(xr_ref, pl.ds(n2, cst.khp, stride=FFT_N2), res[:cst.khp])
        _sc_store(xi_ref, pl.ds(n2, cst.khp, stride=FFT_N2), res[cst.khp:])
        return c
    lax.fori_loop(0, FFT_N2, body, 0, unroll=HY_UNROLL)


def _hy_inner_fwd(k1, xr_ref, xi_ref, twr_ref, twi_ref, w2_ref):
    start = k1 * FFT_N2
    rows = pl.ds(start if isinstance(k1, int) else pl.multiple_of(start, FFT_N2), FFT_N2)
    xr, xi = _sc_load(xr_ref, rows), _sc_load(xi_ref, rows)
    twr, twi = twr_ref[k1], twi_ref[k1]
    ar = xr * twr - xi * twi
    ai = xr * twi + xi * twr
    ys = jnp.dot(w2_ref[...], jnp.concatenate([ar, ai], axis=0).astype(BF16), preferred_element_type=F32)
    return rows, twr, twi, ys[:FFT_N2], ys[FFT_N2:]


def _hy_for_k1(cst, body):
    if cst.small:
        for k1 in range(cst.kh):
            body(k1, 0)
    else:
        lax.fori_loop(0, cst.kh, body, 0)


def _hy_spectrum_part(cst, hf_ref, hb_ref, fwd_ref, twr_ref, twi_ref, w2_ref, hr_ref, hi_ref, xr_ref, xi_ref):
    width = hf_ref.shape[1]

    def load_rows(start, count, stride):
        idx = pl.ds(start, count) if stride is None else pl.ds(start, count, stride=stride)
        return jnp.concatenate([hf_ref[idx, :], hb_ref[idx, :]], axis=1)
    _hy_outer_fwd(load_rows, cst, fwd_ref, xr_ref, xi_ref)
    hb0 = hb_ref[0:1, :]

    def body(k1, c):
        rows, _, _, yr, yi = _hy_inner_fwd(k1, xr_ref, xi_ref, twr_ref, twi_ref, w2_ref)
        hr_ref[rows, :] = yr[:, :width] + yr[:, width:] - hb0
        hi_ref[rows, :] = yi[:, :width] - yi[:, width:]
        return c
    _hy_for_k1(cst, body)


def _hy_spectrum_kernel(*refs, parts):
    w2_ref = refs[0]
    pos = 1
    n_in = 5 * len(parts)
    outs = refs[1 + n_in:1 + n_in + 2 * len(parts)]
    xr_ref, xi_ref = refs[-2:]
    for p, cst in enumerate(parts):
        hf_ref, hb_ref, fwd_ref, twr_ref, twi_ref = refs[pos:pos + 5]
        pos += 5
        _hy_spectrum_part(cst, hf_ref, hb_ref, fwd_ref, twr_ref, twi_ref, w2_ref, outs[2 * p], outs[2 * p + 1],
                          xr_ref, xi_ref)


def hyena_spectrum(filts):
    parts = tuple(_HyDft(f.shape[0]) for f in filts)
    nt = MIX_WIDTH // LANE
    width = HY_ORDER * MIX_WIDTH
    args = [jnp.asarray(_W2).astype(BF16)]
    in_specs = [pl.BlockSpec((2 * FFT_N2, 2 * FFT_N2), lambda t: (0, 0))]
    out_specs, out_shape = [], []
    for f, cst in zip(filts, parts):
        n = f.shape[0]
        args += [f, f, jnp.asarray(cst.fwd), jnp.asarray(cst.tw_re), jnp.asarray(cst.tw_im)]
        in_specs += [pl.BlockSpec((n, LANE), lambda t: (0, (t // nt) * 2 * nt + t % nt)),
                     pl.BlockSpec((n, LANE), lambda t: (0, (t // nt) * 2 * nt + nt + t % nt)),
                     pl.BlockSpec(cst.fwd.shape, lambda t: (0, 0)),
                     pl.BlockSpec(cst.tw_re.shape, lambda t: (0, 0, 0)),
                     pl.BlockSpec(cst.tw_im.shape, lambda t: (0, 0, 0))]
        out_specs += [pl.BlockSpec((cst.kh * FFT_N2, LANE), lambda t: (0, t))] * 2
        out_shape += [jax.ShapeDtypeStruct((cst.kh * FFT_N2, width), F32)] * 2
    rows = max(c.khp for c in parts) * FFT_N2
    outs = pl.pallas_call(
        functools.partial(_hy_spectrum_kernel, parts=parts),
        grid=(width // LANE,),
        in_specs=in_specs, out_specs=out_specs, out_shape=out_shape,
        scratch_shapes=[pltpu.VMEM((2, rows, LANE), F32)] * 2,
        compiler_params=_cparams(("parallel",)),
        name="hyena_spectrum",
    )(*args)
    return [(outs[2 * p], outs[2 * p + 1]) for p in range(len(parts))]


def _hy_conv_part(cst, row0, u_ref, g_ref, bias_ref, hr_ref, hi_ref, fwd_ref, inv_ref, twr_ref, twi_ref,
                  w2_ref, w2c_ref, o_ref, xr_ref, xi_ref):
    bsz = u_ref.shape[0]

    def load_rows(start, count, stride):
        idx = pl.ds(row0 + start, count) if stride is None else pl.ds(row0 + start, count, stride=stride)
        return jnp.concatenate([u_ref[b, idx, :] for b in range(bsz)], axis=1)
    _hy_outer_fwd(load_rows, cst, fwd_ref, xr_ref, xi_ref)

    def body(k1, c):
        rows, twr, twi, yr, yi = _hy_inner_fwd(k1, xr_ref, xi_ref, twr_ref, twi_ref, w2_ref)
        hr = jnp.concatenate([hr_ref[rows, :]] * bsz, axis=1)
        hi = jnp.concatenate([hi_ref[rows, :]] * bsz, axis=1)
        zr = yr * hr - yi * hi
        zi = yr * hi + yi * hr
        vs = jnp.dot(w2c_ref[...], jnp.concatenate([zr, zi], axis=0).astype(BF16), preferred_element_type=F32)
        vr, vi = vs[:FFT_N2], vs[FFT_N2:]
        _sc_store(xr_ref, rows, vr * twr + vi * twi)
        _sc_store(xi_ref, rows, vi * twr - vr * twi)
        return c
    _hy_for_k1(cst, body)

    bias = bias_ref[...]

    def emit(idx, y):
        for b in range(bsz):
            yb = y[:, b * LANE:(b + 1) * LANE]
            o_ref[b, idx, :] = (g_ref[b, idx, :] * (yb + bias * u_ref[b, idx, :])).astype(o_ref.dtype)

    if cst.small:
        for n1 in range(cst.rows):
            y = sum(float(cst.inv[n1, k1]) * _sc_load(xr_ref, pl.ds(k1 * FFT_N2, FFT_N2))
                    + float(cst.inv[n1, cst.khp + k1]) * _sc_load(xi_ref, pl.ds(k1 * FFT_N2, FFT_N2))
                    for k1 in range(cst.kh))
            emit(pl.ds(row0 + n1 * FFT_N2, FFT_N2), y)
        return

    def out_body(n2, c):
        z = jnp.concatenate([_sc_load(xr_ref, pl.ds(n2, cst.khp, stride=FFT_N2)),
                             _sc_load(xi_ref, pl.ds(n2, cst.khp, stride=FFT_N2))], axis=0)
        y = jnp.dot(inv_ref[...], z, preferred_element_type=F32, precision=HI)
        emit(pl.ds(row0 + n2, cst.rows, stride=FFT_N2), y)
        return c
    lax.fori_loop(0, FFT_N2, out_body, 0, unroll=HY_UNROLL)


def _hy_conv_kernel(*refs, parts, row0s, zero_rows):
    u_ref, g_ref, bias_ref, w2_ref, w2c_ref = refs[:5]
    o_ref, xr_ref, xi_ref = refs[-3:]
    pos = 5
    for cst, row0 in zip(parts, row0s):
        hr_ref, hi_ref, fwd_ref, inv_ref, twr_ref, twi_ref = refs[pos:pos + 6]
        pos += 6
        _hy_conv_part(cst, row0, u_ref, g_ref, bias_ref, hr_ref, hi_ref, fwd_ref, inv_ref, twr_ref, twi_ref,
                      w2_ref, w2c_ref, o_ref, xr_ref, xi_ref)
    if zero_rows is not None:
        lo, hi = zero_rows
        o_ref[:, lo:hi, :] = jnp.zeros((o_ref.shape[0], hi - lo, o_ref.shape[2]), o_ref.dtype)


def hyena_conv(u, u_blk, gate, g_blk, bias, order, spectra, segs, zero_rows):
    out_dtype = F32
    bsz, t, _ = u.shape
    nt = MIX_WIDTH // LANE
    parts = tuple(_HyDft(n) for _, n in segs)
    args = [u, gate, bias, jnp.asarray(_W2).astype(BF16), jnp.asarray(_W2C).astype(BF16)]
    mat = pl.BlockSpec((2 * FFT_N2, 2 * FFT_N2), lambda j: (0, 0))
    in_specs = [pl.BlockSpec((bsz, t, LANE), lambda j: (0, 0, u_blk + j)),
                pl.BlockSpec((bsz, t, LANE), lambda j: (0, 0, g_blk + j)),
                pl.BlockSpec((1, LANE), lambda j: (0, j)), mat, mat]
    for (hr, hi), cst in zip(spectra, parts):
        args += [hr, hi, jnp.asarray(cst.fwd), jnp.asarray(cst.inv), jnp.asarray(cst.tw_re), jnp.asarray(cst.tw_im)]
        hspec = pl.BlockSpec((cst.kh * FFT_N2, LANE), lambda j: (0, order * nt + j))
        in_specs += [hspec, hspec, pl.BlockSpec(cst.fwd.shape, lambda j: (0, 0)),
                     pl.BlockSpec(cst.inv.shape, lambda j: (0, 0)),
                     pl.BlockSpec(cst.tw_re.shape, lambda j: (0, 0, 0)),
                     pl.BlockSpec(cst.tw_im.shape, lambda j: (0, 0, 0))]
    rows = max(c.khp for c in parts) * FFT_N2
    return pl.pallas_call(
        functools.partial(_hy_conv_kernel, parts=parts, row0s=tuple(r for r, _ in segs), zero_rows=zero_rows),
        grid=(nt,),
        in_specs=in_specs,
        out_specs=pl.BlockSpec((bsz, t, LANE), lambda j: (0, 0, j)),
        out_shape=jax.ShapeDtypeStruct((bsz, t, MIX_WIDTH), out_dtype),
        scratch_shapes=[pltpu.VMEM((bsz, rows, LANE), F32)] * 2,
        compiler_params=_cparams(("parallel",)),
        name="hyena_conv",
    )(*args)


def _merge_kernel(ya_ref, yb_ref, yc_ref, yd_ref, ga_ref, gb_ref, gc_ref, gd_ref, w_ref, o_ref):
    acc = None
    for b, (y_ref, g_ref) in enumerate(((ya_ref, ga_ref), (yb_ref, gb_ref), (yc_ref, gc_ref), (yd_ref, gd_ref))):
        t = jax.nn.sigmoid(g_ref[...]) * jnp.dot(y_ref[...].astype(BF16), w_ref[b], preferred_element_type=F32)
        acc = t if acc is None else acc + t
    o_ref[...] = acc.astype(o_ref.dtype)


def merge_branches(ys, gates, w_branch, tm=512, tn=512):
    t, ch = ys[0].shape
    nj = D_MODEL // tn
    yspec = pl.BlockSpec((tm, ch), lambda j, i: (i, 0))
    gspecs = [pl.BlockSpec((tm, tn), functools.partial(lambda j, i, b: (i, b * nj + j), b=b)) for b in range(N_BRANCH)]
    return pl.pallas_call(
        _merge_kernel,
        grid=(nj, t // tm),
        in_specs=[yspec] * 4 + gspecs + [pl.BlockSpec((N_BRANCH, ch, tn), lambda j, i: (0, 0, j))],
        out_specs=pl.BlockSpec((tm, tn), lambda j, i: (i, j)),
        out_shape=jax.ShapeDtypeStruct((t, D_MODEL), BF16),
        compiler_params=_cparams(("parallel", "parallel")),
    )(*ys, gates, gates, gates, gates, w_branch)


def _mm_resid_kernel(a_ref, w_ref, x_ref, g_ref, o_ref):
    o_ref[...] = x_ref[...] + g_ref[...] * jnp.dot(a_ref[...], w_ref[...], preferred_element_type=F32)


def matmul_gated_residual(a, w, x, gate, tiles_per_batch, tn=512):
    t, k = a.shape
    n = w.shape[1]
    return pl.pallas_call(
        _mm_resid_kernel,
        grid=(n // tn, t // ROW_TILE),
        in_specs=[pl.BlockSpec((ROW_TILE, k), lambda j, i: (i, 0)),
                  pl.BlockSpec((k, tn), lambda j, i: (0, j)),
                  pl.BlockSpec((ROW_TILE, tn), lambda j, i: (i, j)),
                  pl.BlockSpec((None, 1, tn), lambda j, i: (_seg_index(i, tiles_per_batch), 0, j))],
        out_specs=pl.BlockSpec((ROW_TILE, tn), lambda j, i: (i, j)),
        out_shape=jax.ShapeDtypeStruct((t, n), F32),
        compiler_params=_cparams(("parallel", "parallel")),
    )(a, w, x, gate)


ROUTE_OFF = MOE_GROUPS
TOK_S = D_MODEL // LANE
DSP_S = TOK_S + 8
MOE_TM = 512


def _store_token_major(ref, val, slab=TOK_S):
    rows = val.shape[0]
    for s in range(TOK_S):
        ref[pl.ds(s, rows, stride=slab), :] = val[:, s * LANE:(s + 1) * LANE]


def _load_token_major(ref, rows, slab=TOK_S):
    return jnp.concatenate([ref[pl.ds(s, rows, stride=slab), :] for s in range(TOK_S)], axis=1)


def _router_kernel(x_ref, sh_ref, sc_ref, wr_ref, br_ref, h_ref, comb_ref):
    x = x_ref[...]
    ms = jnp.mean(x * x, axis=-1, keepdims=True)
    h = x * lax.rsqrt(ms + NORM_EPS) * (1.0 + sc_ref[...]) + sh_ref[...]
    _store_token_major(h_ref, h, DSP_S)
    logits = jnp.dot(h, wr_ref[...], preferred_element_type=F32, precision=HI) + br_ref[...]
    lane = lax.broadcasted_iota(jnp.int32, logits.shape, 1).astype(F32)
    neg = -jnp.inf
    gmask = lane < MOE_GROUPS
    glog = jnp.where(gmask, logits, neg)
    gmax = jnp.max(glog, axis=-1, keepdims=True)
    g_idx = jnp.min(jnp.where(glog == gmax, lane, float(LANE)), axis=-1, keepdims=True)
    p_g = 1.0 / jnp.sum(jnp.exp(glog - gmax), axis=-1, keepdims=True)
    e_lo = ROUTE_OFF + g_idx * MOE_PER_GROUP
    emask = (lane >= e_lo) & (lane < e_lo + MOE_PER_GROUP)
    v1 = jnp.max(jnp.where(emask, logits, neg), axis=-1, keepdims=True)
    i1 = jnp.min(jnp.where(emask & (logits == v1), lane, float(LANE)), axis=-1, keepdims=True)
    emask2 = emask & (lane != i1)
    v2 = jnp.max(jnp.where(emask2, logits, neg), axis=-1, keepdims=True)
    i2 = jnp.min(jnp.where(emask2 & (logits == v2), lane, float(LANE)), axis=-1, keepdims=True)
    e21 = jnp.exp(v2 - v1)
    w1 = p_g / (1.0 + e21)
    w2 = p_g * e21 / (1.0 + e21)
    comb = jnp.where(lane == i1, w1, jnp.where(lane == i2, w2, 0.0))
    route = jnp.where(lane == 0, g_idx, comb)
    comb_ref[...] = route
    rows = x.shape[0]
    h_ref[pl.ds(TOK_S, rows, stride=DSP_S), :] = route
    for s in range(TOK_S + 1, DSP_S):
        h_ref[pl.ds(s, rows, stride=DSP_S), :] = jnp.zeros_like(route)


def moe_router(x, shift, scale, group_w, group_b, expert_w, expert_b, tiles_per_batch):
    t, d = x.shape
    wr = _pad_cols(jnp.concatenate([group_w, expert_w], axis=1), LANE)
    br = _pad_cols(jnp.concatenate([group_b, expert_b])[None, :], LANE)
    seg = lambda i: (_seg_index(i, tiles_per_batch), 0, 0)
    return pl.pallas_call(
        _router_kernel,
        grid=(t // ROW_TILE,),
        in_specs=[pl.BlockSpec((ROW_TILE, d), lambda i: (i, 0)),
                  pl.BlockSpec((None, 1, d), seg), pl.BlockSpec((None, 1, d), seg),
                  pl.BlockSpec((d, LANE), lambda i: (0, 0)), pl.BlockSpec((1, LANE), lambda i: (0, 0))],
        out_specs=[pl.BlockSpec((ROW_TILE * DSP_S, LANE), lambda i: (i, 0)),
                   pl.BlockSpec((ROW_TILE, LANE), lambda i: (i, 0))],
        out_shape=[jax.ShapeDtypeStruct((t * DSP_S, LANE), F32), jax.ShapeDtypeStruct((t, LANE), F32)],
        compiler_params=_cparams(("parallel",)),
    )(x, shift, scale, wr, br)


def moe_plan(route, n_tiles):
    g = route[:, 0].astype(jnp.int32)
    oh = (g[:, None] == jnp.arange(MOE_GROUPS, dtype=jnp.int32)[None, :]).astype(jnp.int32)
    cnt = jnp.sum(oh, axis=0)
    rank = jnp.sum((jnp.cumsum(oh, axis=0) - oh) * oh, axis=1)
    ptiles = (cnt + MOE_TM - 1) // MOE_TM
    ends = jnp.cumsum(ptiles)
    off = (ends - ptiles) * MOE_TM
    pos = jnp.sum(oh * off[None, :], axis=1) + rank
    n_act = ends[-1]
    tid = jnp.arange(n_tiles, dtype=jnp.int32)
    tsrc = jnp.minimum(tid, n_act - 1)
    tgrp = jnp.sum((tsrc[:, None] >= ends[None, :]).astype(jnp.int32), axis=1)
    tact = (tid < n_act).astype(jnp.int32)
    return pos, tsrc, tgrp, tact


def _slab(ref, row, size):
    return ref.at[pl.ds(pl.multiple_of(row * size, size), size)]


def _moe_scatter_kernel(pos_ref, src_ref, init_ref, dst_ref, sem):
    del init_ref
    base = pl.program_id(0) * ROW_TILE

    def issue(r, c):
        pltpu.make_async_copy(_slab(src_ref, r, DSP_S), _slab(dst_ref, pos_ref[base + r], DSP_S), sem).start()
        return c
    lax.fori_loop(0, ROW_TILE, issue, 0)
    pltpu.make_async_copy(src_ref, dst_ref.at[pl.ds(0, ROW_TILE * DSP_S)], sem).wait()


def moe_scatter(slabs, pos, n_slots):
    t = pos.shape[0]
    init = jnp.zeros((n_slots * DSP_S, LANE), F32)
    return pl.pallas_call(
        _moe_scatter_kernel,
        grid_spec=pltpu.PrefetchScalarGridSpec(
            num_scalar_prefetch=1, grid=(t // ROW_TILE,),
            in_specs=[pl.BlockSpec((ROW_TILE * DSP_S, LANE), lambda i, p: (i, 0)),
                      pl.BlockSpec(memory_space=pl.ANY)],
            out_specs=pl.BlockSpec(memory_space=pl.ANY),
            scratch_shapes=[pltpu.SemaphoreType.DMA(())]),
        out_shape=jax.ShapeDtypeStruct(init.shape, F32),
        input_output_aliases={2: 0},
        compiler_params=_cparams(("arbitrary",)),
    )(pos, slabs, init)


def _moe_expert_kernel(tsrc_ref, tgrp_ref, tact_ref, xs_ref, wg_ref, wu_ref, wd_ref, ys_ref, x_ref, acc_ref):
    i, e, f = pl.program_id(0), pl.program_id(1), pl.program_id(2)
    active = tact_ref[i] == 1
    first = (e == 0) & (f == 0)
    last = (e == pl.num_programs(1) - 1) & (f == pl.num_programs(2) - 1)

    @pl.when(active & first)
    def _():
        x_ref[...] = _load_token_major(xs_ref, MOE_TM, DSP_S).astype(BF16)
        acc_ref[...] = jnp.zeros_like(acc_ref)

    @pl.when(active)
    def _():
        x = x_ref[...]
        a = jnp.dot(x, wg_ref[...].astype(BF16), preferred_element_type=F32)
        u = jnp.dot(x, wu_ref[...].astype(BF16), preferred_element_type=F32)
        act = (a * jax.nn.sigmoid(a) * u).astype(BF16)
        route = xs_ref[pl.ds(TOK_S, MOE_TM, stride=DSP_S), :]
        lane = lax.broadcasted_iota(jnp.int32, route.shape, 1)
        mine = lane == ROUTE_OFF + tgrp_ref[i] * MOE_PER_GROUP + e
        w = jnp.sum(jnp.where(mine, route, 0.0), axis=-1, keepdims=True)
        acc_ref[...] += w * jnp.dot(act, wd_ref[...].astype(BF16), preferred_element_type=F32)

    @pl.when(active & last)
    def _():
        _store_token_major(ys_ref, acc_ref[...])

    @pl.when(jnp.logical_not(active) & first)
    def _():
        ys_ref[...] = jnp.zeros_like(ys_ref)


def moe_experts(xs, tsrc, tgrp, tact, w_gate, w_up, w_down, tf=256):
    n_tiles = tsrc.shape[0]
    _, d, ff = w_gate.shape
    wsel = lambda i, e, f, ts, tg, ta: (tg[i] * MOE_PER_GROUP + e, 0, f)
    rows = lambda i, e, f, ts, tg, ta: (ts[i], 0)
    return pl.pallas_call(
        _moe_expert_kernel,
        grid_spec=pltpu.PrefetchScalarGridSpec(
            num_scalar_prefetch=3, grid=(n_tiles, MOE_PER_GROUP, ff // tf),
            in_specs=[pl.BlockSpec((MOE_TM * DSP_S, LANE), rows),
                      pl.BlockSpec((None, d, tf), wsel),
                      pl.BlockSpec((None, d, tf), wsel),
                      pl.BlockSpec((None, tf, d), lambda i, e, f, ts, tg, ta: (tg[i] * MOE_PER_GROUP + e, f, 0))],
            out_specs=pl.BlockSpec((MOE_TM * TOK_S, LANE), lambda i, e, f, ts, tg, ta: (i, 0)),
            scratch_shapes=[pltpu.VMEM((MOE_TM, d), BF16), pltpu.VMEM((MOE_TM, d), F32)]),
        out_shape=jax.ShapeDtypeStruct((n_tiles * MOE_TM * TOK_S, LANE), F32),
        compiler_params=_cparams(("arbitrary", "arbitrary", "arbitrary")),
    )(tsrc, tgrp, tact, xs, w_gate, w_up, w_down)


def _moe_gather_kernel(pos_ref, ys_ref, x_ref, g_ref, o_ref, buf_ref, sem):
    base = pl.program_id(0) * ROW_TILE

    def issue(r, c):
        pltpu.make_async_copy(_slab(ys_ref, pos_ref[base + r], TOK_S), _slab(buf_ref, r, TOK_S), sem).start()
        return c
    lax.fori_loop(0, ROW_TILE, issue, 0)
    pltpu.make_async_copy(ys_ref.at[pl.ds(0, ROW_TILE * TOK_S)], buf_ref, sem).wait()
    o_ref[...] = x_ref[...] + g_ref[...] * _load_token_major(buf_ref, ROW_TILE)


def moe_gather_residual(ys, pos, x, gate, tiles_per_batch):
    t, d = x.shape
    return pl.pallas_call(
        _moe_gather_kernel,
        grid_spec=pltpu.PrefetchScalarGridSpec(
            num_scalar_prefetch=1, grid=(t // ROW_TILE,),
            in_specs=[pl.BlockSpec(memory_space=pl.ANY),
                      pl.BlockSpec((ROW_TILE, d), lambda i, p: (i, 0)),
                      pl.BlockSpec((None, 1, d), lambda i, p: (_seg_index(i, tiles_per_batch), 0, 0))],
            out_specs=pl.BlockSpec((ROW_TILE, d), lambda i, p: (i, 0)),
            scratch_shapes=[pltpu.VMEM((ROW_TILE * TOK_S, LANE), F32), pltpu.SemaphoreType.DMA(())]),
        out_shape=jax.ShapeDtypeStruct((t, d), F32),
        compiler_params=_cparams(("arbitrary",)),
    )(pos, ys, x, gate)


def _final_norm_kernel(x_ref, w_ref, o_ref):
    x = x_ref[...]
    ms = jnp.mean(x * x, axis=-1, keepdims=True)
    o_ref[...] = x * lax.rsqrt(ms + NORM_EPS) * w_ref[...]


def final_norm(x, w, n_ctx):
    bsz, t, d = x.shape
    off = n_ctx // ROW_TILE
    return pl.pallas_call(
        _final_norm_kernel,
        grid=(bsz, (t - n_ctx) // ROW_TILE),
        in_specs=[pl.BlockSpec((None, ROW_TILE, d), lambda b, i: (b, i + off, 0)),
                  pl.BlockSpec((1, d), lambda b, i: (0, 0))],
        out_specs=pl.BlockSpec((None, ROW_TILE, d), lambda b, i: (b, i, 0)),
        out_shape=jax.ShapeDtypeStruct((bsz, t - n_ctx, d), F32),
        compiler_params=_cparams(("parallel", "parallel")),
    )(x, w.reshape(1, d))


def _lat_colmajor(t, n_ctx, inverse=False):
    bsz, _, ch = t.shape
    lat = t[:, n_ctx:]
    rows = lat.shape[1] // GRID_W
    shp = (bsz, GRID_W, rows, ch) if inverse else (bsz, rows, GRID_W, ch)
    lat = lat.reshape(shp).transpose(0, 2, 1, 3).reshape(bsz, -1, ch)
    return jnp.concatenate([t[:, :n_ctx], lat], axis=1)


def _seg_table(ctx_vec, lat_mat):
    bsz = lat_mat.shape[0]
    tab = jnp.stack([jnp.broadcast_to(ctx_vec[None, :], lat_mat.shape), lat_mat], axis=1)
    return tab.reshape(2 * bsz, 1, -1)


def kernel(x, c, ctx, c_ctx, ada_w, ada_b, w_in, s5_lambda_re, s5_lambda_im, s5_log_step, s5_b_re, s5_b_im, s5_c_re, s5_c_im, s5_d, s5_glu_w, ssd_conv_w, ssd_conv_b, ssd_a_log, ssd_dt_bias, ssd_d, ssd_norm_w, gla_gate_w, gla_gate_b, gla_norm_w, hy_conv_w, hy_conv_b, hy_w1, hy_b1, hy_freq1, hy_w2, hy_b2, hy_freq2, hy_w3, hy_bias, w_branch, w_out, moe_group_w, moe_group_b, moe_expert_w, moe_expert_b, moe_w_gate, moe_w_up, moe_w_down, final_norm_w):
    bsz, n_lat, d = x.shape
    n_ctx = ctx.shape[1]
    depth = ada_w.shape[0]
    t_b = n_ctx + n_lat
    t_all = bsz * t_b
    tiles_pb = t_b // ROW_TILE
    assert n_ctx == ROW_TILE and n_lat % ROW_TILE == 0 and n_lat == FFT_N // 2 and bsz + 1 <= 8

    xa = jnp.concatenate([ctx, x], axis=1).reshape(t_all, d)
    cc = jnp.pad(jnp.concatenate([c, c_ctx[None, :]], axis=0), ((0, 8 - bsz - 1), (0, 0)))
    mod = adaln(cc, ada_w, ada_b)

    for li in range(depth):
        last = li == depth - 1
        m6 = mod[li].reshape(8, 6, d)
        tabs = [_seg_table(m6[bsz, k], m6[:bsz, k]) for k in range(6)]
        h = modnorm(xa, tabs[0], tabs[1], tiles_pb)

        wl = w_in[li].astype(BF16)
        seg = lambda k0, k1: wl[:, IN_OFFS[k0]:IN_OFFS[k1]]
        (u_s5,) = proj_split(h, seg(0, 1), (MIX_WIDTH,))
        w_ssd = jnp.concatenate([seg(1, 3), _pad_cols(seg(3, 4), LANE)], axis=1)
        z_ssd, xbc_raw, dt_raw = proj_split(h, w_ssd, (MIX_WIDTH, SSD_CONV_CH, LANE))
        w_gla = jnp.concatenate([seg(4, 8), _pad_cols(seg(8, 9), LANE)], axis=1)
        q_g, k_g, v_g, g_g, r_g = proj_split(h, w_gla, (GLA_KEY_WIDTH, GLA_KEY_WIDTH, MIX_WIDTH, MIX_WIDTH, LANE))
        (p_hy,) = proj_split(h, seg(9, 10), (3 * MIX_WIDTH,))
        gates = matmul(h, seg(10, 11), 512, 1024)

        b3 = lambda a: a.reshape(bsz, t_b, a.shape[-1])

        ys = []
        for dr in range(2):
            wts = s5_weights(s5_lambda_re[li, dr], s5_lambda_im[li, dr], s5_log_step[li, dr], s5_b_re[li, dr],
                             s5_b_im[li, dr], s5_c_re[li, dr], s5_c_im[li, dr], dr == 1)
            ys.append(s5_scan(b3(u_s5), wts, n_ctx, dr == 1).reshape(t_all, MIX_WIDTH))
        ya = s5_out(ys[0], ys[1], u_s5, s5_d[li], s5_glu_w[li])

        z_cm = _lat_colmajor(b3(z_ssd), n_ctx)
        xbc_cm = _lat_colmajor(b3(xbc_raw), n_ctx)
        dt_cm = _lat_colmajor(b3(dt_raw), n_ctx)
        xbc_act = dwconv_seq(xbc_cm, ssd_conv_w[li], ssd_conv_b[li], n_ctx, silu=True)
        bias_row = _pad_cols(ssd_dt_bias[li].reshape(1, -1), LANE)
        a_row = _pad_cols(-jnp.exp(ssd_a_log[li]).reshape(1, -1), LANE)
        yd_ssd = [ssd_scan(xbc_act, dt_cm, bias_row, a_row, n_ctx, dr).reshape(t_all, MIX_WIDTH) for dr in range(2)]
        d_row = jnp.repeat(ssd_d[li], SSD_HEAD_DIM)[None, :]
        yb_cm = ssd_out(yd_ssd[0], yd_ssd[1], xbc_act.reshape(t_all, SSD_CONV_CH), z_cm.reshape(t_all, MIX_WIDTH),
                        d_row, ssd_norm_w[li])
        yb = _lat_colmajor(b3(yb_cm), n_ctx, inverse=True).reshape(t_all, MIX_WIDTH)

        os_ = []
        for dr in range(2):
            wg = jnp.zeros((LANE, GLA_KEY_WIDTH), F32).at[dr * GLA_GATE_RANK:(dr + 1) * GLA_GATE_RANK].set(
                gla_gate_w[li, dr])
            os_.append(gla_scan(b3(q_g), b3(k_g), b3(v_g), b3(r_g), wg, gla_gate_b[li, dr][None, :], n_ctx,
                                dr == 1).reshape(t_all, MIX_WIDTH))
        yc = gla_out(os_[0], os_[1], g_g, gla_norm_w[li])

        u_hy = dwconv_seq(b3(p_hy), hy_conv_w[li], hy_conv_b[li], n_ctx, silu=False)
        hy_p = (hy_w1[li], hy_b1[li], hy_freq1[li], hy_w2[li], hy_b2[li], hy_freq2[li], hy_w3[li])
        filts = [hyena_filters(n_lat, *hy_p)]
        segs = [(n_ctx, n_lat)]
        if not last:
            filts.append(hyena_filters(n_ctx, *hy_p))
            segs.append((0, n_ctx))
        spectra = hyena_spectrum(filts)
        zero_rows = (0, n_ctx) if last else None
        nt = MIX_WIDTH // LANE
        y1 = hyena_conv(u_hy, 0, u_hy, nt, hy_bias[li][0:1], 0, spectra, segs, zero_rows)
        yd = hyena_conv(y1, 0, u_hy, 2 * nt, hy_bias[li][1:2], 1, spectra, segs, zero_rows)
        yd = yd.reshape(t_all, MIX_WIDTH)

        merged = merge_branches((ya, yb, yc, yd), gates, w_branch[li].astype(BF16))
        xa = matmul_gated_residual(merged, w_out[li].astype(BF16), xa, tabs[2], tiles_pb)

        h2, route = moe_router(xa, tabs[3], tabs[4], moe_group_w[li], moe_group_b[li], moe_expert_w[li],
                               moe_expert_b[li], tiles_pb)
        n_tiles = -(-t_all // MOE_TM) + MOE_GROUPS
        pos, tsrc, tgrp, tact = moe_plan(route, n_tiles)
        xs = moe_scatter(h2, pos, n_tiles * MOE_TM)
        ys = moe_experts(xs, tsrc, tgrp, tact, moe_w_gate[li], moe_w_up[li], moe_w_down[li])
        xa = moe_gather_residual(ys, pos, xa, tabs[5], tiles_pb)

    return final_norm(xa.reshape(bsz, t_b, d), final_norm_w, n_ctx)
```

```python
import functools
import math

import numpy as np
import jax
import jax.numpy as jnp
from jax import lax
from jax.experimental import pallas as pl
from jax.experimental.pallas import tpu as pltpu

F32 = jnp.float32
BF16 = jnp.bfloat16
HI = lax.Precision.HIGHEST

D_MODEL = 2048
GRID_W = 64
NORM_EPS = 1e-6
N_BRANCH = 4
MIX_WIDTH = 768
S5_GROUP = 16
S5_GROUPS = MIX_WIDTH // S5_GROUP
S5_STATE = 64
SSD_HEAD_DIM = 64
SSD_HEADS = MIX_WIDTH // SSD_HEAD_DIM
SSD_GROUPS = 2
SSD_STATE = 64
SSD_CHUNK = 128
SSD_CONV_CH = MIX_WIDTH + 2 * SSD_GROUPS * SSD_STATE
GLA_HEADS = 6
GLA_KEY_WIDTH = MIX_WIDTH // 2
GLA_DK = GLA_KEY_WIDTH // GLA_HEADS
GLA_DV = MIX_WIDTH // GLA_HEADS
GLA_GATE_RANK = 16
GLA_TAU = 16.0
GLA_CHUNK = 64
HY_ORDER = 2
HY_FILTER_DIM = 64
HY_BANDS = 16
HY_EMB = 2 * HY_BANDS + 1
HY_MAX_DECAY = math.log(1e-2) / 0.3
HY_MIN_DECAY = math.log(1e-2) / 1.5
MOE_GROUPS = 4
MOE_PER_GROUP = 4
MOE_EXPERTS = MOE_GROUPS * MOE_PER_GROUP
MOE_FF = 1024

IN_SIZES = (MIX_WIDTH, MIX_WIDTH, SSD_CONV_CH, 2 * SSD_HEADS, GLA_KEY_WIDTH, GLA_KEY_WIDTH, MIX_WIDTH, MIX_WIDTH,
            2 * GLA_GATE_RANK, (HY_ORDER + 1) * MIX_WIDTH, N_BRANCH * D_MODEL)
IN_OFFS = tuple(int(v) for v in np.cumsum((0,) + IN_SIZES))

LANE = 128
ROW_TILE = 256
VMEM_LIMIT = 56 * 1024 * 1024

FFT_N1 = 64
FFT_N2 = 128
FFT_N = FFT_N1 * FFT_N2


def _cparams(sem):
    return pltpu.CompilerParams(dimension_semantics=sem, vmem_limit_bytes=VMEM_LIMIT)


def _pad_cols(a, width):
    return jnp.pad(a, [(0, 0)] * (a.ndim - 1) + [(0, width - a.shape[-1])])


def _mm_kernel(x_ref, w_ref, o_ref, *, precision):
    o_ref[...] = jnp.dot(x_ref[...], w_ref[...], preferred_element_type=F32,
                         precision=precision).astype(o_ref.dtype)


def matmul(x, w, tm, tn, out_dtype=F32, precision=None):
    m, k = x.shape
    n = w.shape[1]
    return pl.pallas_call(
        functools.partial(_mm_kernel, precision=precision),
        grid=(n // tn, m // tm),
        in_specs=[pl.BlockSpec((tm, k), lambda j, i: (i, 0)), pl.BlockSpec((k, tn), lambda j, i: (0, j))],
        out_specs=pl.BlockSpec((tm, tn), lambda j, i: (i, j)),
        out_shape=jax.ShapeDtypeStruct((m, n), out_dtype),
        compiler_params=_cparams(("parallel", "parallel")),
    )(x, w)


def _proj_kernel(x_ref, w_ref, *o_refs):
    acc = jnp.dot(x_ref[...], w_ref[...], preferred_element_type=F32)
    off = 0
    for o_ref in o_refs:
        wd = o_ref.shape[-1]
        o_ref[...] = acc[:, off:off + wd].astype(o_ref.dtype)
        off += wd


def proj_split(x, w, widths, tm=512):
    m, k = x.shape
    n = w.shape[1]
    assert sum(widths) == n
    return pl.pallas_call(
        _proj_kernel,
        grid=(m // tm,),
        in_specs=[pl.BlockSpec((tm, k), lambda i: (i, 0)), pl.BlockSpec((k, n), lambda i: (0, 0))],
        out_specs=[pl.BlockSpec((tm, wd), lambda i: (i, 0)) for wd in widths],
        out_shape=[jax.ShapeDtypeStruct((m, wd), F32) for wd in widths],
        compiler_params=_cparams(("parallel",)),
    )(x, w)


def _adaln_kernel(c_ref, w_ref, b_ref, o_ref):
    c = c_ref[...]
    s = c * jax.nn.sigmoid(c)
    o_ref[...] = jnp.dot(s, w_ref[...], preferred_element_type=F32, precision=HI) + b_ref[...]


def adaln(cc, ada_w, ada_b, tn=1024):
    depth, d, n = ada_w.shape
    return pl.pallas_call(
        _adaln_kernel,
        grid=(depth, n // tn),
        in_specs=[pl.BlockSpec((8, d), lambda l, j: (0, 0)),
                  pl.BlockSpec((None, d, tn), lambda l, j: (l, 0, j)),
                  pl.BlockSpec((None, 1, tn), lambda l, j: (l, 0, j))],
        out_specs=pl.BlockSpec((None, 8, tn), lambda l, j: (l, 0, j)),
        out_shape=jax.ShapeDtypeStruct((depth, 8, n), F32),
        compiler_params=_cparams(("parallel", "parallel")),
    )(cc, ada_w, ada_b.reshape(depth, 1, n))


def _seg_index(i, tiles_per_batch):
    b = i // tiles_per_batch
    return 2 * b + jnp.where(i % tiles_per_batch == 0, 0, 1)


def _modnorm_kernel(x_ref, sh_ref, sc_ref, o_ref):
    x = x_ref[...]
    ms = jnp.mean(x * x, axis=-1, keepdims=True)
    o_ref[...] = (x * lax.rsqrt(ms + NORM_EPS) * (1.0 + sc_ref[...]) + sh_ref[...]).astype(o_ref.dtype)


def modnorm(x, shift, scale, tiles_per_batch):
    t, d = x.shape
    seg = lambda i: (_seg_index(i, tiles_per_batch), 0, 0)
    return pl.pallas_call(
        _modnorm_kernel,
        grid=(t // ROW_TILE,),
        in_specs=[pl.BlockSpec((ROW_TILE, d), lambda i: (i, 0)),
                  pl.BlockSpec((None, 1, d), seg), pl.BlockSpec((None, 1, d), seg)],
        out_specs=pl.BlockSpec((ROW_TILE, d), lambda i: (i, 0)),
        out_shape=jax.ShapeDtypeStruct((t, d), BF16),
        compiler_params=_cparams(("parallel",)),
    )(x, shift, scale)


def _dwconv_kernel(x_ref, w_ref, b_ref, o_ref, *, n_ctx, silu):
    x = x_ref[...]
    t = x.shape[0]
    rows = lax.broadcasted_iota(jnp.int32, x.shape, 0)
    prev = jnp.where((rows == 0) | (rows == n_ctx), 0.0, pltpu.roll(x, 1, 0))
    nxt = jnp.where((rows == n_ctx - 1) | (rows == t - 1), 0.0, pltpu.roll(x, t - 1, 0))
    y = w_ref[0:1, :] * prev + w_ref[1:2, :] * x + w_ref[2:3, :] * nxt + b_ref[...]
    if silu:
        y = y * jax.nn.sigmoid(y)
    o_ref[...] = y


def dwconv_seq(x, w, b, n_ctx, silu, tc=LANE):
    bsz, t, ch = x.shape
    return pl.pallas_call(
        functools.partial(_dwconv_kernel, n_ctx=n_ctx, silu=silu),
        grid=(bsz, ch // tc),
        in_specs=[pl.BlockSpec((None, t, tc), lambda b_, j: (b_, 0, j)),
                  pl.BlockSpec((3, tc), lambda b_, j: (0, j)),
                  pl.BlockSpec((1, tc), lambda b_, j: (0, j))],
        out_specs=pl.BlockSpec((None, t, tc), lambda b_, j: (b_, 0, j)),
        out_shape=jax.ShapeDtypeStruct((bsz, t, ch), F32),
        compiler_params=_cparams(("parallel", "parallel")),
    )(x, w, b.reshape(1, ch))


S5_Q = 8
S5_TG = LANE // S5_GROUP
S5_TS = S5_TG * S5_STATE
S5_NT = MIX_WIDTH // LANE


def _cmul(ar, ai, br, bi):
    return ar * br - ai * bi, ar * bi + ai * br


def s5_weights(lam_re, lam_im, log_step, b_re, b_im, c_re, c_im, reverse):
    g, p, q = S5_GROUPS, S5_STATE, S5_Q
    step = jnp.exp(log_step)[:, None]
    mag = jnp.exp(lam_re * step)
    ab_re = mag * jnp.cos(lam_im * step)
    ab_im = mag * jnp.sin(lam_im * step)
    den = lam_re * lam_re + lam_im * lam_im
    zr, zi = _cmul(ab_re - 1.0, ab_im, lam_re, -lam_im)
    zr, zi = zr / den, zi / den
    bb_re, bb_im = _cmul(zr[..., None], zi[..., None], b_re, b_im)
    pw_re, pw_im = [jnp.ones_like(ab_re)], [jnp.zeros_like(ab_im)]
    for _ in range(q):
        r, i = _cmul(pw_re[-1], pw_im[-1], ab_re, ab_im)
        pw_re.append(r)
        pw_im.append(i)
    pw_re, pw_im = jnp.stack(pw_re), jnp.stack(pw_im)
    cp_re = c_re[None] * pw_re[:, :, None, :] - c_im[None] * pw_im[:, :, None, :]
    cp_im = c_re[None] * pw_im[:, :, None, :] + c_im[None] * pw_re[:, :, None, :]
    kd = (jnp.einsum('dghp,gpk->dghk', cp_re[:q], bb_re, precision=HI)
          - jnp.einsum('dghp,gpk->dghk', cp_im[:q], bb_im, precision=HI))
    def group_blocks(tab, w):
        lead = tab.shape[:-3]
        rows = tab.shape[-2]
        expand = jnp.asarray(np.tile(np.eye(w, dtype=np.float32), (1, S5_TG)))
        wide = jnp.dot(tab, expand, precision=HI).reshape(lead + (S5_NT, S5_TG, rows, S5_TG * w))
        own = (np.arange(S5_TG * w)[None, None, :] // w) == np.arange(S5_TG)[:, None, None]
        return jnp.where(jnp.asarray(own), wide, 0.0).reshape(lead + (S5_NT, S5_TG * rows, S5_TG * w))

    kblk = group_blocks(jnp.swapaxes(kd, -1, -2), S5_GROUP)
    zero_blk = jnp.zeros_like(kblk[0])
    lag = (lambda s, r: s - r) if reverse else (lambda s, r: r - s)
    ktoep = jnp.concatenate(
        [jnp.concatenate([kblk[lag(s, r)] if lag(s, r) >= 0 else zero_blk for r in range(q)], axis=-1)
         for s in range(q)], axis=-2)
    e_of_s = (np.arange(q) if reverse else (q - 1 - np.arange(q)))
    pb_re = pw_re[e_of_s][..., None] * bb_re[None] - pw_im[e_of_s][..., None] * bb_im[None]
    pb_im = pw_re[e_of_s][..., None] * bb_im[None] + pw_im[e_of_s][..., None] * bb_re[None]
    pblk = [group_blocks(jnp.swapaxes(t, -1, -2), p) for t in (pb_re, pb_im)]
    bw = jnp.concatenate([jnp.concatenate([pblk[0][s], pblk[1][s]], axis=-1) for s in range(q)], axis=-2)
    f_of_r = (q - np.arange(q)) if reverse else (np.arange(q) + 1)
    cblk = [group_blocks(jnp.swapaxes(t[f_of_r], -1, -2), S5_GROUP) for t in (cp_re, -cp_im)]
    cw = jnp.concatenate([jnp.concatenate([cb[r] for r in range(q)], axis=-1) for cb in cblk], axis=-2)
    ar, ai = pw_re[q], pw_im[q]
    lv = []
    for _ in range(16):
        lv.append(jnp.concatenate([ar.reshape(S5_NT, S5_TS), ai.reshape(S5_NT, S5_TS)], axis=-1))
        ar, ai = _cmul(ar, ai, ar, ai)
    a2 = jnp.stack(lv, axis=1)
    return ktoep.astype(BF16), bw.astype(BF16), cw.astype(BF16), a2


def _s5_seg_scan(xr, xi, a2_ref, reverse):
    n = xr.shape[0]
    rows = lax.broadcasted_iota(jnp.int32, xr.shape, 0)
    k, shift = 0, 1
    while shift < n:
        ar = a2_ref[k:k + 1, :S5_TS]
        ai = a2_ref[k:k + 1, S5_TS:]
        if reverse:
            keep = rows < n - shift
            sr, si = pltpu.roll(xr, n - shift, 0), pltpu.roll(xi, n - shift, 0)
        else:
            keep = rows >= shift
            sr, si = pltpu.roll(xr, shift, 0), pltpu.roll(xi, shift, 0)
        sr = jnp.where(keep, sr, 0.0)
        si = jnp.where(keep, si, 0.0)
        xr, xi = xr + ar * sr - ai * si, xi + ar * si + ai * sr
        k += 1
        shift *= 2
    return xr, xi


def _s5_entering(sr, si, carry, a2_ref, reverse):
    n = sr.shape[0]
    rows = lax.broadcasted_iota(jnp.int32, sr.shape, 0)
    first = (n - 1) if reverse else 0
    if carry is not None:
        cr, ci = carry
        ar, ai = a2_ref[0:1, :S5_TS], a2_ref[0:1, S5_TS:]
        ir, ii = _cmul(ar, ai, cr, ci)
        sr = jnp.where(rows == first, sr + ir, sr)
        si = jnp.where(rows == first, si + ii, si)
    hr, hi = _s5_seg_scan(sr, si, a2_ref, reverse)
    if reverse:
        out = (hr[0:1], hi[0:1])
        er, ei = pltpu.roll(hr, n - 1, 0), pltpu.roll(hi, n - 1, 0)
    else:
        out = (hr[n - 1:n], hi[n - 1:n])
        er, ei = pltpu.roll(hr, 1, 0), pltpu.roll(hi, 1, 0)
    if carry is None:
        er = jnp.where(rows == first, 0.0, er)
        ei = jnp.where(rows == first, 0.0, ei)
    else:
        er = jnp.where(rows == first, cr, er)
        ei = jnp.where(rows == first, ci, ei)
    return er, ei, out


def _s5_kernel(u_ref, kt_ref, bw_ref, cw_ref, a2_ref, y_ref, *, n_ctx_rows, reverse):
    n = u_ref.shape[0] // S5_Q
    x = jnp.concatenate([u_ref[pl.ds(s, n, stride=S5_Q), :] for s in range(S5_Q)], axis=1).astype(BF16)
    y = jnp.dot(x, kt_ref[...], preferred_element_type=F32)
    st = jnp.dot(x, bw_ref[...], preferred_element_type=F32)
    sr, si = st[:, :S5_TS], st[:, S5_TS:]
    cr, ci, carry = _s5_entering(sr[:n_ctx_rows], si[:n_ctx_rows], None, a2_ref, reverse)
    lr, li, _ = _s5_entering(sr[n_ctx_rows:], si[n_ctx_rows:], carry, a2_ref, reverse)
    h = jnp.concatenate([jnp.concatenate([cr, lr], axis=0), jnp.concatenate([ci, li], axis=0)], axis=1)
    y = y + jnp.dot(h.astype(BF16), cw_ref[...], preferred_element_type=F32)
    for r in range(S5_Q):
        y_ref[pl.ds(r, n, stride=S5_Q), :] = y[:, r * LANE:(r + 1) * LANE]


def s5_scan(u, weights, n_ctx, reverse):
    bsz, t, ch = u.shape
    ktoep, bw, cw, a2 = weights
    qk = S5_Q * LANE
    return pl.pallas_call(
        functools.partial(_s5_kernel, n_ctx_rows=n_ctx // S5_Q, reverse=reverse),
        grid=(S5_NT, bsz),
        in_specs=[pl.BlockSpec((None, t, LANE), lambda j, b: (b, 0, j)),
                  pl.BlockSpec((None, qk, qk), lambda j, b: (j, 0, 0)),
                  pl.BlockSpec((None, qk, 2 * S5_TS), lambda j, b: (j, 0, 0)),
                  pl.BlockSpec((None, 2 * S5_TS, qk), lambda j, b: (j, 0, 0)),
                  pl.BlockSpec((None, 16, 2 * S5_TS), lambda j, b: (j, 0, 0))],
        out_specs=pl.BlockSpec((None, t, LANE), lambda j, b: (b, 0, j)),
        out_shape=jax.ShapeDtypeStruct((bsz, t, ch), F32),
        compiler_params=_cparams(("parallel", "parallel")),
    )(u, ktoep, bw, cw, a2)


def _s5_out_kernel(yf_ref, yb_ref, u_ref, d_ref, w_ref, o_ref):
    y = yf_ref[...] + yb_ref[...] + d_ref[...] * u_ref[...]
    g = jax.nn.gelu(y)
    z = jnp.dot(g.astype(BF16), w_ref[...], preferred_element_type=F32)
    o_ref[...] = (g * jax.nn.sigmoid(z)).astype(o_ref.dtype)


def s5_out(yf, yb, u, d_skip, glu_w, tm=512):
    t, ch = u.shape
    row = pl.BlockSpec((tm, ch), lambda i: (i, 0))
    return pl.pallas_call(
        _s5_out_kernel,
        grid=(t // tm,),
        in_specs=[row, row, row, pl.BlockSpec((1, ch), lambda i: (0, 0)), pl.BlockSpec((ch, ch), lambda i: (0, 0))],
        out_specs=row,
        out_shape=jax.ShapeDtypeStruct((t, ch), BF16),
        compiler_params=_cparams(("parallel",)),
    )(yf, yb, u, d_skip.reshape(1, ch), glu_w.astype(BF16))


def _softplus(x):
    return jnp.maximum(x, 0.0) + jnp.log1p(jnp.exp(-jnp.abs(x)))


def _ssd_kernel(xf_ref, dtf_ref, xb_ref, dtb_ref, bias_ref, a_ref, tri_ref, yf_ref, yb_ref, st_ref, *, bsz):
    @pl.when(pl.program_id(0) == 0)
    def _():
        st_ref[...] = jnp.zeros_like(st_ref)

    for d, (x_ref, dt_ref, y_ref) in enumerate(((xf_ref, dtf_ref, yf_ref), (xb_ref, dtb_ref, yb_ref))):
        for b in range(bsz):
            _ssd_step(x_ref.at[b], dt_ref.at[b], bias_ref, a_ref, tri_ref.at[d], y_ref.at[b], st_ref.at[d, b],
                      d=d, reverse=d == 1)


def _ssd_step(xbc_ref, dt_ref, bias_ref, a_ref, tri_ref, y_ref, st_ref, *, d, reverse):
    q = SSD_CHUNK
    hp = SSD_HEAD_DIM
    hg = SSD_HEADS // SSD_GROUPS
    dt = _softplus(dt_ref[...] + bias_ref[...])
    adt = dt * a_ref[...]
    cs = jnp.dot(tri_ref[...], adt, preferred_element_type=F32, precision=HI)
    tot = jnp.sum(adt, axis=0, keepdims=True)
    cs_t = cs.T
    ecs = jnp.exp(cs)
    dec = jnp.exp(tot - cs)
    etot = jnp.exp(tot)
    xs = xbc_ref[:, :MIX_WIDTH]
    bm = xbc_ref[:, MIX_WIDTH:MIX_WIDTH + SSD_GROUPS * SSD_STATE]
    cm = xbc_ref[:, MIX_WIDTH + SSD_GROUPS * SSD_STATE:]
    bm_t = bm.T
    li = lax.broadcasted_iota(jnp.int32, (q, q), 0)
    si = lax.broadcasted_iota(jnp.int32, (q, q), 1)
    mask = (si >= li) if reverse else (si <= li)
    for g in range(SSD_GROUPS):
        cg = cm[:, g * SSD_STATE:(g + 1) * SSD_STATE].astype(BF16)
        bg = bm[:, g * SSD_STATE:(g + 1) * SSD_STATE].astype(BF16)
        gmat = lax.dot_general(cg, bg, (((1,), (1,)), ((), ())), preferred_element_type=F32)
        yoff = jnp.dot(cg, st_ref[g].astype(BF16), preferred_element_type=F32)
        xw = []
        for hl in range(hg):
            h = g * hg + hl
            col = d * SSD_HEADS + h
            seg = cs[:, col:col + 1] - cs_t[col:col + 1, :]
            lmat = jnp.exp(jnp.where(mask, seg, -jnp.inf))
            xdt = xs[:, h * hp:(h + 1) * hp] * dt[:, col:col + 1]
            yd = jnp.dot((gmat * lmat).astype(BF16), xdt.astype(BF16), preferred_element_type=F32)
            y_ref[:, h * hp:(h + 1) * hp] = yd + yoff[:, hl * hp:(hl + 1) * hp] * ecs[:, col:col + 1]
            xw.append(xdt * dec[:, col:col + 1])
        xw = jnp.concatenate(xw, axis=1).astype(BF16)
        new = jnp.dot(bm_t[g * SSD_STATE:(g + 1) * SSD_STATE, :].astype(BF16), xw, preferred_element_type=F32)
        for hl in range(hg):
            col = d * SSD_HEADS + g * hg + hl
            sl = slice(hl * hp, (hl + 1) * hp)
            st_ref[g, :, sl] = st_ref[g, :, sl] * etot[:, col:col + 1] + new[:, sl]


def ssd_scan(xbc, dt_raw, dt_bias_row, a_row, n_ctx):
    bsz, t, _ = xbc.shape
    q = SSD_CHUNK
    orders = _chunk_order(t // q, n_ctx // q)
    blk = lambda w, cidx: pl.BlockSpec((bsz, q, w), lambda i: (0, cidx(i), 0))
    vec = pl.BlockSpec((1, LANE), lambda i: (0, 0))
    return pl.pallas_call(
        functools.partial(_ssd_kernel, bsz=bsz),
        grid=(t // q,),
        in_specs=[blk(w, c) for c in orders for w in (SSD_CONV_CH, LANE)]
        + [vec, vec, pl.BlockSpec((2, q, q), lambda i: (0, 0, 0))],
        out_specs=[blk(MIX_WIDTH, c) for c in orders],
        out_shape=[jax.ShapeDtypeStruct((bsz, t, MIX_WIDTH), F32)] * 2,
        scratch_shapes=[pltpu.VMEM((2, bsz, SSD_GROUPS, SSD_STATE, (SSD_HEADS // SSD_GROUPS) * SSD_HEAD_DIM), F32)],
        compiler_params=_cparams(("arbitrary",)),
        name="ssd_scan",
    )(xbc, dt_raw, xbc, dt_raw, dt_bias_row, a_row, _cumsum_mats(q))


def _ssd_out_kernel(yf_ref, yb_ref, xbc_ref, z_ref, d_ref, nw_ref, o_ref):
    z = z_ref[...]
    y = (d_ref[...] * xbc_ref[...] + yf_ref[...] + yb_ref[...]) * (z * jax.nn.sigmoid(z))
    ms = jnp.mean(y * y, axis=-1, keepdims=True)
    o_ref[...] = (y * lax.rsqrt(ms + NORM_EPS) * nw_ref[...]).astype(o_ref.dtype)


def ssd_out(yf, yb, xbc, z, d_row, norm_w, tm=512):
    t, ch = z.shape
    row = pl.BlockSpec((tm, ch), lambda i: (i, 0))
    vec = pl.BlockSpec((1, ch), lambda i: (0, 0))
    return pl.pallas_call(
        _ssd_out_kernel,
        grid=(t // tm,),
        in_specs=[row, row, pl.BlockSpec((tm, ch), lambda i: (i, 0)), row, vec, vec],
        out_specs=row,
        out_shape=jax.ShapeDtypeStruct((t, ch), BF16),
        compiler_params=_cparams(("parallel",)),
    )(yf, yb, xbc, z, d_row, norm_w.reshape(1, ch))


def _gla_kernel(*refs, bsz):
    ins, (wg_ref, gb_ref, tri_ref), outs, st_ref = refs[:8], refs[8:11], refs[11:13], refs[13]

    @pl.when(pl.program_id(0) == 0)
    def _():
        st_ref[...] = jnp.zeros_like(st_ref)

    for d in range(2):
        q_ref, k_ref, v_ref, r_ref = ins[4 * d:4 * d + 4]
        for b in range(bsz):
            _gla_step(q_ref.at[b], k_ref.at[b], v_ref.at[b], r_ref.at[b], wg_ref.at[d], gb_ref.at[d],
                      tri_ref.at[d], outs[d].at[b], st_ref.at[d, b], reverse=d == 1)


def _gla_step(q_ref, k_ref, v_ref, r_ref, wg_ref, gb_ref, tri_ref, o_ref, st_ref, *, reverse):
    qs = GLA_CHUNK
    zl = jnp.dot(r_ref[...], wg_ref[...], preferred_element_type=F32, precision=HI) + gb_ref[...]
    la = (jnp.minimum(zl, 0.0) - jnp.log1p(jnp.exp(-jnp.abs(zl)))) / GLA_TAU
    bc = jnp.dot(tri_ref[...], la, preferred_element_type=F32, precision=HI)
    tot = jnp.sum(la, axis=0, keepdims=True)
    qd = (q_ref[...] * (GLA_DK ** -0.5) * jnp.exp(bc)).astype(BF16)
    ki = (k_ref[...] * jnp.exp(-bc)).astype(BF16)
    ke = (k_ref[...] * jnp.exp(tot - bc)).astype(BF16)
    gam = jnp.exp(tot)
    ti = lax.broadcasted_iota(jnp.int32, (qs, qs), 0)
    si = lax.broadcasted_iota(jnp.int32, (qs, qs), 1)
    mask = (si >= ti) if reverse else (si <= ti)
    for h in range(GLA_HEADS):
        ks = slice(h * GLA_DK, (h + 1) * GLA_DK)
        vs = slice(h * GLA_DV, (h + 1) * GLA_DV)
        vh = v_ref[:, vs].astype(BF16)
        sc = lax.dot_general(qd[:, ks], ki[:, ks], (((1,), (1,)), ((), ())), preferred_element_type=F32)
        sc = jnp.where(mask, sc, 0.0).astype(BF16)
        o_intra = jnp.dot(sc, vh, preferred_element_type=F32)
        st = st_ref[:, ks]
        o_inter = lax.dot_general(qd[:, ks], st.astype(BF16), (((1,), (1,)), ((), ())), preferred_element_type=F32)
        o_ref[:, vs] = o_intra + o_inter
        ds = lax.dot_general(vh, ke[:, ks], (((0,), (0,)), ((), ())), preferred_element_type=F32)
        st_ref[:, ks] = st * gam[:, ks] + ds


def _chunk_order(nchunk, nctx):
    fwd = lambda i: i
    bwd = lambda i: jnp.where(i < nctx, nctx - 1 - i, nchunk + nctx - 1 - i)
    return fwd, bwd


def _cumsum_mats(q):
    tri = np.tril(np.ones((q, q), np.float32))
    return jnp.asarray(np.stack([tri, tri.T]))


def gla_scan(q, k, v, r, wg, gb, n_ctx):
    bsz, t, _ = q.shape
    qs = GLA_CHUNK
    orders = _chunk_order(t // qs, n_ctx // qs)
    blk = lambda w, cidx: pl.BlockSpec((bsz, qs, w), lambda i: (0, cidx(i), 0))
    widths = (GLA_KEY_WIDTH, GLA_KEY_WIDTH, MIX_WIDTH, LANE)
    return pl.pallas_call(
        functools.partial(_gla_kernel, bsz=bsz),
        grid=(t // qs,),
        in_specs=[blk(w, c) for c in orders for w in widths]
        + [pl.BlockSpec((2, LANE, GLA_KEY_WIDTH), lambda i: (0, 0, 0)),
           pl.BlockSpec((2, 1, GLA_KEY_WIDTH), lambda i: (0, 0, 0)),
           pl.BlockSpec((2, qs, qs), lambda i: (0, 0, 0))],
        out_specs=[blk(MIX_WIDTH, c) for c in orders],
        out_shape=[jax.ShapeDtypeStruct((bsz, t, MIX_WIDTH), F32)] * 2,
        scratch_shapes=[pltpu.VMEM((2, bsz, GLA_DV, GLA_KEY_WIDTH), F32)],
        compiler_params=_cparams(("arbitrary",)),
        name="gla_scan",
    )(q, k, v, r, q, k, v, r, wg, gb, _cumsum_mats(qs))


def _gla_out_kernel(of_ref, ob_ref, g_ref, nw_ref, o_ref):
    g = g_ref[...]
    sg = g * jax.nn.sigmoid(g)
    for h in range(GLA_HEADS):
        vs = slice(h * GLA_DV, (h + 1) * GLA_DV)
        o = of_ref[:, vs] + ob_ref[:, vs]
        ms = jnp.mean(o * o, axis=-1, keepdims=True)
        o_ref[:, vs] = (o * lax.rsqrt(ms + NORM_EPS) * nw_ref[...] * sg[:, vs]).astype(o_ref.dtype)


def gla_out(of, ob, g, norm_w, tm=512):
    t, ch = g.shape
    row = pl.BlockSpec((tm, ch), lambda i: (i, 0))
    return pl.pallas_call(
        _gla_out_kernel,
        grid=(t // tm,),
        in_specs=[row, row, row, pl.BlockSpec((1, GLA_DV), lambda i: (0, 0))],
        out_specs=row,
        out_shape=jax.ShapeDtypeStruct((t, ch), BF16),
        compiler_params=_cparams(("parallel",)),
    )(of, ob, g, norm_w.reshape(1, GLA_DV))


def _hy_emb(n):
    t = np.linspace(0.0, 1.0, n)[:, None]
    freqs = np.linspace(1e-4, HY_BANDS - 1, HY_BANDS)
    ang = (2.0 * math.pi / n) * np.arange(n)[:, None] * freqs[None, :]
    emb = np.concatenate([t, np.cos(ang), -np.sin(ang)], axis=-1)
    return np.pad(emb, ((0, 0), (0, LANE - HY_EMB))).astype(np.float32)


def _hy_filter_kernel(emb_ref, w1_ref, b1_ref, f1_ref, w2_ref, b2_ref, f2_ref, w3_ref, dl_ref, o_ref):
    emb = emb_ref[...]
    h = jnp.sin(f1_ref[...] * (jnp.dot(emb, w1_ref[...], preferred_element_type=F32, precision=HI) + b1_ref[...]))
    h = jnp.sin(f2_ref[...] * (jnp.dot(h, w2_ref[...], preferred_element_type=F32, precision=HI) + b2_ref[...]))
    h = jnp.dot(h, w3_ref[...], preferred_element_type=F32, precision=HI)
    o_ref[...] = h * jnp.exp(-emb[:, 0:1] * dl_ref[...])


def hyena_filters(n, w1, b1, f1, w2, b2, f2, w3, tm=256):
    nf = w3.shape[1]
    emb = jnp.asarray(_hy_emb(n))
    deltas = np.abs(np.linspace(HY_MIN_DECAY, HY_MAX_DECAY, MIX_WIDTH)).astype(np.float32)
    dl = jnp.asarray(np.tile(deltas, nf // MIX_WIDTH)[None, :])
    w1p = jnp.pad(w1, ((0, LANE - HY_EMB), (0, 0)))
    fd = HY_FILTER_DIM
    vec = lambda w: pl.BlockSpec((1, w), lambda i: (0, 0))
    return pl.pallas_call(
        _hy_filter_kernel,
        grid=(n // tm,),
        in_specs=[pl.BlockSpec((tm, LANE), lambda i: (i, 0)),
                  pl.BlockSpec((LANE, fd), lambda i: (0, 0)), vec(fd), vec(fd),
                  pl.BlockSpec((fd, fd), lambda i: (0, 0)), vec(fd), vec(fd),
                  pl.BlockSpec((fd, nf), lambda i: (0, 0)), vec(nf)],
        out_specs=pl.BlockSpec((tm, nf), lambda i: (i, 0)),
        out_shape=jax.ShapeDtypeStruct((n, nf), F32),
        compiler_params=_cparams(("parallel",)),
    )(emb, w1p, b1.reshape(1, fd), f1.reshape(1, fd), w2, b2.reshape(1, fd), f2.reshape(1, fd), w3, dl)


class _HyDft:
    def __init__(self, n):
        big = 2 * n
        self.n1 = big // FFT_N2
        self.rows = n // FFT_N2
        self.kh = self.n1 // 2 + 1
        self.khp = -(-self.kh // 8) * 8
        k1 = np.arange(self.kh)
        ang = 2.0 * np.pi * np.outer(k1, np.arange(self.rows)) / self.n1
        fwd = np.zeros((2 * self.khp, self.rows))
        fwd[:self.kh] = np.cos(ang)
        fwd[self.khp:self.khp + self.kh] = -np.sin(ang)
        wgt = np.full(self.kh, 2.0)
        wgt[0] = wgt[-1] = 1.0
        inv = np.zeros((self.rows, 2 * self.khp))
        inv[:, :self.kh] = (wgt[:, None] * np.cos(ang)).T / big
        inv[:, self.khp:self.khp + self.kh] = -(wgt[:, None] * np.sin(ang)).T / big
        tw = 2.0 * np.pi * np.outer(k1, np.arange(FFT_N2)) / big
        f32 = lambda a: np.ascontiguousarray(a, dtype=np.float32)
        self.fwd, self.inv = f32(fwd), f32(inv)
        self.tw_re, self.tw_im = f32(np.cos(tw)[:, :, None]), f32(-np.sin(tw)[:, :, None])
        self.small = self.rows < 8


def _inner_dft_mats():
    k2 = np.arange(FFT_N2)
    f2 = np.exp(-2j * np.pi * np.outer(k2, k2) / FFT_N2)
    w2 = np.block([[f2.real, -f2.imag], [f2.imag, f2.real]])
    w2c = np.block([[f2.real, f2.imag], [-f2.imag, f2.real]])
    return w2.astype(np.float32), w2c.astype(np.float32)


_W2, _W2C = _inner_dft_mats()
HY_UNROLL = 4


def _sc_load(ref, idx):
    return jnp.concatenate([ref[w, idx, :] for w in range(ref.shape[0])], axis=1)


def _sc_store(ref, idx, val):
    for w in range(ref.shape[0]):
        ref[w, idx, :] = val[:, w * LANE:(w + 1) * LANE]


def _hy_outer_fwd(load_rows, cst, fwd_ref, xr_ref, xi_ref):
    if cst.small:
        blocks = [load_rows(n1 * FFT_N2, FFT_N2, None) for n1 in range(cst.rows)]
        for k1 in range(cst.kh):
            xr = sum(float(cst.fwd[k1, n1]) * blocks[n1] for n1 in range(cst.rows))
            xi = sum(float(cst.fwd[cst.khp + k1, n1]) * blocks[n1] for n1 in range(cst.rows))
            _sc_store(xr_ref, pl.ds(k1 * FFT_N2, FFT_N2), xr)
            _sc_store(xi_ref, pl.ds(k1 * FFT_N2, FFT_N2), xi)
        return

    def body(n2, c):
        res = jnp.dot(fwd_ref[...], load_rows(n2, cst.rows, FFT_N2), preferred_element_type=F32, precision=HI)
        _sc_store(xr_ref, pl.ds(n2, cst.khp, stride=FFT_N2), res[:cst.khp])
        _sc_store(xi_ref, pl.ds(n2, cst.khp, stride=FFT_N2), res[cst.khp:])
        return c
    lax.fori_loop(0, FFT_N2, body, 0, unroll=HY_UNROLL)


def _hy_inner_fwd(k1, xr_ref, xi_ref, twr_ref, twi_ref, w2_ref):
    start = k1 * FFT_N2
    rows = pl.ds(start if isinstance(k1, int) else pl.multiple_of(start, FFT_N2), FFT_N2)
    xr, xi = _sc_load(xr_ref, rows), _sc_load(xi_ref, rows)
    twr, twi = twr_ref[k1], twi_ref[k1]
    ar = xr * twr - xi * twi
    ai = xr * twi + xi * twr
    ys = jnp.dot(w2_ref[...], jnp.concatenate([ar, ai], axis=0).astype(BF16), preferred_element_type=F32)
    return rows, twr, twi, ys[:FFT_N2], ys[FFT_N2:]


def _hy_for_k1(cst, body):
    if cst.small:
        for k1 in range(cst.kh):
            body(k1, 0)
    else:
        lax.fori_loop(0, cst.kh, body, 0)


def _hy_spectrum_part(cst, hf_ref, hb_ref, fwd_ref, twr_ref, twi_ref, w2_ref, hr_ref, hi_ref, xr_ref, xi_ref):
    width = hf_ref.shape[1]

    def load_rows(start, count, stride):
        idx = pl.ds(start, count) if stride is None else pl.ds(start, count, stride=stride)
        return jnp.concatenate([hf_ref[idx, :], hb_ref[idx, :]], axis=1)
    _hy_outer_fwd(load_rows, cst, fwd_ref, xr_ref, xi_ref)
    hb0 = hb_ref[0:1, :]

    def body(k1, c):
        rows, _, _, yr, yi = _hy_inner_fwd(k1, xr_ref, xi_ref, twr_ref, twi_ref, w2_ref)
        hr_ref[rows, :] = yr[:, :width] + yr[:, width:] - hb0
        hi_ref[rows, :] = yi[:, :width] - yi[:, width:]
        return c
    _hy_for_k1(cst, body)


def _hy_spectrum_kernel(*refs, parts):
    w2_ref = refs[0]
    pos = 1
    n_in = 5 * len(parts)
    outs = refs[1 + n_in:1 + n_in + 2 * len(parts)]
    xr_ref, xi_ref = refs[-2:]
    for p, cst in enumerate(parts):
        hf_ref, hb_ref, fwd_ref, twr_ref, twi_ref = refs[pos:pos + 5]
        pos += 5
        _hy_spectrum_part(cst, hf_ref, hb_ref, fwd_ref, twr_ref, twi_ref, w2_ref, outs[2 * p], outs[2 * p + 1],
                          xr_ref, xi_ref)


def hyena_spectrum(filts):
    parts = tuple(_HyDft(f.shape[0]) for f in filts)
    nt = MIX_WIDTH // LANE
    width = HY_ORDER * MIX_WIDTH
    args = [jnp.asarray(_W2).astype(BF16)]
    in_specs = [pl.BlockSpec((2 * FFT_N2, 2 * FFT_N2), lambda t: (0, 0))]
    out_specs, out_shape = [], []
    for f, cst in zip(filts, parts):
        n = f.shape[0]
        args += [f, f, jnp.asarray(cst.fwd), jnp.asarray(cst.tw_re), jnp.asarray(cst.tw_im)]
        in_specs += [pl.BlockSpec((n, LANE), lambda t: (0, (t // nt) * 2 * nt + t % nt)),
                     pl.BlockSpec((n, LANE), lambda t: (0, (t // nt) * 2 * nt + nt + t % nt)),
                     pl.BlockSpec(cst.fwd.shape, lambda t: (0, 0)),
                     pl.BlockSpec(cst.tw_re.shape, lambda t: (0, 0, 0)),
                     pl.BlockSpec(cst.tw_im.shape, lambda t: (0, 0, 0))]
        out_specs += [pl.BlockSpec((cst.kh * FFT_N2, LANE), lambda t: (0, t))] * 2
        out_shape += [jax.ShapeDtypeStruct((cst.kh * FFT_N2, width), F32)] * 2
    rows = max(c.khp for c in parts) * FFT_N2
    outs = pl.pallas_call(
        functools.partial(_hy_spectrum_kernel, parts=parts),
        grid=(width // LANE,),
        in_specs=in_specs, out_specs=out_specs, out_shape=out_shape,
        scratch_shapes=[pltpu.VMEM((2, rows, LANE), F32)] * 2,
        compiler_params=_cparams(("parallel",)),
        name="hyena_spectrum",
    )(*args)
    return [(outs[2 * p], outs[2 * p + 1]) for p in range(len(parts))]


def _hy_conv_part(cst, row0, u_ref, g_ref, bias_ref, hr_ref, hi_ref, fwd_ref, inv_ref, twr_ref, twi_ref,
                  w2_ref, w2c_ref, o_ref, xr_ref, xi_ref):
    bsz = u_ref.shape[0]

    def load_rows(start, count, stride):
        idx = pl.ds(row0 + start, count) if stride is None else pl.ds(row0 + start, count, stride=stride)
        return jnp.concatenate([u_ref[b, idx, :] for b in range(bsz)], axis=1)
    _hy_outer_fwd(load_rows, cst, fwd_ref, xr_ref, xi_ref)

    def body(k1, c):
        rows, twr, twi, yr, yi = _hy_inner_fwd(k1, xr_ref, xi_ref, twr_ref, twi_ref, w2_ref)
        hr = jnp.concatenate([hr_ref[rows, :]] * bsz, axis=1)
        hi = jnp.concatenate([hi_ref[rows, :]] * bsz, axis=1)
        zr = yr * hr - yi * hi
        zi = yr * hi + yi * hr
        vs = jnp.dot(w2c_ref[...], jnp.concatenate([zr, zi], axis=0).astype(BF16), preferred_element_type=F32)
        vr, vi = vs[:FFT_N2], vs[FFT_N2:]
        _sc_store(xr_ref, rows, vr * twr + vi * twi)
        _sc_store(xi_ref, rows, vi * twr - vr * twi)
        return c
    _hy_for_k1(cst, body)

    bias = bias_ref[...]

    def emit(idx, y):
        for b in range(bsz):
            yb = y[:, b * LANE:(b + 1) * LANE]
            o_ref[b, idx, :] = (g_ref[b, idx, :] * (yb + bias * u_ref[b, idx, :])).astype(o_ref.dtype)

    if cst.small:
        for n1 in range(cst.rows):
            y = sum(float(cst.inv[n1, k1]) * _sc_load(xr_ref, pl.ds(k1 * FFT_N2, FFT_N2))
                    + float(cst.inv[n1, cst.khp + k1]) * _sc_load(xi_ref, pl.ds(k1 * FFT_N2, FFT_N2))
                    for k1 in range(cst.kh))
            emit(pl.ds(row0 + n1 * FFT_N2, FFT_N2), y)
        return

    def out_body(n2, c):
        z = jnp.concatenate([_sc_load(xr_ref, pl.ds(n2, cst.khp, stride=FFT_N2)),
                             _sc_load(xi_ref, pl.ds(n2, cst.khp, stride=FFT_N2))], axis=0)
        y = jnp.dot(inv_ref[...], z, preferred_element_type=F32, precision=HI)
        emit(pl.ds(row0 + n2, cst.rows, stride=FFT_N2), y)
        return c
    lax.fori_loop(0, FFT_N2, out_body, 0, unroll=HY_UNROLL)


def _hy_conv_kernel(*refs, parts, row0s, zero_rows):
    u_ref, g_ref, bias_ref, w2_ref, w2c_ref = refs[:5]
    o_ref, xr_ref, xi_ref = refs[-3:]
    pos = 5
    for cst, row0 in zip(parts, row0s):
        hr_ref, hi_ref, fwd_ref, inv_ref, twr_ref, twi_ref = refs[pos:pos + 6]
        pos += 6
        _hy_conv_part(cst, row0, u_ref, g_ref, bias_ref, hr_ref, hi_ref, fwd_ref, inv_ref, twr_ref, twi_ref,
                      w2_ref, w2c_ref, o_ref, xr_ref, xi_ref)
    if zero_rows is not None:
        lo, hi = zero_rows
        o_ref[:, lo:hi, :] = jnp.zeros((o_ref.shape[0], hi - lo, o_ref.shape[2]), o_ref.dtype)


def hyena_conv(u, u_blk, gate, g_blk, bias, order, spectra, segs, zero_rows):
    out_dtype = F32
    bsz, t, _ = u.shape
    nt = MIX_WIDTH // LANE
    parts = tuple(_HyDft(n) for _, n in segs)
    args = [u, gate, bias, jnp.asarray(_W2).astype(BF16), jnp.asarray(_W2C).astype(BF16)]
    mat = pl.BlockSpec((2 * FFT_N2, 2 * FFT_N2), lambda j: (0, 0))
    in_specs = [pl.BlockSpec((bsz, t, LANE), lambda j: (0, 0, u_blk + j)),
                pl.BlockSpec((bsz, t, LANE), lambda j: (0, 0, g_blk + j)),
                pl.BlockSpec((1, LANE), lambda j: (0, j)), mat, mat]
    for (hr, hi), cst in zip(spectra, parts):
        args += [hr, hi, jnp.asarray(cst.fwd), jnp.asarray(cst.inv), jnp.asarray(cst.tw_re), jnp.asarray(cst.tw_im)]
        hspec = pl.BlockSpec((cst.kh * FFT_N2, LANE), lambda j: (0, order * nt + j))
        in_specs += [hspec, hspec, pl.BlockSpec(cst.fwd.shape, lambda j: (0, 0)),
                     pl.BlockSpec(cst.inv.shape, lambda j: (0, 0)),
                     pl.BlockSpec(cst.tw_re.shape, lambda j: (0, 0, 0)),
                     pl.BlockSpec(cst.tw_im.shape, lambda j: (0, 0, 0))]
    rows = max(c.khp for c in parts) * FFT_N2
    return pl.pallas_call(
        functools.partial(_hy_conv_kernel, parts=parts, row0s=tuple(r for r, _ in segs), zero_rows=zero_rows),
        grid=(nt,),
        in_specs=in_specs,
        out_specs=pl.BlockSpec((bsz, t, LANE), lambda j: (0, 0, j)),
        out_shape=jax.ShapeDtypeStruct((bsz, t, MIX_WIDTH), out_dtype),
        scratch_shapes=[pltpu.VMEM((bsz, rows, LANE), F32)] * 2,
        compiler_params=_cparams(("parallel",)),
        name="hyena_conv",
    )(*args)


def _merge_kernel(ya_ref, yb_ref, yc_ref, yd_ref, ga_ref, gb_ref, gc_ref, gd_ref, w_ref, o_ref):
    acc = None
    for b, (y_ref, g_ref) in enumerate(((ya_ref, ga_ref), (yb_ref, gb_ref), (yc_ref, gc_ref), (yd_ref, gd_ref))):
        t = jax.nn.sigmoid(g_ref[...].astype(F32)) * jnp.dot(y_ref[...].astype(BF16), w_ref[b], preferred_element_type=F32)
        acc = t if acc is None else acc + t
    o_ref[...] = acc.astype(o_ref.dtype)


def merge_branches(ys, gates, w_branch, tm=512, tn=512):
    t, ch = ys[0].shape
    nj = D_MODEL // tn
    yspec = pl.BlockSpec((tm, ch), lambda j, i: (i, 0))
    gspecs = [pl.BlockSpec((tm, tn), functools.partial(lambda j, i, b: (i, b * nj + j), b=b)) for b in range(N_BRANCH)]
    return pl.pallas_call(
        _merge_kernel,
        grid=(nj, t // tm),
        in_specs=[yspec] * 4 + gspecs + [pl.BlockSpec((N_BRANCH, ch, tn), lambda j, i: (0, 0, j))],
        out_specs=pl.BlockSpec((tm, tn), lambda j, i: (i, j)),
        out_shape=jax.ShapeDtypeStruct((t, D_MODEL), BF16),
        compiler_params=_cparams(("parallel", "parallel")),
    )(*ys, gates, gates, gates, gates, w_branch)


def _mm_resid_kernel(a_ref, w_ref, x_ref, g_ref, o_ref):
    o_ref[...] = x_ref[...] + g_ref[...] * jnp.dot(a_ref[...], w_ref[...], preferred_element_type=F32)


def matmul_gated_residual(a, w, x, gate, tiles_per_batch, tn=512):
    t, k = a.shape
    n = w.shape[1]
    return pl.pallas_call(
        _mm_resid_kernel,
        grid=(n // tn, t // ROW_TILE),
        in_specs=[pl.BlockSpec((ROW_TILE, k), lambda j, i: (i, 0)),
                  pl.BlockSpec((k, tn), lambda j, i: (0, j)),
                  pl.BlockSpec((ROW_TILE, tn), lambda j, i: (i, j)),
                  pl.BlockSpec((None, 1, tn), lambda j, i: (_seg_index(i, tiles_per_batch), 0, j))],
        out_specs=pl.BlockSpec((ROW_TILE, tn), lambda j, i: (i, j)),
        out_shape=jax.ShapeDtypeStruct((t, n), F32),
        compiler_params=_cparams(("parallel", "parallel")),
    )(a, w, x, gate)


ROUTE_OFF = MOE_GROUPS
TOK_S = D_MODEL // LANE
DSP_S = TOK_S + 8
MOE_TM = 512


def _store_token_major(ref, val, slab=TOK_S):
    rows = val.shape[0]
    for s in range(TOK_S):
        ref[pl.ds(s, rows, stride=slab), :] = val[:, s * LANE:(s + 1) * LANE]


def _load_token_major(ref, rows, slab=TOK_S):
    return jnp.concatenate([ref[pl.ds(s, rows, stride=slab), :] for s in range(TOK_S)], axis=1)


def _router_kernel(x_ref, sh_ref, sc_ref, wr_ref, br_ref, h_ref, comb_ref):
    x = x_ref[...]
    ms = jnp.mean(x * x, axis=-1, keepdims=True)
    h = x * lax.rsqrt(ms + NORM_EPS) * (1.0 + sc_ref[...]) + sh_ref[...]
    _store_token_major(h_ref, h, DSP_S)
    logits = jnp.dot(h, wr_ref[...], preferred_element_type=F32, precision=HI) + br_ref[...]
    lane = lax.broadcasted_iota(jnp.int32, logits.shape, 1).astype(F32)
    neg = -jnp.inf
    gmask = lane < MOE_GROUPS
    glog = jnp.where(gmask, logits, neg)
    gmax = jnp.max(glog, axis=-1, keepdims=True)
    g_idx = jnp.min(jnp.where(glog == gmax, lane, float(LANE)), axis=-1, keepdims=True)
    p_g = 1.0 / jnp.sum(jnp.exp(glog - gmax), axis=-1, keepdims=True)
    e_lo = ROUTE_OFF + g_idx * MOE_PER_GROUP
    emask = (lane >= e_lo) & (lane < e_lo + MOE_PER_GROUP)
    v1 = jnp.max(jnp.where(emask, logits, neg), axis=-1, keepdims=True)
    i1 = jnp.min(jnp.where(emask & (logits == v1), lane, float(LANE)), axis=-1, keepdims=True)
    emask2 = emask & (lane != i1)
    v2 = jnp.max(jnp.where(emask2, logits, neg), axis=-1, keepdims=True)
    i2 = jnp.min(jnp.where(emask2 & (logits == v2), lane, float(LANE)), axis=-1, keepdims=True)
    e21 = jnp.exp(v2 - v1)
    w1 = p_g / (1.0 + e21)
    w2 = p_g * e21 / (1.0 + e21)
    comb = jnp.where(lane == i1, w1, jnp.where(lane == i2, w2, 0.0))
    route = jnp.where(lane == 0, g_idx, comb)
    comb_ref[...] = route
    rows = x.shape[0]
    h_ref[pl.ds(TOK_S, rows, stride=DSP_S), :] = route
    for s in range(TOK_S + 1, DSP_S):
        h_ref[pl.ds(s, rows, stride=DSP_S), :] = jnp.zeros_like(route)


def moe_router(x, shift, scale, group_w, group_b, expert_w, expert_b, tiles_per_batch):
    t, d = x.shape
    wr = _pad_cols(jnp.concatenate([group_w, expert_w], axis=1), LANE)
    br = _pad_cols(jnp.concatenate([group_b, expert_b])[None, :], LANE)
    seg = lambda i: (_seg_index(i, tiles_per_batch), 0, 0)
    return pl.pallas_call(
        _router_kernel,
        grid=(t // ROW_TILE,),
        in_specs=[pl.BlockSpec((ROW_TILE, d), lambda i: (i, 0)),
                  pl.BlockSpec((None, 1, d), seg), pl.BlockSpec((None, 1, d), seg),
                  pl.BlockSpec((d, LANE), lambda i: (0, 0)), pl.BlockSpec((1, LANE), lambda i: (0, 0))],
        out_specs=[pl.BlockSpec((ROW_TILE * DSP_S, LANE), lambda i: (i, 0)),
                   pl.BlockSpec((ROW_TILE, LANE), lambda i: (i, 0))],
        out_shape=[jax.ShapeDtypeStruct((t * DSP_S, LANE), F32), jax.ShapeDtypeStruct((t, LANE), F32)],
        compiler_params=_cparams(("parallel",)),
    )(x, shift, scale, wr, br)


def moe_plan(route, n_tiles):
    g = route[:, 0].astype(jnp.int32)
    oh = (g[:, None] == jnp.arange(MOE_GROUPS, dtype=jnp.int32)[None, :]).astype(jnp.int32)
    cnt = jnp.sum(oh, axis=0)
    rank = jnp.sum((jnp.cumsum(oh, axis=0) - oh) * oh, axis=1)
    ptiles = (cnt + MOE_TM - 1) // MOE_TM
    ends = jnp.cumsum(ptiles)
    off = (ends - ptiles) * MOE_TM
    pos = jnp.sum(oh * off[None, :], axis=1) + rank
    n_act = ends[-1]
    tid = jnp.arange(n_tiles, dtype=jnp.int32)
    tsrc = jnp.minimum(tid, n_act - 1)
    tgrp = jnp.sum((tsrc[:, None] >= ends[None, :]).astype(jnp.int32), axis=1)
    tact = (tid < n_act).astype(jnp.int32)
    return pos, tsrc, tgrp, tact


def _slab(ref, row, size):
    return ref.at[pl.ds(pl.multiple_of(row * size, size), size)]


def _moe_scatter_kernel(pos_ref, src_ref, init_ref, dst_ref, sem):
    del init_ref
    base = pl.program_id(0) * ROW_TILE

    def issue(r, c):
        pltpu.make_async_copy(_slab(src_ref, r, DSP_S), _slab(dst_ref, pos_ref[base + r], DSP_S), sem).start()
        return c
    lax.fori_loop(0, ROW_TILE, issue, 0)
    pltpu.make_async_copy(src_ref, dst_ref.at[pl.ds(0, ROW_TILE * DSP_S)], sem).wait()


def moe_scatter(slabs, pos, n_slots):
    t = pos.shape[0]
    init = jnp.zeros((n_slots * DSP_S, LANE), F32)
    return pl.pallas_call(
        _moe_scatter_kernel,
        grid_spec=pltpu.PrefetchScalarGridSpec(
            num_scalar_prefetch=1, grid=(t // ROW_TILE,),
            in_specs=[pl.BlockSpec((ROW_TILE * DSP_S, LANE), lambda i, p: (i, 0)),
                      pl.BlockSpec(memory_space=pl.ANY)],
            out_specs=pl.BlockSpec(memory_space=pl.ANY),
            scratch_shapes=[pltpu.SemaphoreType.DMA(())]),
        out_shape=jax.ShapeDtypeStruct(init.shape, F32),
        input_output_aliases={2: 0},
        compiler_params=_cparams(("arbitrary",)),
    )(pos, slabs, init)


def _moe_expert_kernel(tsrc_ref, tgrp_ref, tact_ref, xs_ref, wg_ref, wu_ref, wd_ref, ys_ref, x_ref, acc_ref):
    i, e, f = pl.program_id(0), pl.program_id(1), pl.program_id(2)
    active = tact_ref[i] == 1
    first = (e == 0) & (f == 0)
    last = (e == pl.num_programs(1) - 1) & (f == pl.num_programs(2) - 1)

    @pl.when(active & first)
    def _():
        x_ref[...] = _load_token_major(xs_ref, MOE_TM, DSP_S).astype(BF16)
        acc_ref[...] = jnp.zeros_like(acc_ref)

    @pl.when(active)
    def _():
        x = x_ref[...]
        a = jnp.dot(x, wg_ref[...].astype(BF16), preferred_element_type=F32)
        u = jnp.dot(x, wu_ref[...].astype(BF16), preferred_element_type=F32)
        act = (a * jax.nn.sigmoid(a) * u).astype(BF16)
        route = xs_ref[pl.ds(TOK_S, MOE_TM, stride=DSP_S), :]
        lane = lax.broadcasted_iota(jnp.int32, route.shape, 1)
        mine = lane == ROUTE_OFF + tgrp_ref[i] * MOE_PER_GROUP + e
        w = jnp.sum(jnp.where(mine, route, 0.0), axis=-1, keepdims=True)
        acc_ref[...] += w * jnp.dot(act, wd_ref[...].astype(BF16), preferred_element_type=F32)

    @pl.when(active & last)
    def _():
        _store_token_major(ys_ref, acc_ref[...])

    @pl.when(jnp.logical_not(active) & first)
    def _():
        ys_ref[...] = jnp.zeros_like(ys_ref)


def moe_experts(xs, tsrc, tgrp, tact, w_gate, w_up, w_down, li, tf=256):
    n_tiles = tsrc.shape[0]
    _, _, d, ff = w_gate.shape
    wsel = lambda i, e, f, ts, tg, ta: (li, tg[i] * MOE_PER_GROUP + e, 0, f)
    rows = lambda i, e, f, ts, tg, ta: (ts[i], 0)
    return pl.pallas_call(
        _moe_expert_kernel,
        grid_spec=pltpu.PrefetchScalarGridSpec(
            num_scalar_prefetch=3, grid=(n_tiles, MOE_PER_GROUP, ff // tf),
            in_specs=[pl.BlockSpec((MOE_TM * DSP_S, LANE), rows),
                      pl.BlockSpec((None, None, d, tf), wsel),
                      pl.BlockSpec((None, None, d, tf), wsel),
                      pl.BlockSpec((None, None, tf, d),
                                   lambda i, e, f, ts, tg, ta: (li, tg[i] * MOE_PER_GROUP + e, f, 0))],
            out_specs=pl.BlockSpec((MOE_TM * TOK_S, LANE), lambda i, e, f, ts, tg, ta: (i, 0)),
            scratch_shapes=[pltpu.VMEM((MOE_TM, d), BF16), pltpu.VMEM((MOE_TM, d), F32)]),
        out_shape=jax.ShapeDtypeStruct((n_tiles * MOE_TM * TOK_S, LANE), F32),
        compiler_params=_cparams(("arbitrary", "arbitrary", "arbitrary")),
    )(tsrc, tgrp, tact, xs, w_gate, w_up, w_down)


def _moe_gather_kernel(pos_ref, ys_ref, x_ref, g_ref, o_ref, buf_ref, sem):
    base = pl.program_id(0) * ROW_TILE

    def issue(r, c):
        pltpu.make_async_copy(_slab(ys_ref, pos_ref[base + r], TOK_S), _slab(buf_ref, r, TOK_S), sem).start()
        return c
    lax.fori_loop(0, ROW_TILE, issue, 0)
    pltpu.make_async_copy(ys_ref.at[pl.ds(0, ROW_TILE * TOK_S)], buf_ref, sem).wait()
    o_ref[...] = x_ref[...] + g_ref[...] * _load_token_major(buf_ref, ROW_TILE)


def moe_gather_residual(ys, pos, x, gate, tiles_per_batch):
    t, d = x.shape
    return pl.pallas_call(
        _moe_gather_kernel,
        grid_spec=pltpu.PrefetchScalarGridSpec(
            num_scalar_prefetch=1, grid=(t // ROW_TILE,),
            in_specs=[pl.BlockSpec(memory_space=pl.ANY),
                      pl.BlockSpec((ROW_TILE, d), lambda i, p: (i, 0)),
                      pl.BlockSpec((None, 1, d), lambda i, p: (_seg_index(i, tiles_per_batch), 0, 0))],
            out_specs=pl.BlockSpec((ROW_TILE, d), lambda i, p: (i, 0)),
            scratch_shapes=[pltpu.VMEM((ROW_TILE * TOK_S, LANE), F32), pltpu.SemaphoreType.DMA(())]),
        out_shape=jax.ShapeDtypeStruct((t, d), F32),
        compiler_params=_cparams(("arbitrary",)),
    )(pos, ys, x, gate)


def _final_norm_kernel(x_ref, w_ref, o_ref):
    x = x_ref[...]
    ms = jnp.mean(x * x, axis=-1, keepdims=True)
    o_ref[...] = x * lax.rsqrt(ms + NORM_EPS) * w_ref[...]


def final_norm(x, w, n_ctx):
    bsz, t, d = x.shape
    off = n_ctx // ROW_TILE
    return pl.pallas_call(
        _final_norm_kernel,
        grid=(bsz, (t - n_ctx) // ROW_TILE),
        in_specs=[pl.BlockSpec((None, ROW_TILE, d), lambda b, i: (b, i + off, 0)),
                  pl.BlockSpec((1, d), lambda b, i: (0, 0))],
        out_specs=pl.BlockSpec((None, ROW_TILE, d), lambda b, i: (b, i, 0)),
        out_shape=jax.ShapeDtypeStruct((bsz, t - n_ctx, d), F32),
        compiler_params=_cparams(("parallel", "parallel")),
    )(x, w.reshape(1, d))


def _lat_colmajor(t, n_ctx, inverse=False):
    bsz, _, ch = t.shape
    lat = t[:, n_ctx:]
    rows = lat.shape[1] // GRID_W
    shp = (bsz, GRID_W, rows, ch) if inverse else (bsz, rows, GRID_W, ch)
    lat = lat.reshape(shp).transpose(0, 2, 1, 3).reshape(bsz, -1, ch)
    return jnp.concatenate([t[:, :n_ctx], lat], axis=1)


def _seg_table(ctx_vec, lat_mat):
    bsz = lat_mat.shape[0]
    tab = jnp.stack([jnp.broadcast_to(ctx_vec[None, :], lat_mat.shape), lat_mat], axis=1)
    return tab.reshape(2 * bsz, 1, -1)


def kernel(x, c, ctx, c_ctx, ada_w, ada_b, w_in, s5_lambda_re, s5_lambda_im, s5_log_step, s5_b_re, s5_b_im, s5_c_re, s5_c_im, s5_d, s5_glu_w, ssd_conv_w, ssd_conv_b, ssd_a_log, ssd_dt_bias, ssd_d, ssd_norm_w, gla_gate_w, gla_gate_b, gla_norm_w, hy_conv_w, hy_conv_b, hy_w1, hy_b1, hy_freq1, hy_w2, hy_b2, hy_freq2, hy_w3, hy_bias, w_branch, w_out, moe_group_w, moe_group_b, moe_expert_w, moe_expert_b, moe_w_gate, moe_w_up, moe_w_down, final_norm_w):
    bsz, n_lat, d = x.shape
    n_ctx = ctx.shape[1]
    depth = ada_w.shape[0]
    t_b = n_ctx + n_lat
    t_all = bsz * t_b
    tiles_pb = t_b // ROW_TILE
    assert n_ctx == ROW_TILE and n_lat % ROW_TILE == 0 and n_lat == FFT_N // 2 and bsz + 1 <= 8

    xa = jnp.concatenate([ctx, x], axis=1).reshape(t_all, d)
    cc = jnp.pad(jnp.concatenate([c, c_ctx[None, :]], axis=0), ((0, 8 - bsz - 1), (0, 0)))
    mod = adaln(cc, ada_w, ada_b)

    for li in range(depth):
        last = li == depth - 1
        m6 = mod[li].reshape(8, 6, d)
        tabs = [_seg_table(m6[bsz, k], m6[:bsz, k]) for k in range(6)]
        h = modnorm(xa, tabs[0], tabs[1], tiles_pb)

        wl = w_in[li].astype(BF16)
        seg = lambda k0, k1: wl[:, IN_OFFS[k0]:IN_OFFS[k1]]
        (u_s5,) = proj_split(h, seg(0, 1), (MIX_WIDTH,))
        w_ssd = jnp.concatenate([seg(1, 3), _pad_cols(seg(3, 4), LANE)], axis=1)
        z_ssd, xbc_raw, dt_raw = proj_split(h, w_ssd, (MIX_WIDTH, SSD_CONV_CH, LANE))
        w_gla = jnp.concatenate([seg(4, 8), _pad_cols(seg(8, 9), LANE)], axis=1)
        q_g, k_g, v_g, g_g, r_g = proj_split(h, w_gla, (GLA_KEY_WIDTH, GLA_KEY_WIDTH, MIX_WIDTH, MIX_WIDTH, LANE))
        (p_hy,) = proj_split(h, seg(9, 10), (3 * MIX_WIDTH,))
        gates = matmul(h, seg(10, 11), 512, 1024, out_dtype=BF16)

        b3 = lambda a: a.reshape(bsz, t_b, a.shape[-1])

        ys = []
        for dr in range(2):
            wts = s5_weights(s5_lambda_re[li, dr], s5_lambda_im[li, dr], s5_log_step[li, dr], s5_b_re[li, dr],
                             s5_b_im[li, dr], s5_c_re[li, dr], s5_c_im[li, dr], dr == 1)
            ys.append(s5_scan(b3(u_s5), wts, n_ctx, dr == 1).reshape(t_all, MIX_WIDTH))
        ya = s5_out(ys[0], ys[1], u_s5, s5_d[li], s5_glu_w[li])

        z_cm = _lat_colmajor(b3(z_ssd), n_ctx)
        xbc_cm = _lat_colmajor(b3(xbc_raw), n_ctx)
        dt_cm = _lat_colmajor(b3(dt_raw), n_ctx)
        xbc_act = dwconv_seq(xbc_cm, ssd_conv_w[li], ssd_conv_b[li], n_ctx, silu=True)
        bias_row = _pad_cols(ssd_dt_bias[li].reshape(1, -1), LANE)
        a_row = _pad_cols(-jnp.exp(ssd_a_log[li]).reshape(1, -1), LANE)
        yd_ssd = [y.reshape(t_all, MIX_WIDTH) for y in ssd_scan(xbc_act, dt_cm, bias_row, a_row, n_ctx)]
        d_row = jnp.repeat(ssd_d[li], SSD_HEAD_DIM)[None, :]
        yb_cm = ssd_out(yd_ssd[0], yd_ssd[1], xbc_act.reshape(t_all, SSD_CONV_CH), z_cm.reshape(t_all, MIX_WIDTH),
                        d_row, ssd_norm_w[li])
        yb = _lat_colmajor(b3(yb_cm), n_ctx, inverse=True).reshape(t_all, MIX_WIDTH)

        wg = jnp.stack([jnp.pad(gla_gate_w[li, dr], ((dr * GLA_GATE_RANK, LANE - (dr + 1) * GLA_GATE_RANK), (0, 0)))
                        for dr in range(2)])
        os_ = [o.reshape(t_all, MIX_WIDTH)
               for o in gla_scan(b3(q_g), b3(k_g), b3(v_g), b3(r_g), wg, gla_gate_b[li][:, None, :], n_ctx)]
        yc = gla_out(os_[0], os_[1], g_g, gla_norm_w[li])

        u_hy = dwconv_seq(b3(p_hy), hy_conv_w[li], hy_conv_b[li], n_ctx, silu=False)
        hy_p = (hy_w1[li], hy_b1[li], hy_freq1[li], hy_w2[li], hy_b2[li], hy_freq2[li], hy_w3[li])
        filts = [hyena_filters(n_lat, *hy_p)]
        segs = [(n_ctx, n_lat)]
        if not last:
            filts.append(hyena_filters(n_ctx, *hy_p))
            segs.append((0, n_ctx))
        spectra = hyena_spectrum(filts)
        zero_rows = (0, n_ctx) if last else None
        nt = MIX_WIDTH // LANE
        y1 = hyena_conv(u_hy, 0, u_hy, nt, hy_bias[li][0:1], 0, spectra, segs, zero_rows)
        yd = hyena_conv(y1, 0, u_hy, 2 * nt, hy_bias[li][1:2], 1, spectra, segs, zero_rows)
        yd = yd.reshape(t_all, MIX_WIDTH)

        merged = merge_branches((ya, yb, yc, yd), gates, w_branch[li].astype(BF16))
        xa = matmul_gated_residual(merged, w_out[li].astype(BF16), xa, tabs[2], tiles_pb)

        h2, route = moe_router(xa, tabs[3], tabs[4], moe_group_w[li], moe_group_b[li], moe_expert_w[li],
                               moe_expert_b[li], tiles_pb)
        n_tiles = -(-t_all // MOE_TM) + MOE_GROUPS
        pos, tsrc, tgrp, tact = moe_plan(route, n_tiles)
        xs = moe_scatter(h2, pos, n_tiles * MOE_TM)
        ys = moe_experts(xs, tsrc, tgrp, tact, moe_w_gate, moe_w_up, moe_w_down, li)
        xa = moe_gather_residual(ys, pos, xa, tabs[5], tiles_pb)

    return final_norm(xa.reshape(bsz, t_b, d), final_norm_w, n_ctx)
```

```python
import functools
import math

import numpy as np
import jax
import jax.numpy as jnp
from jax import lax
from jax.experimental import pallas as pl
from jax.experimental.pallas import tpu as pltpu

F32 = jnp.float32
BF16 = jnp.bfloat16
HI = lax.Precision.HIGHEST

D_MODEL = 2048
GRID_W = 64
NORM_EPS = 1e-6
N_BRANCH = 4
MIX_WIDTH = 768
S5_GROUP = 16
S5_GROUPS = MIX_WIDTH // S5_GROUP
S5_STATE = 64
SSD_HEAD_DIM = 64
SSD_HEADS = MIX_WIDTH // SSD_HEAD_DIM
SSD_GROUPS = 2
SSD_STATE = 64
SSD_CHUNK = 128
SSD_CONV_CH = MIX_WIDTH + 2 * SSD_GROUPS * SSD_STATE
GLA_HEADS = 6
GLA_KEY_WIDTH = MIX_WIDTH // 2
GLA_DK = GLA_KEY_WIDTH // GLA_HEADS
GLA_DV = MIX_WIDTH // GLA_HEADS
GLA_GATE_RANK = 16
GLA_TAU = 16.0
GLA_CHUNK = 64
HY_ORDER = 2
HY_FILTER_DIM = 64
HY_BANDS = 16
HY_EMB = 2 * HY_BANDS + 1
HY_MAX_DECAY = math.log(1e-2) / 0.3
HY_MIN_DECAY = math.log(1e-2) / 1.5
MOE_GROUPS = 4
MOE_PER_GROUP = 4
MOE_EXPERTS = MOE_GROUPS * MOE_PER_GROUP
MOE_FF = 1024

IN_SIZES = (MIX_WIDTH, MIX_WIDTH, SSD_CONV_CH, 2 * SSD_HEADS, GLA_KEY_WIDTH, GLA_KEY_WIDTH, MIX_WIDTH, MIX_WIDTH,
            2 * GLA_GATE_RANK, (HY_ORDER + 1) * MIX_WIDTH, N_BRANCH * D_MODEL)
IN_OFFS = tuple(int(v) for v in np.cumsum((0,) + IN_SIZES))

LANE = 128
ROW_TILE = 256
VMEM_LIMIT = 56 * 1024 * 1024

FFT_N1 = 64
FFT_N2 = 128
FFT_N = FFT_N1 * FFT_N2


def _cparams(sem):
    return pltpu.CompilerParams(dimension_semantics=sem, vmem_limit_bytes=VMEM_LIMIT)


def _split3(a):
    hi = a.astype(BF16)
    r = a - hi.astype(F32)
    mid = r.astype(BF16)
    lo = (r - mid.astype(F32)).astype(BF16)
    return hi, mid, lo


def _dot_sel(sel, x):
    s = sel.astype(BF16)
    return sum(jnp.dot(s, t, preferred_element_type=F32) for t in _split3(x))


def _dot_by_sel(x, sel):
    s = sel.astype(BF16)
    return sum(jnp.dot(t, s, preferred_element_type=F32) for t in _split3(x))


def _split2(a):
    hi = a.astype(BF16)
    return hi, (a - hi.astype(F32)).astype(BF16)


def _dot3(a2, b):
    b_hi, b_lo = _split2(b)
    return (jnp.dot(a2[0], b_hi, preferred_element_type=F32) + jnp.dot(a2[0], b_lo, preferred_element_type=F32)
            + jnp.dot(a2[1], b_hi, preferred_element_type=F32))


def _pad_cols(a, width):
    return jnp.pad(a, [(0, 0)] * (a.ndim - 1) + [(0, width - a.shape[-1])])


def _mm_kernel(x_ref, w_ref, o_ref, *, precision):
    o_ref[...] = jnp.dot(x_ref[...], w_ref[...], preferred_element_type=F32,
                         precision=precision).astype(o_ref.dtype)


def matmul(x, w, tm, tn, out_dtype=F32, precision=None):
    m, k = x.shape
    n = w.shape[1]
    return pl.pallas_call(
        functools.partial(_mm_kernel, precision=precision),
        grid=(n // tn, m // tm),
        in_specs=[pl.BlockSpec((tm, k), lambda j, i: (i, 0)), pl.BlockSpec((k, tn), lambda j, i: (0, j))],
        out_specs=pl.BlockSpec((tm, tn), lambda j, i: (i, j)),
        out_shape=jax.ShapeDtypeStruct((m, n), out_dtype),
        compiler_params=_cparams(("parallel", "parallel")),
    )(x, w)


def _proj_kernel(x_ref, w_ref, *o_refs):
    acc = jnp.dot(x_ref[...], w_ref[...], preferred_element_type=F32)
    off = 0
    for o_ref in o_refs:
        wd = o_ref.shape[-1]
        o_ref[...] = acc[:, off:off + wd].astype(o_ref.dtype)
        off += wd


def proj_split(x, w, widths, tm=512):
    m, k = x.shape
    n = w.shape[1]
    assert sum(widths) == n
    return pl.pallas_call(
        _proj_kernel,
        grid=(m // tm,),
        in_specs=[pl.BlockSpec((tm, k), lambda i: (i, 0)), pl.BlockSpec((k, n), lambda i: (0, 0))],
        out_specs=[pl.BlockSpec((tm, wd), lambda i: (i, 0)) for wd in widths],
        out_shape=[jax.ShapeDtypeStruct((m, wd), F32) for wd in widths],
        compiler_params=_cparams(("parallel",)),
    )(x, w)


def _adaln_kernel(c_ref, w_ref, b_ref, o_ref):
    c = c_ref[...]
    s = c * jax.nn.sigmoid(c)
    o_ref[...] = jnp.dot(s, w_ref[...], preferred_element_type=F32, precision=HI) + b_ref[...]


def adaln(cc, ada_w, ada_b, tn=1024):
    depth, d, n = ada_w.shape
    return pl.pallas_call(
        _adaln_kernel,
        grid=(depth, n // tn),
        in_specs=[pl.BlockSpec((8, d), lambda l, j: (0, 0)),
                  pl.BlockSpec((None, d, tn), lambda l, j: (l, 0, j)),
                  pl.BlockSpec((None, 1, tn), lambda l, j: (l, 0, j))],
        out_specs=pl.BlockSpec((None, 8, tn), lambda l, j: (l, 0, j)),
        out_shape=jax.ShapeDtypeStruct((depth, 8, n), F32),
        compiler_params=_cparams(("parallel", "parallel")),
    )(cc, ada_w, ada_b.reshape(depth, 1, n))


def _seg_index(i, tiles_per_batch):
    b = i // tiles_per_batch
    return 2 * b + jnp.where(i % tiles_per_batch == 0, 0, 1)


def _modnorm_kernel(x_ref, sh_ref, sc_ref, o_ref):
    x = x_ref[...]
    ms = jnp.mean(x * x, axis=-1, keepdims=True)
    o_ref[...] = (x * lax.rsqrt(ms + NORM_EPS) * (1.0 + sc_ref[...]) + sh_ref[...]).astype(o_ref.dtype)


def modnorm(x, shift, scale, tiles_per_batch):
    t, d = x.shape
    seg = lambda i: (_seg_index(i, tiles_per_batch), 0, 0)
    return pl.pallas_call(
        _modnorm_kernel,
        grid=(t // ROW_TILE,),
        in_specs=[pl.BlockSpec((ROW_TILE, d), lambda i: (i, 0)),
                  pl.BlockSpec((None, 1, d), seg), pl.BlockSpec((None, 1, d), seg)],
        out_specs=pl.BlockSpec((ROW_TILE, d), lambda i: (i, 0)),
        out_shape=jax.ShapeDtypeStruct((t, d), BF16),
        compiler_params=_cparams(("parallel",)),
    )(x, shift, scale)


def _dwconv_kernel(x_ref, w_ref, b_ref, o_ref, *, n_ctx, silu):
    x = x_ref[...]
    t = x.shape[0]
    rows = lax.broadcasted_iota(jnp.int32, x.shape, 0)
    prev = jnp.where((rows == 0) | (rows == n_ctx), 0.0, pltpu.roll(x, 1, 0))
    nxt = jnp.where((rows == n_ctx - 1) | (rows == t - 1), 0.0, pltpu.roll(x, t - 1, 0))
    y = w_ref[0:1, :] * prev + w_ref[1:2, :] * x + w_ref[2:3, :] * nxt + b_ref[...]
    if silu:
        y = y * jax.nn.sigmoid(y)
    o_ref[...] = y


def dwconv_seq(x, w, b, n_ctx, silu, tc=LANE):
    bsz, t, ch = x.shape
    return pl.pallas_call(
        functools.partial(_dwconv_kernel, n_ctx=n_ctx, silu=silu),
        grid=(bsz, ch // tc),
        in_specs=[pl.BlockSpec((None, t, tc), lambda b_, j: (b_, 0, j)),
                  pl.BlockSpec((3, tc), lambda b_, j: (0, j)),
                  pl.BlockSpec((1, tc), lambda b_, j: (0, j))],
        out_specs=pl.BlockSpec((None, t, tc), lambda b_, j: (b_, 0, j)),
        out_shape=jax.ShapeDtypeStruct((bsz, t, ch), F32),
        compiler_params=_cparams(("parallel", "parallel")),
    )(x, w, b.reshape(1, ch))


S5_Q = 8
S5_TG = LANE // S5_GROUP
S5_TS = S5_TG * S5_STATE
S5_NT = MIX_WIDTH // LANE


def _cmul(ar, ai, br, bi):
    return ar * br - ai * bi, ar * bi + ai * br


def s5_weights(lam_re, lam_im, log_step, b_re, b_im, c_re, c_im, reverse):
    g, p, q = S5_GROUPS, S5_STATE, S5_Q
    step = jnp.exp(log_step)[:, None]
    mag = jnp.exp(lam_re * step)
    ab_re = mag * jnp.cos(lam_im * step)
    ab_im = mag * jnp.sin(lam_im * step)
    den = lam_re * lam_re + lam_im * lam_im
    zr, zi = _cmul(ab_re - 1.0, ab_im, lam_re, -lam_im)
    zr, zi = zr / den, zi / den
    bb_re, bb_im = _cmul(zr[..., None], zi[..., None], b_re, b_im)
    pw_re, pw_im = [jnp.ones_like(ab_re)], [jnp.zeros_like(ab_im)]
    for _ in range(q):
        r, i = _cmul(pw_re[-1], pw_im[-1], ab_re, ab_im)
        pw_re.append(r)
        pw_im.append(i)
    pw_re, pw_im = jnp.stack(pw_re), jnp.stack(pw_im)
    cp_re = c_re[None] * pw_re[:, :, None, :] - c_im[None] * pw_im[:, :, None, :]
    cp_im = c_re[None] * pw_im[:, :, None, :] + c_im[None] * pw_re[:, :, None, :]
    kd = (jnp.einsum('dghp,gpk->dghk', cp_re[:q], bb_re, precision=HI)
          - jnp.einsum('dghp,gpk->dghk', cp_im[:q], bb_im, precision=HI))
    def group_blocks(tab, w):
        lead = tab.shape[:-3]
        rows = tab.shape[-2]
        expand = jnp.asarray(np.tile(np.eye(w, dtype=np.float32), (1, S5_TG)))
        wide = jnp.dot(tab, expand, precision=HI).reshape(lead + (S5_NT, S5_TG, rows, S5_TG * w))
        own = (np.arange(S5_TG * w)[None, None, :] // w) == np.arange(S5_TG)[:, None, None]
        return jnp.where(jnp.asarray(own), wide, 0.0).reshape(lead + (S5_NT, S5_TG * rows, S5_TG * w))

    kblk = group_blocks(jnp.swapaxes(kd, -1, -2), S5_GROUP)
    zero_blk = jnp.zeros_like(kblk[0])
    lag = (lambda s, r: s - r) if reverse else (lambda s, r: r - s)
    ktoep = jnp.concatenate(
        [jnp.concatenate([kblk[lag(s, r)] if lag(s, r) >= 0 else zero_blk for r in range(q)], axis=-1)
         for s in range(q)], axis=-2)
    e_of_s = (np.arange(q) if reverse else (q - 1 - np.arange(q)))
    pb_re = pw_re[e_of_s][..., None] * bb_re[None] - pw_im[e_of_s][..., None] * bb_im[None]
    pb_im = pw_re[e_of_s][..., None] * bb_im[None] + pw_im[e_of_s][..., None] * bb_re[None]
    pblk = [group_blocks(jnp.swapaxes(t, -1, -2), p) for t in (pb_re, pb_im)]
    bw = jnp.concatenate([jnp.concatenate([pblk[0][s], pblk[1][s]], axis=-1) for s in range(q)], axis=-2)
    f_of_r = (q - np.arange(q)) if reverse else (np.arange(q) + 1)
    cblk = [group_blocks(jnp.swapaxes(t[f_of_r], -1, -2), S5_GROUP) for t in (cp_re, -cp_im)]
    cw = jnp.concatenate([jnp.concatenate([cb[r] for r in range(q)], axis=-1) for cb in cblk], axis=-2)
    ar, ai = pw_re[q], pw_im[q]
    lv = []
    for _ in range(16):
        lv.append(jnp.concatenate([ar.reshape(S5_NT, S5_TS), ai.reshape(S5_NT, S5_TS)], axis=-1))
        ar, ai = _cmul(ar, ai, ar, ai)
    a2 = jnp.stack(lv, axis=1)
    return ktoep.astype(BF16), bw.astype(BF16), cw.astype(BF16), a2


def _s5_seg_scan(xr, xi, a2_ref, reverse):
    n = xr.shape[0]
    rows = lax.broadcasted_iota(jnp.int32, xr.shape, 0)
    k, shift = 0, 1
    while shift < n:
        ar = a2_ref[k:k + 1, :S5_TS]
        ai = a2_ref[k:k + 1, S5_TS:]
        if reverse:
            keep = rows < n - shift
            sr, si = pltpu.roll(xr, n - shift, 0), pltpu.roll(xi, n - shift, 0)
        else:
            keep = rows >= shift
            sr, si = pltpu.roll(xr, shift, 0), pltpu.roll(xi, shift, 0)
        sr = jnp.where(keep, sr, 0.0)
        si = jnp.where(keep, si, 0.0)
        xr, xi = xr + ar * sr - ai * si, xi + ar * si + ai * sr
        k += 1
        shift *= 2
    return xr, xi


def _s5_entering(sr, si, carry, a2_ref, reverse):
    n = sr.shape[0]
    rows = lax.broadcasted_iota(jnp.int32, sr.shape, 0)
    first = (n - 1) if reverse else 0
    if carry is not None:
        cr, ci = carry
        ar, ai = a2_ref[0:1, :S5_TS], a2_ref[0:1, S5_TS:]
        ir, ii = _cmul(ar, ai, cr, ci)
        sr = jnp.where(rows == first, sr + ir, sr)
        si = jnp.where(rows == first, si + ii, si)
    hr, hi = _s5_seg_scan(sr, si, a2_ref, reverse)
    if reverse:
        out = (hr[0:1], hi[0:1])
        er, ei = pltpu.roll(hr, n - 1, 0), pltpu.roll(hi, n - 1, 0)
    else:
        out = (hr[n - 1:n], hi[n - 1:n])
        er, ei = pltpu.roll(hr, 1, 0), pltpu.roll(hi, 1, 0)
    if carry is None:
        er = jnp.where(rows == first, 0.0, er)
        ei = jnp.where(rows == first, 0.0, ei)
    else:
        er = jnp.where(rows == first, cr, er)
        ei = jnp.where(rows == first, ci, ei)
    return er, ei, out


def _s5_kernel(u_ref, kt_ref, bw_ref, cw_ref, a2_ref, y_ref, *, n_ctx_rows, reverse):
    n = u_ref.shape[0] // S5_Q
    x = jnp.concatenate([u_ref[pl.ds(s, n, stride=S5_Q), :] for s in range(S5_Q)], axis=1).astype(BF16)
    y = jnp.dot(x, kt_ref[...], preferred_element_type=F32)
    st = jnp.dot(x, bw_ref[...], preferred_element_type=F32)
    sr, si = st[:, :S5_TS], st[:, S5_TS:]
    cr, ci, carry = _s5_entering(sr[:n_ctx_rows], si[:n_ctx_rows], None, a2_ref, reverse)
    lr, li, _ = _s5_entering(sr[n_ctx_rows:], si[n_ctx_rows:], carry, a2_ref, reverse)
    h = jnp.concatenate([jnp.concatenate([cr, lr], axis=0), jnp.concatenate([ci, li], axis=0)], axis=1)
    y = y + jnp.dot(h.astype(BF16), cw_ref[...], preferred_element_type=F32)
    for r in range(S5_Q):
        y_ref[pl.ds(r, n, stride=S5_Q), :] = y[:, r * LANE:(r + 1) * LANE]


def s5_scan(u, weights, n_ctx, reverse):
    bsz, t, ch = u.shape
    ktoep, bw, cw, a2 = weights
    qk = S5_Q * LANE
    return pl.pallas_call(
        functools.partial(_s5_kernel, n_ctx_rows=n_ctx // S5_Q, reverse=reverse),
        grid=(S5_NT, bsz),
        in_specs=[pl.BlockSpec((None, t, LANE), lambda j, b: (b, 0, j)),
                  pl.BlockSpec((None, qk, qk), lambda j, b: (j, 0, 0)),
                  pl.BlockSpec((None, qk, 2 * S5_TS), lambda j, b: (j, 0, 0)),
                  pl.BlockSpec((None, 2 * S5_TS, qk), lambda j, b: (j, 0, 0)),
                  pl.BlockSpec((None, 16, 2 * S5_TS), lambda j, b: (j, 0, 0))],
        out_specs=pl.BlockSpec((None, t, LANE), lambda j, b: (b, 0, j)),
        out_shape=jax.ShapeDtypeStruct((bsz, t, ch), F32),
        compiler_params=_cparams(("parallel", "parallel")),
    )(u, ktoep, bw, cw, a2)


def _s5_out_kernel(yf_ref, yb_ref, u_ref, d_ref, w_ref, o_ref):
    y = yf_ref[...] + yb_ref[...] + d_ref[...] * u_ref[...]
    g = jax.nn.gelu(y)
    z = jnp.dot(g.astype(BF16), w_ref[...], preferred_element_type=F32)
    o_ref[...] = (g * jax.nn.sigmoid(z)).astype(o_ref.dtype)


def s5_out(yf, yb, u, d_skip, glu_w, tm=512):
    t, ch = u.shape
    row = pl.BlockSpec((tm, ch), lambda i: (i, 0))
    return pl.pallas_call(
        _s5_out_kernel,
        grid=(t // tm,),
        in_specs=[row, row, row, pl.BlockSpec((1, ch), lambda i: (0, 0)), pl.BlockSpec((ch, ch), lambda i: (0, 0))],
        out_specs=row,
        out_shape=jax.ShapeDtypeStruct((t, ch), BF16),
        compiler_params=_cparams(("parallel",)),
    )(yf, yb, u, d_skip.reshape(1, ch), glu_w.astype(BF16))


def _softplus(x):
    return jnp.maximum(x, 0.0) + jnp.log1p(jnp.exp(-jnp.abs(x)))


def _ssd_kernel(xf_ref, dtf_ref, xb_ref, dtb_ref, bias_ref, a_ref, spread_ref, tri_ref, yf_ref, yb_ref, st_ref, *,
                bsz):
    @pl.when(pl.program_id(0) == 0)
    def _():
        st_ref[...] = jnp.zeros_like(st_ref)

    for d, (x_ref, dt_ref, y_ref) in enumerate(((xf_ref, dtf_ref, yf_ref), (xb_ref, dtb_ref, yb_ref))):
        for b in range(bsz):
            _ssd_step(x_ref.at[b], dt_ref.at[b], bias_ref, a_ref, spread_ref.at[d], tri_ref.at[d], y_ref.at[b],
                      st_ref.at[d, b], d=d, reverse=d == 1)


def _ssd_head_spread():
    m = np.zeros((2, LANE, MIX_WIDTH), np.float32)
    for d in range(2):
        for h in range(SSD_HEADS):
            m[d, d * SSD_HEADS + h, h * SSD_HEAD_DIM:(h + 1) * SSD_HEAD_DIM] = 1.0
    return m


def _ssd_step(xbc_ref, dt_ref, bias_ref, a_ref, spread_ref, tri_ref, y_ref, st_ref, *, d, reverse):
    q = SSD_CHUNK
    hp = SSD_HEAD_DIM
    hg = SSD_HEADS // SSD_GROUPS
    dt = _softplus(dt_ref[...] + bias_ref[...])
    adt = dt * a_ref[...]
    cs = _dot_sel(tri_ref[...], adt)
    cs_t = cs.T
    spread = spread_ref[...]
    dt_x = _dot_by_sel(dt, spread)
    cs_x = _dot_by_sel(cs, spread)
    tot_x = cs_x[0:1, :] if reverse else cs_x[q - 1:q, :]
    ecs_x = jnp.exp(cs_x)
    dec_x = jnp.exp(tot_x - cs_x)
    etot_x = jnp.exp(tot_x)
    xdt = xbc_ref[:, :MIX_WIDTH] * dt_x
    xw = (xdt * dec_x).astype(BF16)
    xdt = xdt.astype(BF16)
    bm = xbc_ref[:, MIX_WIDTH:MIX_WIDTH + SSD_GROUPS * SSD_STATE]
    cm = xbc_ref[:, MIX_WIDTH + SSD_GROUPS * SSD_STATE:]
    bm_t = bm.T
    li = lax.broadcasted_iota(jnp.int32, (q, q), 0)
    si = lax.broadcasted_iota(jnp.int32, (q, q), 1)
    mask = (si >= li) if reverse else (si <= li)
    first_of_pair = lax.broadcasted_iota(jnp.int32, (q, 2 * hp), 1) < hp
    gw = hg * hp
    for g in range(SSD_GROUPS):
        cg = cm[:, g * SSD_STATE:(g + 1) * SSD_STATE].astype(BF16)
        bg = bm[:, g * SSD_STATE:(g + 1) * SSD_STATE].astype(BF16)
        gmat = lax.dot_general(cg, bg, (((1,), (1,)), ((), ())), preferred_element_type=F32)
        yoff = jnp.dot(cg, st_ref[g].astype(BF16), preferred_element_type=F32) * ecs_x[:, g * gw:(g + 1) * gw]
        for pr in range(hg // 2):
            lanes = slice(g * gw + pr * 2 * hp, g * gw + (pr + 1) * 2 * hp)
            xp = xdt[:, lanes]
            yd = []
            for hl in (2 * pr, 2 * pr + 1):
                col = d * SSD_HEADS + g * hg + hl
                seg = cs[:, col:col + 1] - cs_t[col:col + 1, :]
                lmat = jnp.exp(jnp.where(mask, seg, -jnp.inf))
                yd.append(jnp.dot((gmat * lmat).astype(BF16), xp, preferred_element_type=F32))
            y_ref[:, lanes] = jnp.where(first_of_pair, yd[0], yd[1]) + yoff[:, pr * 2 * hp:(pr + 1) * 2 * hp]
        new = jnp.dot(bm_t[g * SSD_STATE:(g + 1) * SSD_STATE, :].astype(BF16), xw[:, g * gw:(g + 1) * gw],
                      preferred_element_type=F32)
        st_ref[g] = st_ref[g] * etot_x[:, g * gw:(g + 1) * gw] + new


def ssd_scan(xbc, dt_raw, dt_bias_row, a_row, n_ctx):
    bsz, t, _ = xbc.shape
    q = SSD_CHUNK
    orders = _chunk_order(t // q, n_ctx // q)
    blk = lambda w, cidx: pl.BlockSpec((bsz, q, w), lambda i: (0, cidx(i), 0))
    vec = pl.BlockSpec((1, LANE), lambda i: (0, 0))
    return pl.pallas_call(
        functools.partial(_ssd_kernel, bsz=bsz),
        grid=(t // q,),
        in_specs=[blk(w, c) for c in orders for w in (SSD_CONV_CH, LANE)]
        + [vec, vec, pl.BlockSpec((2, LANE, MIX_WIDTH), lambda i: (0, 0, 0)),
           pl.BlockSpec((2, q, q), lambda i: (0, 0, 0))],
        out_specs=[blk(MIX_WIDTH, c) for c in orders],
        out_shape=[jax.ShapeDtypeStruct((bsz, t, MIX_WIDTH), F32)] * 2,
        scratch_shapes=[pltpu.VMEM((2, bsz, SSD_GROUPS, SSD_STATE, (SSD_HEADS // SSD_GROUPS) * SSD_HEAD_DIM), F32)],
        compiler_params=_cparams(("arbitrary",)),
        name="ssd_scan",
    )(xbc, dt_raw, xbc, dt_raw, dt_bias_row, a_row, jnp.asarray(_ssd_head_spread()), _cumsum_mats(q))


def _ssd_out_kernel(yf_ref, yb_ref, xbc_ref, z_ref, d_ref, nw_ref, o_ref):
    z = z_ref[...]
    y = (d_ref[...] * xbc_ref[...] + yf_ref[...] + yb_ref[...]) * (z * jax.nn.sigmoid(z))
    ms = jnp.mean(y * y, axis=-1, keepdims=True)
    o_ref[...] = (y * lax.rsqrt(ms + NORM_EPS) * nw_ref[...]).astype(o_ref.dtype)


def ssd_out(yf, yb, xbc, z, d_row, norm_w, tm=512):
    t, ch = z.shape
    row = pl.BlockSpec((tm, ch), lambda i: (i, 0))
    vec = pl.BlockSpec((1, ch), lambda i: (0, 0))
    return pl.pallas_call(
        _ssd_out_kernel,
        grid=(t // tm,),
        in_specs=[row, row, pl.BlockSpec((tm, ch), lambda i: (i, 0)), row, vec, vec],
        out_specs=row,
        out_shape=jax.ShapeDtypeStruct((t, ch), BF16),
        compiler_params=_cparams(("parallel",)),
    )(yf, yb, xbc, z, d_row, norm_w.reshape(1, ch))


def _gla_kernel(*refs, bsz):
    ins, (wg_ref, gb_ref, tri_ref), outs, st_ref = refs[:8], refs[8:11], refs[11:13], refs[13]

    @pl.when(pl.program_id(0) == 0)
    def _():
        st_ref[...] = jnp.zeros_like(st_ref)

    for d in range(2):
        q_ref, k_ref, v_ref, r_ref = ins[4 * d:4 * d + 4]
        for b in range(bsz):
            _gla_step(q_ref.at[b], k_ref.at[b], v_ref.at[b], r_ref.at[b], wg_ref.at[d], gb_ref.at[d],
                      tri_ref.at[d], outs[d].at[b], st_ref.at[d, b], reverse=d == 1)


def _gla_step(q_ref, k_ref, v_ref, r_ref, wg_ref, gb_ref, tri_ref, o_ref, st_ref, *, reverse):
    qs = GLA_CHUNK
    zl = jnp.dot(r_ref[...], wg_ref[...], preferred_element_type=F32, precision=HI) + gb_ref[...]
    la = (jnp.minimum(zl, 0.0) - jnp.log1p(jnp.exp(-jnp.abs(zl)))) / GLA_TAU
    bc = _dot_sel(tri_ref[...], la)
    tot = jnp.sum(la, axis=0, keepdims=True)
    qd = (q_ref[...] * (GLA_DK ** -0.5) * jnp.exp(bc)).astype(BF16)
    ki = (k_ref[...] * jnp.exp(-bc)).astype(BF16)
    ke = (k_ref[...] * jnp.exp(tot - bc)).astype(BF16)
    gam = jnp.exp(tot)
    ti = lax.broadcasted_iota(jnp.int32, (qs, qs), 0)
    si = lax.broadcasted_iota(jnp.int32, (qs, qs), 1)
    mask = (si >= ti) if reverse else (si <= ti)
    for h in range(GLA_HEADS):
        ks = slice(h * GLA_DK, (h + 1) * GLA_DK)
        vs = slice(h * GLA_DV, (h + 1) * GLA_DV)
        vh = v_ref[:, vs].astype(BF16)
        sc = lax.dot_general(qd[:, ks], ki[:, ks], (((1,), (1,)), ((), ())), preferred_element_type=F32)
        sc = jnp.where(mask, sc, 0.0).astype(BF16)
        o_intra = jnp.dot(sc, vh, preferred_element_type=F32)
        st = st_ref[:, ks]
        o_inter = lax.dot_general(qd[:, ks], st.astype(BF16), (((1,), (1,)), ((), ())), preferred_element_type=F32)
        o_ref[:, vs] = o_intra + o_inter
        ds = lax.dot_general(vh, ke[:, ks], (((0,), (0,)), ((), ())), preferred_element_type=F32)
        st_ref[:, ks] = st * gam[:, ks] + ds


def _chunk_order(nchunk, nctx):
    fwd = lambda i: i
    bwd = lambda i: jnp.where(i < nctx, nctx - 1 - i, nchunk + nctx - 1 - i)
    return fwd, bwd


def _cumsum_mats(q):
    tri = np.tril(np.ones((q, q), np.float32))
    return jnp.asarray(np.stack([tri, tri.T]))


def gla_scan(q, k, v, r, wg, gb, n_ctx):
    bsz, t, _ = q.shape
    qs = GLA_CHUNK
    orders = _chunk_order(t // qs, n_ctx // qs)
    blk = lambda w, cidx: pl.BlockSpec((bsz, qs, w), lambda i: (0, cidx(i), 0))
    widths = (GLA_KEY_WIDTH, GLA_KEY_WIDTH, MIX_WIDTH, LANE)
    return pl.pallas_call(
        functools.partial(_gla_kernel, bsz=bsz),
        grid=(t // qs,),
        in_specs=[blk(w, c) for c in orders for w in widths]
        + [pl.BlockSpec((2, LANE, GLA_KEY_WIDTH), lambda i: (0, 0, 0)),
           pl.BlockSpec((2, 1, GLA_KEY_WIDTH), lambda i: (0, 0, 0)),
           pl.BlockSpec((2, qs, qs), lambda i: (0, 0, 0))],
        out_specs=[blk(MIX_WIDTH, c) for c in orders],
        out_shape=[jax.ShapeDtypeStruct((bsz, t, MIX_WIDTH), F32)] * 2,
        scratch_shapes=[pltpu.VMEM((2, bsz, GLA_DV, GLA_KEY_WIDTH), F32)],
        compiler_params=_cparams(("arbitrary",)),
        name="gla_scan",
    )(q, k, v, r, q, k, v, r, wg, gb, _cumsum_mats(qs))


def _gla_out_kernel(of_ref, ob_ref, g_ref, nw_ref, o_ref):
    g = g_ref[...]
    sg = g * jax.nn.sigmoid(g)
    for h in range(GLA_HEADS):
        vs = slice(h * GLA_DV, (h + 1) * GLA_DV)
        o = of_ref[:, vs] + ob_ref[:, vs]
        ms = jnp.mean(o * o, axis=-1, keepdims=True)
        o_ref[:, vs] = (o * lax.rsqrt(ms + NORM_EPS) * nw_ref[...] * sg[:, vs]).astype(o_ref.dtype)


def gla_out(of, ob, g, norm_w, tm=512):
    t, ch = g.shape
    row = pl.BlockSpec((tm, ch), lambda i: (i, 0))
    return pl.pallas_call(
        _gla_out_kernel,
        grid=(t // tm,),
        in_specs=[row, row, row, pl.BlockSpec((1, GLA_DV), lambda i: (0, 0))],
        out_specs=row,
        out_shape=jax.ShapeDtypeStruct((t, ch), BF16),
        compiler_params=_cparams(("parallel",)),
    )(of, ob, g, norm_w.reshape(1, GLA_DV))


def _hy_emb(n):
    t = np.linspace(0.0, 1.0, n)[:, None]
    freqs = np.linspace(1e-4, HY_BANDS - 1, HY_BANDS)
    ang = (2.0 * math.pi / n) * np.arange(n)[:, None] * freqs[None, :]
    emb = np.concatenate([t, np.cos(ang), -np.sin(ang)], axis=-1)
    return np.pad(emb, ((0, 0), (0, LANE - HY_EMB))).astype(np.float32)


def _hy_filter_kernel(emb_ref, w1_ref, b1_ref, f1_ref, w2_ref, b2_ref, f2_ref, w3_ref, dl_ref, o_ref):
    emb = emb_ref[...]
    h = jnp.sin(f1_ref[...] * (jnp.dot(emb, w1_ref[...], preferred_element_type=F32, precision=HI) + b1_ref[...]))
    h = jnp.sin(f2_ref[...] * (jnp.dot(h, w2_ref[...], preferred_element_type=F32, precision=HI) + b2_ref[...]))
    h = jnp.dot(h, w3_ref[...], preferred_element_type=F32, precision=HI)
    o_ref[...] = h * jnp.exp(-emb[:, 0:1] * dl_ref[...])


def hyena_filters(n, w1, b1, f1, w2, b2, f2, w3, tm=256):
    nf = w3.shape[1]
    emb = jnp.asarray(_hy_emb(n))
    deltas = np.abs(np.linspace(HY_MIN_DECAY, HY_MAX_DECAY, MIX_WIDTH)).astype(np.float32)
    dl = jnp.asarray(np.tile(deltas, nf // MIX_WIDTH)[None, :])
    w1p = jnp.pad(w1, ((0, LANE - HY_EMB), (0, 0)))
    fd = HY_FILTER_DIM
    vec = lambda w: pl.BlockSpec((1, w), lambda i: (0, 0))
    return pl.pallas_call(
        _hy_filter_kernel,
        grid=(n // tm,),
        in_specs=[pl.BlockSpec((tm, LANE), lambda i: (i, 0)),
                  pl.BlockSpec((LANE, fd), lambda i: (0, 0)), vec(fd), vec(fd),
                  pl.BlockSpec((fd, fd), lambda i: (0, 0)), vec(fd), vec(fd),
                  pl.BlockSpec((fd, nf), lambda i: (0, 0)), vec(nf)],
        out_specs=pl.BlockSpec((tm, nf), lambda i: (i, 0)),
        out_shape=jax.ShapeDtypeStruct((n, nf), F32),
        compiler_params=_cparams(("parallel",)),
    )(emb, w1p, b1.reshape(1, fd), f1.reshape(1, fd), w2, b2.reshape(1, fd), f2.reshape(1, fd), w3, dl)


class _HyDft:
    def __init__(self, n):
        big = 2 * n
        self.n1 = big // FFT_N2
        self.rows = n // FFT_N2
        self.kh = self.n1 // 2 + 1
        self.khp = -(-self.kh // 8) * 8
        k1 = np.arange(self.kh)
        ang = 2.0 * np.pi * np.outer(k1, np.arange(self.rows)) / self.n1
        fwd = np.zeros((2 * self.khp, self.rows))
        fwd[:self.kh] = np.cos(ang)
        fwd[self.khp:self.khp + self.kh] = -np.sin(ang)
        wgt = np.full(self.kh, 2.0)
        wgt[0] = wgt[-1] = 1.0
        inv = np.zeros((self.rows, 2 * self.khp))
        inv[:, :self.kh] = (wgt[:, None] * np.cos(ang)).T / big
        inv[:, self.khp:self.khp + self.kh] = -(wgt[:, None] * np.sin(ang)).T / big
        tw = 2.0 * np.pi * np.outer(k1, np.arange(FFT_N2)) / big
        f32 = lambda a: np.ascontiguousarray(a, dtype=np.float32)
        self.fwd, self.inv = f32(fwd), f32(inv)
        self.tw_re, self.tw_im = f32(np.cos(tw)[:, :, None]), f32(-np.sin(tw)[:, :, None])
        self.small = self.rows < 8


def _inner_dft_mats():
    k2 = np.arange(FFT_N2)
    f2 = np.exp(-2j * np.pi * np.outer(k2, k2) / FFT_N2)
    w2 = np.block([[f2.real, -f2.imag], [f2.imag, f2.real]])
    w2c = np.block([[f2.real, f2.imag], [-f2.imag, f2.real]])
    return w2.astype(np.float32), w2c.astype(np.float32)


_W2, _W2C = _inner_dft_mats()
HY_UNROLL = 4


def _sc_load(ref, idx):
    return jnp.concatenate([ref[w, idx, :] for w in range(ref.shape[0])], axis=1)


def _sc_store(ref, idx, val):
    for w in range(ref.shape[0]):
        ref[w, idx, :] = val[:, w * LANE:(w + 1) * LANE]


def _hy_outer_fwd(load_rows, cst, fwd_ref, xr_ref, xi_ref):
    if cst.small:
        blocks = [load_rows(n1 * FFT_N2, FFT_N2, None) for n1 in range(cst.rows)]
        for k1 in range(cst.kh):
            xr = sum(float(cst.fwd[k1, n1]) * blocks[n1] for n1 in range(cst.rows))
            xi = sum(float(cst.fwd[cst.khp + k1, n1]) * blocks[n1] for n1 in range(cst.rows))
            _sc_store(xr_ref, pl.ds(k1 * FFT_N2, FFT_N2), xr)
            _sc_store(xi_ref, pl.ds(k1 * FFT_N2, FFT_N2), xi)
        return

    fwd2 = _split2(fwd_ref[...])

    def body(n2, c):
        res = _dot3(fwd2, load_rows(n2, cst.rows, FFT_N2))
        _sc_store(xr_ref, pl.ds(n2, cst.khp, stride=FFT_N2), res[:cst.khp])
        _sc_store(xi_ref, pl.ds(n2, cst.khp, stride=FFT_N2), res[cst.khp:])
        return c
    lax.fori_loop(0, FFT_N2, body, 0, unroll=HY_UNROLL)


def _hy_inner_fwd(k1, xr_ref, xi_ref, twr_ref, twi_ref, w2_ref):
    start = k1 * FFT_N2
    rows = pl.ds(start if isinstance(k1, int) else pl.multiple_of(start, FFT_N2), FFT_N2)
    xr, xi = _sc_load(xr_ref, rows), _sc_load(xi_ref, rows)
    twr, twi = twr_ref[k1], twi_ref[k1]
    ar = xr * twr - xi * twi
    ai = xr * twi + xi * twr
    ys = jnp.dot(w2_ref[...], jnp.concatenate([ar, ai], axis=0).astype(BF16), preferred_element_type=F32)
    return rows, twr, twi, ys[:FFT_N2], ys[FFT_N2:]


def _hy_for_k1(cst, body):
    if cst.small:
        for k1 in range(cst.kh):
            body(k1, 0)
    else:
        lax.fori_loop(0, cst.kh, body, 0, unroll=3 if cst.kh % 3 == 0 else 1)


def _hy_spectrum_part(cst, hf_ref, hb_ref, fwd_ref, twr_ref, twi_ref, w2_ref, hr_ref, hi_ref, xr_ref, xi_ref):
    width = hf_ref.shape[1]

    def load_rows(start, count, stride):
        idx = pl.ds(start, count) if stride is None else pl.ds(start, count, stride=stride)
        return jnp.concatenate([hf_ref[idx, :], hb_ref[idx, :]], axis=1)
    _hy_outer_fwd(load_rows, cst, fwd_ref, xr_ref, xi_ref)
    hb0 = hb_ref[0:1, :]

    def body(k1, c):
        rows, _, _, yr, yi = _hy_inner_fwd(k1, xr_ref, xi_ref, twr_ref, twi_ref, w2_ref)
        hr_ref[rows, :] = yr[:, :width] + yr[:, width:] - hb0
        hi_ref[rows, :] = yi[:, :width] - yi[:, width:]
        return c
    _hy_for_k1(cst, body)


def _hy_spectrum_kernel(*refs, parts):
    w2_ref = refs[0]
    pos = 1
    n_in = 5 * len(parts)
    outs = refs[1 + n_in:1 + n_in + 2 * len(parts)]
    xr_ref, xi_ref = refs[-2:]
    for p, cst in enumerate(parts):
        hf_ref, hb_ref, fwd_ref, twr_ref, twi_ref = refs[pos:pos + 5]
        pos += 5
        _hy_spectrum_part(cst, hf_ref, hb_ref, fwd_ref, twr_ref, twi_ref, w2_ref, outs[2 * p], outs[2 * p + 1],
                          xr_ref, xi_ref)


def hyena_spectrum(filts):
    parts = tuple(_HyDft(f.shape[0]) for f in filts)
    nt = MIX_WIDTH // LANE
    width = HY_ORDER * MIX_WIDTH
    args = [jnp.asarray(_W2).astype(BF16)]
    in_specs = [pl.BlockSpec((2 * FFT_N2, 2 * FFT_N2), lambda t: (0, 0))]
    out_specs, out_shape = [], []
    for f, cst in zip(filts, parts):
        n = f.shape[0]
        args += [f, f, jnp.asarray(cst.fwd), jnp.asarray(cst.tw_re), jnp.asarray(cst.tw_im)]
        in_specs += [pl.BlockSpec((n, LANE), lambda t: (0, (t // nt) * 2 * nt + t % nt)),
                     pl.BlockSpec((n, LANE), lambda t: (0, (t // nt) * 2 * nt + nt + t % nt)),
                     pl.BlockSpec(cst.fwd.shape, lambda t: (0, 0)),
                     pl.BlockSpec(cst.tw_re.shape, lambda t: (0, 0, 0)),
                     pl.BlockSpec(cst.tw_im.shape, lambda t: (0, 0, 0))]
        out_specs += [pl.BlockSpec((cst.kh * FFT_N2, LANE), lambda t: (0, t))] * 2
        out_shape += [jax.ShapeDtypeStruct((cst.kh * FFT_N2, width), F32)] * 2
    rows = max(c.khp for c in parts) * FFT_N2
    outs = pl.pallas_call(
        functools.partial(_hy_spectrum_kernel, parts=parts),
        grid=(width // LANE,),
        in_specs=in_specs, out_specs=out_specs, out_shape=out_shape,
        scratch_shapes=[pltpu.VMEM((2, rows, LANE), F32)] * 2,
        compiler_params=_cparams(("parallel",)),
        name="hyena_spectrum",
    )(*args)
    return [(outs[2 * p], outs[2 * p + 1]) for p in range(len(parts))]


def _hy_conv_part(cst, row0, u_ref, g_ref, bias_ref, hr_ref, hi_ref, fwd_ref, inv_ref, twr_ref, twi_ref,
                  w2_ref, w2c_ref, o_ref, xr_ref, xi_ref):
    bsz = u_ref.shape[0]

    def load_rows(start, count, stride):
        idx = pl.ds(row0 + start, count) if stride is None else pl.ds(row0 + start, count, stride=stride)
        return jnp.concatenate([u_ref[b, idx, :] for b in range(bsz)], axis=1)
    _hy_outer_fwd(load_rows, cst, fwd_ref, xr_ref, xi_ref)

    def body(k1, c):
        rows, twr, twi, yr, yi = _hy_inner_fwd(k1, xr_ref, xi_ref, twr_ref, twi_ref, w2_ref)
        hr = jnp.concatenate([hr_ref[rows, :]] * bsz, axis=1)
        hi = jnp.concatenate([hi_ref[rows, :]] * bsz, axis=1)
        zr = yr * hr - yi * hi
        zi = yr * hi + yi * hr
        vs = jnp.dot(w2c_ref[...], jnp.concatenate([zr, zi], axis=0).astype(BF16), preferred_element_type=F32)
        vr, vi = vs[:FFT_N2], vs[FFT_N2:]
        _sc_store(xr_ref, rows, vr * twr + vi * twi)
        _sc_store(xi_ref, rows, vi * twr - vr * twi)
        return c
    _hy_for_k1(cst, body)

    bias = bias_ref[...]

    def emit(idx, y):
        for b in range(bsz):
            yb = y[:, b * LANE:(b + 1) * LANE]
            o_ref[b, idx, :] = (g_ref[b, idx, :] * (yb + bias * u_ref[b, idx, :])).astype(o_ref.dtype)

    if cst.small:
        for n1 in range(cst.rows):
            y = sum(float(cst.inv[n1, k1]) * _sc_load(xr_ref, pl.ds(k1 * FFT_N2, FFT_N2))
                    + float(cst.inv[n1, cst.khp + k1]) * _sc_load(xi_ref, pl.ds(k1 * FFT_N2, FFT_N2))
                    for k1 in range(cst.kh))
            emit(pl.ds(row0 + n1 * FFT_N2, FFT_N2), y)
        return

    inv2 = _split2(inv_ref[...])

    def out_body(n2, c):
        z = jnp.concatenate([_sc_load(xr_ref, pl.ds(n2, cst.khp, stride=FFT_N2)),
                             _sc_load(xi_ref, pl.ds(n2, cst.khp, stride=FFT_N2))], axis=0)
        y = _dot3(inv2, z)
        emit(pl.ds(row0 + n2, cst.rows, stride=FFT_N2), y)
        return c
    lax.fori_loop(0, FFT_N2, out_body, 0, unroll=HY_UNROLL)


def _hy_conv_kernel(*refs, parts, row0s, zero_rows):
    u_ref, g_ref, bias_ref, w2_ref, w2c_ref = refs[:5]
    o_ref, xr_ref, xi_ref = refs[-3:]
    pos = 5
    for cst, row0 in zip(parts, row0s):
        hr_ref, hi_ref, fwd_ref, inv_ref, twr_ref, twi_ref = refs[pos:pos + 6]
        pos += 6
        _hy_conv_part(cst, row0, u_ref, g_ref, bias_ref, hr_ref, hi_ref, fwd_ref, inv_ref, twr_ref, twi_ref,
                      w2_ref, w2c_ref, o_ref, xr_ref, xi_ref)
    if zero_rows is not None:
        lo, hi = zero_rows
        o_ref[:, lo:hi, :] = jnp.zeros((o_ref.shape[0], hi - lo, o_ref.shape[2]), o_ref.dtype)


def hyena_conv(u, u_blk, gate, g_blk, bias, order, spectra, segs, zero_rows):
    out_dtype = F32
    bsz, t, _ = u.shape
    nt = MIX_WIDTH // LANE
    parts = tuple(_HyDft(n) for _, n in segs)
    args = [u, gate, bias, jnp.asarray(_W2).astype(BF16), jnp.asarray(_W2C).astype(BF16)]
    mat = pl.BlockSpec((2 * FFT_N2, 2 * FFT_N2), lambda j: (0, 0))
    in_specs = [pl.BlockSpec((bsz, t, LANE), lambda j: (0, 0, u_blk + j)),
                pl.BlockSpec((bsz, t, LANE), lambda j: (0, 0, g_blk + j)),
                pl.BlockSpec((1, LANE), lambda j: (0, j)), mat, mat]
    for (hr, hi), cst in zip(spectra, parts):
        args += [hr, hi, jnp.asarray(cst.fwd), jnp.asarray(cst.inv), jnp.asarray(cst.tw_re), jnp.asarray(cst.tw_im)]
        hspec = pl.BlockSpec((cst.kh * FFT_N2, LANE), lambda j: (0, order * nt + j))
        in_specs += [hspec, hspec, pl.BlockSpec(cst.fwd.shape, lambda j: (0, 0)),
                     pl.BlockSpec(cst.inv.shape, lambda j: (0, 0)),
                     pl.BlockSpec(cst.tw_re.shape, lambda j: (0, 0, 0)),
                     pl.BlockSpec(cst.tw_im.shape, lambda j: (0, 0, 0))]
    rows = max(c.khp for c in parts) * FFT_N2
    return pl.pallas_call(
        functools.partial(_hy_conv_kernel, parts=parts, row0s=tuple(r for r, _ in segs), zero_rows=zero_rows),
        grid=(nt,),
        in_specs=in_specs,
        out_specs=pl.BlockSpec((bsz, t, LANE), lambda j: (0, 0, j)),
        out_shape=jax.ShapeDtypeStruct((bsz, t, MIX_WIDTH), out_dtype),
        scratch_shapes=[pltpu.VMEM((bsz, rows, LANE), F32)] * 2,
        compiler_params=_cparams(("parallel",)),
        name="hyena_conv",
    )(*args)


def _merge_kernel(ya_ref, yb_ref, yc_ref, yd_ref, ga_ref, gb_ref, gc_ref, gd_ref, w_ref, o_ref):
    acc = None
    for b, (y_ref, g_ref) in enumerate(((ya_ref, ga_ref), (yb_ref, gb_ref), (yc_ref, gc_ref), (yd_ref, gd_ref))):
        t = jax.nn.sigmoid(g_ref[...].astype(F32)) * jnp.dot(y_ref[...].astype(BF16), w_ref[b], preferred_element_type=F32)
        acc = t if acc is None else acc + t
    o_ref[...] = acc.astype(o_ref.dtype)


def merge_branches(ys, gates, w_branch, tm=512, tn=512):
    t, ch = ys[0].shape
    nj = D_MODEL // tn
    yspec = pl.BlockSpec((tm, ch), lambda j, i: (i, 0))
    gspecs = [pl.BlockSpec((tm, tn), functools.partial(lambda j, i, b: (i, b * nj + j), b=b)) for b in range(N_BRANCH)]
    return pl.pallas_call(
        _merge_kernel,
        grid=(nj, t // tm),
        in_specs=[yspec] * 4 + gspecs + [pl.BlockSpec((N_BRANCH, ch, tn), lambda j, i: (0, 0, j))],
        out_specs=pl.BlockSpec((tm, tn), lambda j, i: (i, j)),
        out_shape=jax.ShapeDtypeStruct((t, D_MODEL), BF16),
        compiler_params=_cparams(("parallel", "parallel")),
    )(*ys, gates, gates, gates, gates, w_branch)


def _mm_resid_kernel(a_ref, w_ref, x_ref, g_ref, o_ref):
    o_ref[...] = x_ref[...] + g_ref[...] * jnp.dot(a_ref[...], w_ref[...], preferred_element_type=F32)


def matmul_gated_residual(a, w, x, gate, tiles_per_batch, tn=512):
    t, k = a.shape
    n = w.shape[1]
    return pl.pallas_call(
        _mm_resid_kernel,
        grid=(n // tn, t // ROW_TILE),
        in_specs=[pl.BlockSpec((ROW_TILE, k), lambda j, i: (i, 0)),
                  pl.BlockSpec((k, tn), lambda j, i: (0, j)),
                  pl.BlockSpec((ROW_TILE, tn), lambda j, i: (i, j)),
                  pl.BlockSpec((None, 1, tn), lambda j, i: (_seg_index(i, tiles_per_batch), 0, j))],
        out_specs=pl.BlockSpec((ROW_TILE, tn), lambda j, i: (i, j)),
        out_shape=jax.ShapeDtypeStruct((t, n), F32),
        compiler_params=_cparams(("parallel", "parallel")),
    )(a, w, x, gate)


ROUTE_OFF = MOE_GROUPS
TOK_S = D_MODEL // LANE
DSP_S = TOK_S + 8
MOE_TM = 512


def _store_token_major(ref, val, slab=TOK_S):
    rows = val.shape[0]
    for s in range(TOK_S):
        ref[pl.ds(s, rows, stride=slab), :] = val[:, s * LANE:(s + 1) * LANE]


def _load_token_major(ref, rows, slab=TOK_S):
    return jnp.concatenate([ref[pl.ds(s, rows, stride=slab), :] for s in range(TOK_S)], axis=1)


def _router_kernel(x_ref, sh_ref, sc_ref, wr_ref, br_ref, h_ref, comb_ref):
    x = x_ref[...]
    ms = jnp.mean(x * x, axis=-1, keepdims=True)
    h = x * lax.rsqrt(ms + NORM_EPS) * (1.0 + sc_ref[...]) + sh_ref[...]
    _store_token_major(h_ref, h, DSP_S)
    logits = jnp.dot(h, wr_ref[...], preferred_element_type=F32, precision=HI) + br_ref[...]
    lane = lax.broadcasted_iota(jnp.int32, logits.shape, 1).astype(F32)
    neg = -jnp.inf
    gmask = lane < MOE_GROUPS
    glog = jnp.where(gmask, logits, neg)
    gmax = jnp.max(glog, axis=-1, keepdims=True)
    g_idx = jnp.min(jnp.where(glog == gmax, lane, float(LANE)), axis=-1, keepdims=True)
    p_g = 1.0 / jnp.sum(jnp.exp(glog - gmax), axis=-1, keepdims=True)
    e_lo = ROUTE_OFF + g_idx * MOE_PER_GROUP
    emask = (lane >= e_lo) & (lane < e_lo + MOE_PER_GROUP)
    v1 = jnp.max(jnp.where(emask, logits, neg), axis=-1, keepdims=True)
    i1 = jnp.min(jnp.where(emask & (logits == v1), lane, float(LANE)), axis=-1, keepdims=True)
    emask2 = emask & (lane != i1)
    v2 = jnp.max(jnp.where(emask2, logits, neg), axis=-1, keepdims=True)
    i2 = jnp.min(jnp.where(emask2 & (logits == v2), lane, float(LANE)), axis=-1, keepdims=True)
    e21 = jnp.exp(v2 - v1)
    w1 = p_g / (1.0 + e21)
    w2 = p_g * e21 / (1.0 + e21)
    comb = jnp.where(lane == i1, w1, jnp.where(lane == i2, w2, 0.0))
    route = jnp.where(lane == 0, g_idx, comb)
    comb_ref[...] = route
    rows = x.shape[0]
    h_ref[pl.ds(TOK_S, rows, stride=DSP_S), :] = route
    for s in range(TOK_S + 1, DSP_S):
        h_ref[pl.ds(s, rows, stride=DSP_S), :] = jnp.zeros_like(route)


def moe_router(x, shift, scale, group_w, group_b, expert_w, expert_b, tiles_per_batch):
    t, d = x.shape
    wr = _pad_cols(jnp.concatenate([group_w, expert_w], axis=1), LANE)
    br = _pad_cols(jnp.concatenate([group_b, expert_b])[None, :], LANE)
    seg = lambda i: (_seg_index(i, tiles_per_batch), 0, 0)
    return pl.pallas_call(
        _router_kernel,
        grid=(t // ROW_TILE,),
        in_specs=[pl.BlockSpec((ROW_TILE, d), lambda i: (i, 0)),
                  pl.BlockSpec((None, 1, d), seg), pl.BlockSpec((None, 1, d), seg),
                  pl.BlockSpec((d, LANE), lambda i: (0, 0)), pl.BlockSpec((1, LANE), lambda i: (0, 0))],
        out_specs=[pl.BlockSpec((ROW_TILE * DSP_S, LANE), lambda i: (i, 0)),
                   pl.BlockSpec((ROW_TILE, LANE), lambda i: (i, 0))],
        out_shape=[jax.ShapeDtypeStruct((t * DSP_S, LANE), F32), jax.ShapeDtypeStruct((t, LANE), F32)],
        compiler_params=_cparams(("parallel",)),
    )(x, shift, scale, wr, br)


def moe_plan(route, n_tiles):
    g = route[:, 0].astype(jnp.int32)
    oh = (g[:, None] == jnp.arange(MOE_GROUPS, dtype=jnp.int32)[None, :]).astype(jnp.int32)
    cnt = jnp.sum(oh, axis=0)
    rank = jnp.sum((jnp.cumsum(oh, axis=0) - oh) * oh, axis=1)
    ptiles = (cnt + MOE_TM - 1) // MOE_TM
    ends = jnp.cumsum(ptiles)
    off = (ends - ptiles) * MOE_TM
    pos = jnp.sum(oh * off[None, :], axis=1) + rank
    n_act = ends[-1]
    tid = jnp.arange(n_tiles, dtype=jnp.int32)
    tsrc = jnp.minimum(tid, n_act - 1)
    tgrp = jnp.sum((tsrc[:, None] >= ends[None, :]).astype(jnp.int32), axis=1)
    tact = (tid < n_act).astype(jnp.int32)
    return pos, tsrc, tgrp, tact


def _slab(ref, row, size):
    return ref.at[pl.ds(pl.multiple_of(row * size, size), size)]


def _moe_scatter_kernel(pos_ref, src_ref, init_ref, dst_ref, sem):
    del init_ref
    base = pl.program_id(0) * ROW_TILE

    def issue(r, c):
        pltpu.make_async_copy(_slab(src_ref, r, DSP_S), _slab(dst_ref, pos_ref[base + r], DSP_S), sem).start()
        return c
    lax.fori_loop(0, ROW_TILE, issue, 0)
    pltpu.make_async_copy(src_ref, dst_ref.at[pl.ds(0, ROW_TILE * DSP_S)], sem).wait()


def moe_scatter(slabs, pos, n_slots):
    t = pos.shape[0]
    init = jnp.zeros((n_slots * DSP_S, LANE), F32)
    return pl.pallas_call(
        _moe_scatter_kernel,
        grid_spec=pltpu.PrefetchScalarGridSpec(
            num_scalar_prefetch=1, grid=(t // ROW_TILE,),
            in_specs=[pl.BlockSpec((ROW_TILE * DSP_S, LANE), lambda i, p: (i, 0)),
                      pl.BlockSpec(memory_space=pl.ANY)],
            out_specs=pl.BlockSpec(memory_space=pl.ANY),
            scratch_shapes=[pltpu.SemaphoreType.DMA(())]),
        out_shape=jax.ShapeDtypeStruct(init.shape, F32),
        input_output_aliases={2: 0},
        compiler_params=_cparams(("arbitrary",)),
    )(pos, slabs, init)


def _moe_expert_kernel(tsrc_ref, tgrp_ref, tact_ref, xs_ref, wg_ref, wu_ref, wd_ref, ys_ref, x_ref, acc_ref):
    i, e, f = pl.program_id(0), pl.program_id(1), pl.program_id(2)
    active = tact_ref[i] == 1
    first = (e == 0) & (f == 0)
    last = (e == pl.num_programs(1) - 1) & (f == pl.num_programs(2) - 1)

    @pl.when(active & first)
    def _():
        x_ref[...] = _load_token_major(xs_ref, MOE_TM, DSP_S).astype(BF16)
        acc_ref[...] = jnp.zeros_like(acc_ref)

    @pl.when(active)
    def _():
        x = x_ref[...]
        a = jnp.dot(x, wg_ref[...].astype(BF16), preferred_element_type=F32)
        u = jnp.dot(x, wu_ref[...].astype(BF16), preferred_element_type=F32)
        act = (a * jax.nn.sigmoid(a) * u).astype(BF16)
        route = xs_ref[pl.ds(TOK_S, MOE_TM, stride=DSP_S), :]
        lane = lax.broadcasted_iota(jnp.int32, route.shape, 1)
        mine = lane == ROUTE_OFF + tgrp_ref[i] * MOE_PER_GROUP + e
        w = jnp.sum(jnp.where(mine, route, 0.0), axis=-1, keepdims=True)
        acc_ref[...] += w * jnp.dot(act, wd_ref[...].astype(BF16), preferred_element_type=F32)

    @pl.when(active & last)
    def _():
        _store_token_major(ys_ref, acc_ref[...])

    @pl.when(jnp.logical_not(active) & first)
    def _():
        ys_ref[...] = jnp.zeros_like(ys_ref)


def moe_experts(xs, tsrc, tgrp, tact, w_gate, w_up, w_down, li, tf=256):
    n_tiles = tsrc.shape[0]
    _, _, d, ff = w_gate.shape
    wsel = lambda i, e, f, ts, tg, ta: (li, tg[i] * MOE_PER_GROUP + e, 0, f)
    rows = lambda i, e, f, ts, tg, ta: (ts[i], 0)
    return pl.pallas_call(
        _moe_expert_kernel,
        grid_spec=pltpu.PrefetchScalarGridSpec(
            num_scalar_prefetch=3, grid=(n_tiles, MOE_PER_GROUP, ff // tf),
            in_specs=[pl.BlockSpec((MOE_TM * DSP_S, LANE), rows),
                      pl.BlockSpec((None, None, d, tf), wsel),
                      pl.BlockSpec((None, None, d, tf), wsel),
                      pl.BlockSpec((None, None, tf, d),
                                   lambda i, e, f, ts, tg, ta: (li, tg[i] * MOE_PER_GROUP + e, f, 0))],
            out_specs=pl.BlockSpec((MOE_TM * TOK_S, LANE), lambda i, e, f, ts, tg, ta: (i, 0)),
            scratch_shapes=[pltpu.VMEM((MOE_TM, d), BF16), pltpu.VMEM((MOE_TM, d), F32)]),
        out_shape=jax.ShapeDtypeStruct((n_tiles * MOE_TM * TOK_S, LANE), F32),
        compiler_params=_cparams(("arbitrary", "arbitrary", "arbitrary")),
    )(tsrc, tgrp, tact, xs, w_gate, w_up, w_down)


def _moe_gather_kernel(pos_ref, ys_ref, x_ref, g_ref, o_ref, buf_ref, sem):
    base = pl.program_id(0) * ROW_TILE

    def issue(r, c):
        pltpu.make_async_copy(_slab(ys_ref, pos_ref[base + r], TOK_S), _slab(buf_ref, r, TOK_S), sem).start()
        return c
    lax.fori_loop(0, ROW_TILE, issue, 0)
    pltpu.make_async_copy(ys_ref.at[pl.ds(0, ROW_TILE * TOK_S)], buf_ref, sem).wait()
    o_ref[...] = x_ref[...] + g_ref[...] * _load_token_major(buf_ref, ROW_TILE)


def moe_gather_residual(ys, pos, x, gate, tiles_per_batch):
    t, d = x.shape
    return pl.pallas_call(
        _moe_gather_kernel,
        grid_spec=pltpu.PrefetchScalarGridSpec(
            num_scalar_prefetch=1, grid=(t // ROW_TILE,),
            in_specs=[pl.BlockSpec(memory_space=pl.ANY),
                      pl.BlockSpec((ROW_TILE, d), lambda i, p: (i, 0)),
                      pl.BlockSpec((None, 1, d), lambda i, p: (_seg_index(i, tiles_per_batch), 0, 0))],
            out_specs=pl.BlockSpec((ROW_TILE, d), lambda i, p: (i, 0)),
            scratch_shapes=[pltpu.VMEM((ROW_TILE * TOK_S, LANE), F32), pltpu.SemaphoreType.DMA(())]),
        out_shape=jax.ShapeDtypeStruct((t, d), F32),
        compiler_params=_cparams(("arbitrary",)),
    )(pos, ys, x, gate)


def _final_norm_kernel(x_ref, w_ref, o_ref):
    x = x_ref[...]
    ms = jnp.mean(x * x, axis=-1, keepdims=True)
    o_ref[...] = x * lax.rsqrt(ms + NORM_EPS) * w_ref[...]


def final_norm(x, w, n_ctx):
    bsz, t, d = x.shape
    off = n_ctx // ROW_TILE
    return pl.pallas_call(
        _final_norm_kernel,
        grid=(bsz, (t - n_ctx) // ROW_TILE),
        in_specs=[pl.BlockSpec((None, ROW_TILE, d), lambda b, i: (b, i + off, 0)),
                  pl.BlockSpec((1, d), lambda b, i: (0, 0))],
        out_specs=pl.BlockSpec((None, ROW_TILE, d), lambda b, i: (b, i, 0)),
        out_shape=jax.ShapeDtypeStruct((bsz, t - n_ctx, d), F32),
        compiler_params=_cparams(("parallel", "parallel")),
    )(x, w.reshape(1, d))


def _lat_colmajor(t, n_ctx, inverse=False):
    bsz, _, ch = t.shape
    lat = t[:, n_ctx:]
    rows = lat.shape[1] // GRID_W
    shp = (bsz, GRID_W, rows, ch) if inverse else (bsz, rows, GRID_W, ch)
    lat = lat.reshape(shp).transpose(0, 2, 1, 3).reshape(bsz, -1, ch)
    return jnp.concatenate([t[:, :n_ctx], lat], axis=1)


def _seg_table(ctx_vec, lat_mat):
    bsz = lat_mat.shape[0]
    tab = jnp.stack([jnp.broadcast_to(ctx_vec[None, :], lat_mat.shape), lat_mat], axis=1)
    return tab.reshape(2 * bsz, 1, -1)


def kernel(x, c, ctx, c_ctx, ada_w, ada_b, w_in, s5_lambda_re, s5_lambda_im, s5_log_step, s5_b_re, s5_b_im, s5_c_re, s5_c_im, s5_d, s5_glu_w, ssd_conv_w, ssd_conv_b, ssd_a_log, ssd_dt_bias, ssd_d, ssd_norm_w, gla_gate_w, gla_gate_b, gla_norm_w, hy_conv_w, hy_conv_b, hy_w1, hy_b1, hy_freq1, hy_w2, hy_b2, hy_freq2, hy_w3, hy_bias, w_branch, w_out, moe_group_w, moe_group_b, moe_expert_w, moe_expert_b, moe_w_gate, moe_w_up, moe_w_down, final_norm_w):
    bsz, n_lat, d = x.shape
    n_ctx = ctx.shape[1]
    depth = ada_w.shape[0]
    t_b = n_ctx + n_lat
    t_all = bsz * t_b
    tiles_pb = t_b // ROW_TILE
    assert n_ctx == ROW_TILE and n_lat % ROW_TILE == 0 and n_lat == FFT_N // 2 and bsz + 1 <= 8

    xa = jnp.concatenate([ctx, x], axis=1).reshape(t_all, d)
    cc = jnp.pad(jnp.concatenate([c, c_ctx[None, :]], axis=0), ((0, 8 - bsz - 1), (0, 0)))
    mod = adaln(cc, ada_w, ada_b)

    for li in range(depth):
        last = li == depth - 1
        m6 = mod[li].reshape(8, 6, d)
        tabs = [_seg_table(m6[bsz, k], m6[:bsz, k]) for k in range(6)]
        h = modnorm(xa, tabs[0], tabs[1], tiles_pb)

        wl = w_in[li].astype(BF16)
        seg = lambda k0, k1: wl[:, IN_OFFS[k0]:IN_OFFS[k1]]
        (u_s5,) = proj_split(h, seg(0, 1), (MIX_WIDTH,))
        w_ssd = jnp.concatenate([seg(1, 3), _pad_cols(seg(3, 4), LANE)], axis=1)
        z_ssd, xbc_raw, dt_raw = proj_split(h, w_ssd, (MIX_WIDTH, SSD_CONV_CH, LANE))
        w_gla = jnp.concatenate([seg(4, 8), _pad_cols(seg(8, 9), LANE)], axis=1)
        q_g, k_g, v_g, g_g, r_g = proj_split(h, w_gla, (GLA_KEY_WIDTH, GLA_KEY_WIDTH, MIX_WIDTH, MIX_WIDTH, LANE))
        (p_hy,) = proj_split(h, seg(9, 10), (3 * MIX_WIDTH,))
        gates = matmul(h, seg(10, 11), 512, 1024, out_dtype=BF16)

        b3 = lambda a: a.reshape(bsz, t_b, a.shape[-1])

        ys = []
        for dr in range(2):
            wts = s5_weights(s5_lambda_re[li, dr], s5_lambda_im[li, dr], s5_log_step[li, dr], s5_b_re[li, dr],
                             s5_b_im[li, dr], s5_c_re[li, dr], s5_c_im[li, dr], dr == 1)
            ys.append(s5_scan(b3(u_s5), wts, n_ctx, dr == 1).reshape(t_all, MIX_WIDTH))
        ya = s5_out(ys[0], ys[1], u_s5, s5_d[li], s5_glu_w[li])

        z_cm = _lat_colmajor(b3(z_ssd), n_ctx)
        xbc_cm = _lat_colmajor(b3(xbc_raw), n_ctx)
        dt_cm = _lat_colmajor(b3(dt_raw), n_ctx)
        xbc_act = dwconv_seq(xbc_cm, ssd_conv_w[li], ssd_conv_b[li], n_ctx, silu=True)
        bias_row = _pad_cols(ssd_dt_bias[li].reshape(1, -1), LANE)
        a_row = _pad_cols(-jnp.exp(ssd_a_log[li]).reshape(1, -1), LANE)
        yd_ssd = [y.reshape(t_all, MIX_WIDTH) for y in ssd_scan(xbc_act, dt_cm, bias_row, a_row, n_ctx)]
        d_row = jnp.repeat(ssd_d[li], SSD_HEAD_DIM)[None, :]
        yb_cm = ssd_out(yd_ssd[0], yd_ssd[1], xbc_act.reshape(t_all, SSD_CONV_CH), z_cm.reshape(t_all, MIX_WIDTH),
                        d_row, ssd_norm_w[li])
        yb = _lat_colmajor(b3(yb_cm), n_ctx, inverse=True).reshape(t_all, MIX_WIDTH)

        wg = jnp.stack([jnp.pad(gla_gate_w[li, dr], ((dr * GLA_GATE_RANK, LANE - (dr + 1) * GLA_GATE_RANK), (0, 0)))
                        for dr in range(2)])
        os_ = [o.reshape(t_all, MIX_WIDTH)
               for o in gla_scan(b3(q_g), b3(k_g), b3(v_g), b3(r_g), wg, gla_gate_b[li][:, None, :], n_ctx)]
        yc = gla_out(os_[0], os_[1], g_g, gla_norm_w[li])

        u_hy = dwconv_seq(b3(p_hy), hy_conv_w[li], hy_conv_b[li], n_ctx, silu=False)
        hy_p = (hy_w1[li], hy_b1[li], hy_freq1[li], hy_w2[li], hy_b2[li], hy_freq2[li], hy_w3[li])
        filts = [hyena_filters(n_lat, *hy_p)]
        segs = [(n_ctx, n_lat)]
        if not last:
            filts.append(hyena_filters(n_ctx, *hy_p))
            segs.append((0, n_ctx))
        spectra = hyena_spectrum(filts)
        zero_rows = (0, n_ctx) if last else None
        nt = MIX_WIDTH // LANE
        y1 = hyena_conv(u_hy, 0, u_hy, nt, hy_bias[li][0:1], 0, spectra, segs, zero_rows)
        yd = hyena_conv(y1, 0, u_hy, 2 * nt, hy_bias[li][1:2], 1, spectra, segs, zero_rows)
        yd = yd.reshape(t_all, MIX_WIDTH)

        merged = merge_branches((ya, yb, yc, yd), gates, w_branch[li].astype(BF16))
        xa = matmul_gated_residual(merged, w_out[li].astype(BF16), xa, tabs[2], tiles_pb)

        h2, route = moe_router(xa, tabs[3], tabs[4], moe_group_w[li], moe_group_b[li], moe_expert_w[li],
                               moe_expert_b[li], tiles_pb)
        n_tiles = -(-t_all // MOE_TM) + MOE_GROUPS
        pos, tsrc, tgrp, tact = moe_plan(route, n_tiles)
        xs = moe_scatter(h2, pos, n_tiles * MOE_TM)
        ys = moe_experts(xs, tsrc, tgrp, tact, moe_w_gate, moe_w_up, moe_w_down, li)
        xa = moe_gather_residual(ys, pos, xa, tabs[5], tiles_pb)

    return final_norm(xa.reshape(bsz, t_b, d), final_norm_w, n_ctx)
```

```python
import functools
import math

import numpy as np
import jax
import jax.numpy as jnp
from jax import lax
from jax.experimental import pallas as pl
from jax.experimental.pallas import tpu as pltpu

F32 = jnp.float32
BF16 = jnp.bfloat16
HI = lax.Precision.HIGHEST

D_MODEL = 2048
GRID_W = 64
NORM_EPS = 1e-6
N_BRANCH = 4
MIX_WIDTH = 768
S5_GROUP = 16
S5_GROUPS = MIX_WIDTH // S5_GROUP
S5_STATE = 64
SSD_HEAD_DIM = 64
SSD_HEADS = MIX_WIDTH // SSD_HEAD_DIM
SSD_GROUPS = 2
SSD_STATE = 64
SSD_CHUNK = 128
SSD_CONV_CH = MIX_WIDTH + 2 * SSD_GROUPS * SSD_STATE
GLA_HEADS = 6
GLA_KEY_WIDTH = MIX_WIDTH // 2
GLA_DK = GLA_KEY_WIDTH // GLA_HEADS
GLA_DV = MIX_WIDTH // GLA_HEADS
GLA_GATE_RANK = 16
GLA_TAU = 16.0
GLA_CHUNK = 64
HY_ORDER = 2
HY_FILTER_DIM = 64
HY_BANDS = 16
HY_EMB = 2 * HY_BANDS + 1
HY_MAX_DECAY = math.log(1e-2) / 0.3
HY_MIN_DECAY = math.log(1e-2) / 1.5
MOE_GROUPS = 4
MOE_PER_GROUP = 4
MOE_EXPERTS = MOE_GROUPS * MOE_PER_GROUP
MOE_FF = 1024

IN_SIZES = (MIX_WIDTH, MIX_WIDTH, SSD_CONV_CH, 2 * SSD_HEADS, GLA_KEY_WIDTH, GLA_KEY_WIDTH, MIX_WIDTH, MIX_WIDTH,
            2 * GLA_GATE_RANK, (HY_ORDER + 1) * MIX_WIDTH, N_BRANCH * D_MODEL)
IN_OFFS = tuple(int(v) for v in np.cumsum((0,) + IN_SIZES))

LANE = 128
ROW_TILE = 256
VMEM_LIMIT = 56 * 1024 * 1024

FFT_N1 = 64
FFT_N2 = 128
FFT_N = FFT_N1 * FFT_N2


def _cparams(sem):
    return pltpu.CompilerParams(dimension_semantics=sem, vmem_limit_bytes=VMEM_LIMIT)


def _split3(a):
    hi = a.astype(BF16)
    r = a - hi.astype(F32)
    mid = r.astype(BF16)
    lo = (r - mid.astype(F32)).astype(BF16)
    return hi, mid, lo


def _dot_sel(sel, x):
    s = sel.astype(BF16)
    return sum(jnp.dot(s, t, preferred_element_type=F32) for t in _split3(x))


def _dot_by_sel(x, sel):
    s = sel.astype(BF16)
    return sum(jnp.dot(t, s, preferred_element_type=F32) for t in _split3(x))


def _split2(a):
    hi = a.astype(BF16)
    return hi, (a - hi.astype(F32)).astype(BF16)


def _dot3(a2, b):
    b_hi, b_lo = _split2(b)
    return (jnp.dot(a2[0], b_hi, preferred_element_type=F32) + jnp.dot(a2[0], b_lo, preferred_element_type=F32)
            + jnp.dot(a2[1], b_hi, preferred_element_type=F32))


def _pad_cols(a, width):
    return jnp.pad(a, [(0, 0)] * (a.ndim - 1) + [(0, width - a.shape[-1])])


def _mm_kernel(x_ref, w_ref, o_ref, *, precision):
    o_ref[...] = jnp.dot(x_ref[...], w_ref[...], preferred_element_type=F32,
                         precision=precision).astype(o_ref.dtype)


def matmul(x, w, tm, tn, out_dtype=F32, precision=None):
    m, k = x.shape
    n = w.shape[1]
    return pl.pallas_call(
        functools.partial(_mm_kernel, precision=precision),
        grid=(n // tn, m // tm),
        in_specs=[pl.BlockSpec((tm, k), lambda j, i: (i, 0)), pl.BlockSpec((k, tn), lambda j, i: (0, j))],
        out_specs=pl.BlockSpec((tm, tn), lambda j, i: (i, j)),
        out_shape=jax.ShapeDtypeStruct((m, n), out_dtype),
        compiler_params=_cparams(("parallel", "parallel")),
    )(x, w)


def _proj_kernel(x_ref, w_ref, *o_refs):
    acc = jnp.dot(x_ref[...], w_ref[...], preferred_element_type=F32)
    off = 0
    for o_ref in o_refs:
        wd = o_ref.shape[-1]
        o_ref[...] = acc[:, off:off + wd].astype(o_ref.dtype)
        off += wd


def proj_split(x, w, widths, tm=512):
    m, k = x.shape
    n = w.shape[1]
    assert sum(widths) == n
    return pl.pallas_call(
        _proj_kernel,
        grid=(m // tm,),
        in_specs=[pl.BlockSpec((tm, k), lambda i: (i, 0)), pl.BlockSpec((k, n), lambda i: (0, 0))],
        out_specs=[pl.BlockSpec((tm, wd), lambda i: (i, 0)) for wd in widths],
        out_shape=[jax.ShapeDtypeStruct((m, wd), F32) for wd in widths],
        compiler_params=_cparams(("parallel",)),
    )(x, w)


def _adaln_kernel(c_ref, w_ref, b_ref, o_ref):
    c = c_ref[...]
    s = c * jax.nn.sigmoid(c)
    o_ref[...] = jnp.dot(s, w_ref[...], preferred_element_type=F32, precision=HI) + b_ref[...]


def adaln(cc, ada_w, ada_b, tn=1024):
    depth, d, n = ada_w.shape
    return pl.pallas_call(
        _adaln_kernel,
        grid=(depth, n // tn),
        in_specs=[pl.BlockSpec((8, d), lambda l, j: (0, 0)),
                  pl.BlockSpec((None, d, tn), lambda l, j: (l, 0, j)),
                  pl.BlockSpec((None, 1, tn), lambda l, j: (l, 0, j))],
        out_specs=pl.BlockSpec((None, 8, tn), lambda l, j: (l, 0, j)),
        out_shape=jax.ShapeDtypeStruct((depth, 8, n), F32),
        compiler_params=_cparams(("parallel", "parallel")),
    )(cc, ada_w, ada_b.reshape(depth, 1, n))


def _seg_index(i, tiles_per_batch):
    b = i // tiles_per_batch
    return 2 * b + jnp.where(i % tiles_per_batch == 0, 0, 1)


def _modnorm_kernel(x_ref, sh_ref, sc_ref, o_ref):
    x = x_ref[...]
    ms = jnp.mean(x * x, axis=-1, keepdims=True)
    o_ref[...] = (x * lax.rsqrt(ms + NORM_EPS) * (1.0 + sc_ref[...]) + sh_ref[...]).astype(o_ref.dtype)


def modnorm(x, shift, scale, tiles_per_batch):
    t, d = x.shape
    seg = lambda i: (_seg_index(i, tiles_per_batch), 0, 0)
    return pl.pallas_call(
        _modnorm_kernel,
        grid=(t // ROW_TILE,),
        in_specs=[pl.BlockSpec((ROW_TILE, d), lambda i: (i, 0)),
                  pl.BlockSpec((None, 1, d), seg), pl.BlockSpec((None, 1, d), seg)],
        out_specs=pl.BlockSpec((ROW_TILE, d), lambda i: (i, 0)),
        out_shape=jax.ShapeDtypeStruct((t, d), BF16),
        compiler_params=_cparams(("parallel",)),
    )(x, shift, scale)


def _dwconv_kernel(x_ref, w_ref, b_ref, o_ref, *, n_ctx, silu):
    x = x_ref[...]
    t = x.shape[0]
    rows = lax.broadcasted_iota(jnp.int32, x.shape, 0)
    prev = jnp.where((rows == 0) | (rows == n_ctx), 0.0, pltpu.roll(x, 1, 0))
    nxt = jnp.where((rows == n_ctx - 1) | (rows == t - 1), 0.0, pltpu.roll(x, t - 1, 0))
    y = w_ref[0:1, :] * prev + w_ref[1:2, :] * x + w_ref[2:3, :] * nxt + b_ref[...]
    if silu:
        y = y * jax.nn.sigmoid(y)
    o_ref[...] = y


def dwconv_seq(x, w, b, n_ctx, silu, tc=LANE):
    bsz, t, ch = x.shape
    return pl.pallas_call(
        functools.partial(_dwconv_kernel, n_ctx=n_ctx, silu=silu),
        grid=(bsz, ch // tc),
        in_specs=[pl.BlockSpec((None, t, tc), lambda b_, j: (b_, 0, j)),
                  pl.BlockSpec((3, tc), lambda b_, j: (0, j)),
                  pl.BlockSpec((1, tc), lambda b_, j: (0, j))],
        out_specs=pl.BlockSpec((None, t, tc), lambda b_, j: (b_, 0, j)),
        out_shape=jax.ShapeDtypeStruct((bsz, t, ch), F32),
        compiler_params=_cparams(("parallel", "parallel")),
    )(x, w, b.reshape(1, ch))


S5_Q = 8
S5_TG = LANE // S5_GROUP
S5_TS = S5_TG * S5_STATE
S5_NT = MIX_WIDTH // LANE


def _cmul(ar, ai, br, bi):
    return ar * br - ai * bi, ar * bi + ai * br


def s5_weights(lam_re, lam_im, log_step, b_re, b_im, c_re, c_im, reverse):
    g, p, q = S5_GROUPS, S5_STATE, S5_Q
    step = jnp.exp(log_step)[:, None]
    mag = jnp.exp(lam_re * step)
    ab_re = mag * jnp.cos(lam_im * step)
    ab_im = mag * jnp.sin(lam_im * step)
    den = lam_re * lam_re + lam_im * lam_im
    zr, zi = _cmul(ab_re - 1.0, ab_im, lam_re, -lam_im)
    zr, zi = zr / den, zi / den
    bb_re, bb_im = _cmul(zr[..., None], zi[..., None], b_re, b_im)
    pw_re, pw_im = [jnp.ones_like(ab_re)], [jnp.zeros_like(ab_im)]
    for _ in range(q):
        r, i = _cmul(pw_re[-1], pw_im[-1], ab_re, ab_im)
        pw_re.append(r)
        pw_im.append(i)
    pw_re, pw_im = jnp.stack(pw_re), jnp.stack(pw_im)
    cp_re = c_re[None] * pw_re[:, :, None, :] - c_im[None] * pw_im[:, :, None, :]
    cp_im = c_re[None] * pw_im[:, :, None, :] + c_im[None] * pw_re[:, :, None, :]
    kd = (jnp.einsum('dghp,gpk->dghk', cp_re[:q], bb_re, precision=HI)
          - jnp.einsum('dghp,gpk->dghk', cp_im[:q], bb_im, precision=HI))
    def group_blocks(tab, w):
        lead = tab.shape[:-3]
        rows = tab.shape[-2]
        expand = jnp.asarray(np.tile(np.eye(w, dtype=np.float32), (1, S5_TG)))
        wide = jnp.dot(tab, expand, precision=HI).reshape(lead + (S5_NT, S5_TG, rows, S5_TG * w))
        own = (np.arange(S5_TG * w)[None, None, :] // w) == np.arange(S5_TG)[:, None, None]
        return jnp.where(jnp.asarray(own), wide, 0.0).reshape(lead + (S5_NT, S5_TG * rows, S5_TG * w))

    kblk = group_blocks(jnp.swapaxes(kd, -1, -2), S5_GROUP)
    zero_blk = jnp.zeros_like(kblk[0])
    lag = (lambda s, r: s - r) if reverse else (lambda s, r: r - s)
    ktoep = jnp.concatenate(
        [jnp.concatenate([kblk[lag(s, r)] if lag(s, r) >= 0 else zero_blk for r in range(q)], axis=-1)
         for s in range(q)], axis=-2)
    e_of_s = (np.arange(q) if reverse else (q - 1 - np.arange(q)))
    pb_re = pw_re[e_of_s][..., None] * bb_re[None] - pw_im[e_of_s][..., None] * bb_im[None]
    pb_im = pw_re[e_of_s][..., None] * bb_im[None] + pw_im[e_of_s][..., None] * bb_re[None]
    pblk = [group_blocks(jnp.swapaxes(t, -1, -2), p) for t in (pb_re, pb_im)]
    bw = jnp.concatenate([jnp.concatenate([pblk[0][s], pblk[1][s]], axis=-1) for s in range(q)], axis=-2)
    f_of_r = (q - np.arange(q)) if reverse else (np.arange(q) + 1)
    cblk = [group_blocks(jnp.swapaxes(t[f_of_r], -1, -2), S5_GROUP) for t in (cp_re, -cp_im)]
    cw = jnp.concatenate([jnp.concatenate([cb[r] for r in range(q)], axis=-1) for cb in cblk], axis=-2)
    ar, ai = pw_re[q], pw_im[q]
    lv = []
    for _ in range(16):
        lv.append(jnp.concatenate([ar.reshape(S5_NT, S5_TS), ai.reshape(S5_NT, S5_TS)], axis=-1))
        ar, ai = _cmul(ar, ai, ar, ai)
    a2 = jnp.stack(lv, axis=1)
    return ktoep.astype(BF16), bw.astype(BF16), cw.astype(BF16), a2


def _s5_seg_scan(xr, xi, a2_ref, reverse):
    n = xr.shape[0]
    rows = lax.broadcasted_iota(jnp.int32, xr.shape, 0)
    k, shift = 0, 1
    while shift < n:
        ar = a2_ref[k:k + 1, :S5_TS]
        ai = a2_ref[k:k + 1, S5_TS:]
        if reverse:
            keep = rows < n - shift
            sr, si = pltpu.roll(xr, n - shift, 0), pltpu.roll(xi, n - shift, 0)
        else:
            keep = rows >= shift
            sr, si = pltpu.roll(xr, shift, 0), pltpu.roll(xi, shift, 0)
        sr = jnp.where(keep, sr, 0.0)
        si = jnp.where(keep, si, 0.0)
        xr, xi = xr + ar * sr - ai * si, xi + ar * si + ai * sr
        k += 1
        shift *= 2
    return xr, xi


def _s5_entering(sr, si, carry, a2_ref, reverse):
    n = sr.shape[0]
    rows = lax.broadcasted_iota(jnp.int32, sr.shape, 0)
    first = (n - 1) if reverse else 0
    if carry is not None:
        cr, ci = carry
        ar, ai = a2_ref[0:1, :S5_TS], a2_ref[0:1, S5_TS:]
        ir, ii = _cmul(ar, ai, cr, ci)
        sr = jnp.where(rows == first, sr + ir, sr)
        si = jnp.where(rows == first, si + ii, si)
    hr, hi = _s5_seg_scan(sr, si, a2_ref, reverse)
    if reverse:
        out = (hr[0:1], hi[0:1])
        er, ei = pltpu.roll(hr, n - 1, 0), pltpu.roll(hi, n - 1, 0)
    else:
        out = (hr[n - 1:n], hi[n - 1:n])
        er, ei = pltpu.roll(hr, 1, 0), pltpu.roll(hi, 1, 0)
    if carry is None:
        er = jnp.where(rows == first, 0.0, er)
        ei = jnp.where(rows == first, 0.0, ei)
    else:
        er = jnp.where(rows == first, cr, er)
        ei = jnp.where(rows == first, ci, ei)
    return er, ei, out


def _s5_kernel(u_ref, kt_ref, bw_ref, cw_ref, a2_ref, y_ref, *, n_ctx_rows, reverse):
    n = u_ref.shape[0] // S5_Q
    x = jnp.concatenate([u_ref[pl.ds(s, n, stride=S5_Q), :] for s in range(S5_Q)], axis=1).astype(BF16)
    y = jnp.dot(x, kt_ref[...], preferred_element_type=F32)
    st = jnp.dot(x, bw_ref[...], preferred_element_type=F32)
    sr, si = st[:, :S5_TS], st[:, S5_TS:]
    cr, ci, carry = _s5_entering(sr[:n_ctx_rows], si[:n_ctx_rows], None, a2_ref, reverse)
    lr, li, _ = _s5_entering(sr[n_ctx_rows:], si[n_ctx_rows:], carry, a2_ref, reverse)
    h = jnp.concatenate([jnp.concatenate([cr, lr], axis=0), jnp.concatenate([ci, li], axis=0)], axis=1)
    y = y + jnp.dot(h.astype(BF16), cw_ref[...], preferred_element_type=F32)
    for r in range(S5_Q):
        y_ref[pl.ds(r, n, stride=S5_Q), :] = y[:, r * LANE:(r + 1) * LANE]


def s5_scan(u, weights, n_ctx, reverse):
    bsz, t, ch = u.shape
    ktoep, bw, cw, a2 = weights
    qk = S5_Q * LANE
    return pl.pallas_call(
        functools.partial(_s5_kernel, n_ctx_rows=n_ctx // S5_Q, reverse=reverse),
        grid=(S5_NT, bsz),
        in_specs=[pl.BlockSpec((None, t, LANE), lambda j, b: (b, 0, j)),
                  pl.BlockSpec((None, qk, qk), lambda j, b: (j, 0, 0)),
                  pl.BlockSpec((None, qk, 2 * S5_TS), lambda j, b: (j, 0, 0)),
                  pl.BlockSpec((None, 2 * S5_TS, qk), lambda j, b: (j, 0, 0)),
                  pl.BlockSpec((None, 16, 2 * S5_TS), lambda j, b: (j, 0, 0))],
        out_specs=pl.BlockSpec((None, t, LANE), lambda j, b: (b, 0, j)),
        out_shape=jax.ShapeDtypeStruct((bsz, t, ch), F32),
        compiler_params=_cparams(("parallel", "parallel")),
    )(u, ktoep, bw, cw, a2)


def _s5_out_kernel(yf_ref, yb_ref, u_ref, d_ref, w_ref, o_ref):
    y = yf_ref[...] + yb_ref[...] + d_ref[...] * u_ref[...]
    g = jax.nn.gelu(y)
    z = jnp.dot(g.astype(BF16), w_ref[...], preferred_element_type=F32)
    o_ref[...] = (g * jax.nn.sigmoid(z)).astype(o_ref.dtype)


def s5_out(yf, yb, u, d_skip, glu_w, tm=512):
    t, ch = u.shape
    row = pl.BlockSpec((tm, ch), lambda i: (i, 0))
    return pl.pallas_call(
        _s5_out_kernel,
        grid=(t // tm,),
        in_specs=[row, row, row, pl.BlockSpec((1, ch), lambda i: (0, 0)), pl.BlockSpec((ch, ch), lambda i: (0, 0))],
        out_specs=row,
        out_shape=jax.ShapeDtypeStruct((t, ch), BF16),
        compiler_params=_cparams(("parallel",)),
    )(yf, yb, u, d_skip.reshape(1, ch), glu_w.astype(BF16))


def _softplus(x):
    return jnp.maximum(x, 0.0) + jnp.log1p(jnp.exp(-jnp.abs(x)))


def _ssd_kernel(xf_ref, dtf_ref, xb_ref, dtb_ref, bias_ref, a_ref, spread_ref, tri_ref, yf_ref, yb_ref, st_ref, *,
                bsz):
    @pl.when(pl.program_id(0) == 0)
    def _():
        st_ref[...] = jnp.zeros_like(st_ref)

    srcs = ((xf_ref, dtf_ref, yf_ref), (xb_ref, dtb_ref, yb_ref))
    chains = [(d, b) for d in range(2) for b in range(bsz)]
    results = [_ssd_step(srcs[d][0].at[b], srcs[d][1].at[b], bias_ref, a_ref, spread_ref.at[d], tri_ref.at[d],
                         st_ref[d, b], d=d, reverse=d == 1) for d, b in chains]
    for (d, b), (y, st_new) in zip(chains, results):
        srcs[d][2][b] = y
        st_ref[d, b] = st_new


def _ssd_head_spread():
    m = np.zeros((2, LANE, MIX_WIDTH), np.float32)
    for d in range(2):
        for h in range(SSD_HEADS):
            m[d, d * SSD_HEADS + h, h * SSD_HEAD_DIM:(h + 1) * SSD_HEAD_DIM] = 1.0
    return m


def _ssd_step(xbc_ref, dt_ref, bias_ref, a_ref, spread_ref, tri_ref, st, *, d, reverse):
    q = SSD_CHUNK
    hp = SSD_HEAD_DIM
    hg = SSD_HEADS // SSD_GROUPS
    dt = _softplus(dt_ref[...] + bias_ref[...])
    adt = dt * a_ref[...]
    cs = _dot_sel(tri_ref[...], adt)
    cs_t = cs.T
    spread = spread_ref[...]
    dt_x = _dot_by_sel(dt, spread)
    cs_x = _dot_by_sel(cs, spread)
    tot_x = cs_x[0:1, :] if reverse else cs_x[q - 1:q, :]
    ecs_x = jnp.exp(cs_x)
    dec_x = jnp.exp(tot_x - cs_x)
    etot_x = jnp.exp(tot_x)
    xdt = xbc_ref[:, :MIX_WIDTH] * dt_x
    xw = (xdt * dec_x).astype(BF16)
    xdt = xdt.astype(BF16)
    bm = xbc_ref[:, MIX_WIDTH:MIX_WIDTH + SSD_GROUPS * SSD_STATE]
    cm = xbc_ref[:, MIX_WIDTH + SSD_GROUPS * SSD_STATE:]
    bm_t = bm.T
    li = lax.broadcasted_iota(jnp.int32, (q, q), 0)
    si = lax.broadcasted_iota(jnp.int32, (q, q), 1)
    mask = (si >= li) if reverse else (si <= li)
    first_of_pair = lax.broadcasted_iota(jnp.int32, (q, 2 * hp), 1) < hp
    gw = hg * hp
    ys, states = [], []
    for g in range(SSD_GROUPS):
        cg = cm[:, g * SSD_STATE:(g + 1) * SSD_STATE].astype(BF16)
        bg = bm[:, g * SSD_STATE:(g + 1) * SSD_STATE].astype(BF16)
        gmat = lax.dot_general(cg, bg, (((1,), (1,)), ((), ())), preferred_element_type=F32)
        yoff = jnp.dot(cg, st[g].astype(BF16), preferred_element_type=F32) * ecs_x[:, g * gw:(g + 1) * gw]
        for pr in range(hg // 2):
            lanes = slice(g * gw + pr * 2 * hp, g * gw + (pr + 1) * 2 * hp)
            xp = xdt[:, lanes]
            yd = []
            for hl in (2 * pr, 2 * pr + 1):
                col = d * SSD_HEADS + g * hg + hl
                seg = cs[:, col:col + 1] - cs_t[col:col + 1, :]
                lmat = jnp.exp(jnp.where(mask, seg, -jnp.inf))
                yd.append(jnp.dot((gmat * lmat).astype(BF16), xp, preferred_element_type=F32))
            ys.append(jnp.where(first_of_pair, yd[0], yd[1]) + yoff[:, pr * 2 * hp:(pr + 1) * 2 * hp])
        new = jnp.dot(bm_t[g * SSD_STATE:(g + 1) * SSD_STATE, :].astype(BF16), xw[:, g * gw:(g + 1) * gw],
                      preferred_element_type=F32)
        states.append(st[g] * etot_x[:, g * gw:(g + 1) * gw] + new)
    return jnp.concatenate(ys, axis=1), jnp.stack(states)


def ssd_scan(xbc, dt_raw, dt_bias_row, a_row, n_ctx):
    bsz, t, _ = xbc.shape
    q = SSD_CHUNK
    orders = _chunk_order(t // q, n_ctx // q)
    blk = lambda w, cidx: pl.BlockSpec((bsz, q, w), lambda i: (0, cidx(i), 0))
    vec = pl.BlockSpec((1, LANE), lambda i: (0, 0))
    return pl.pallas_call(
        functools.partial(_ssd_kernel, bsz=bsz),
        grid=(t // q,),
        in_specs=[blk(w, c) for c in orders for w in (SSD_CONV_CH, LANE)]
        + [vec, vec, pl.BlockSpec((2, LANE, MIX_WIDTH), lambda i: (0, 0, 0)),
           pl.BlockSpec((2, q, q), lambda i: (0, 0, 0))],
        out_specs=[blk(MIX_WIDTH, c) for c in orders],
        out_shape=[jax.ShapeDtypeStruct((bsz, t, MIX_WIDTH), F32)] * 2,
        scratch_shapes=[pltpu.VMEM((2, bsz, SSD_GROUPS, SSD_STATE, (SSD_HEADS // SSD_GROUPS) * SSD_HEAD_DIM), F32)],
        compiler_params=_cparams(("arbitrary",)),
        name="ssd_scan",
    )(xbc, dt_raw, xbc, dt_raw, dt_bias_row, a_row, jnp.asarray(_ssd_head_spread()), _cumsum_mats(q))


def _ssd_out_kernel(yf_ref, yb_ref, xbc_ref, z_ref, d_ref, nw_ref, o_ref):
    z = z_ref[...]
    y = (d_ref[...] * xbc_ref[...] + yf_ref[...] + yb_ref[...]) * (z * jax.nn.sigmoid(z))
    ms = jnp.mean(y * y, axis=-1, keepdims=True)
    o_ref[...] = (y * lax.rsqrt(ms + NORM_EPS) * nw_ref[...]).astype(o_ref.dtype)


def ssd_out(yf, yb, xbc, z, d_row, norm_w, tm=512):
    t, ch = z.shape
    row = pl.BlockSpec((tm, ch), lambda i: (i, 0))
    vec = pl.BlockSpec((1, ch), lambda i: (0, 0))
    return pl.pallas_call(
        _ssd_out_kernel,
        grid=(t // tm,),
        in_specs=[row, row, pl.BlockSpec((tm, ch), lambda i: (i, 0)), row, vec, vec],
        out_specs=row,
        out_shape=jax.ShapeDtypeStruct((t, ch), BF16),
        compiler_params=_cparams(("parallel",)),
    )(yf, yb, xbc, z, d_row, norm_w.reshape(1, ch))


def _gla_kernel(*refs, bsz):
    ins, (wg_ref, gb_ref, tri_ref), outs, st_ref = refs[:8], refs[8:11], refs[11:13], refs[13]

    @pl.when(pl.program_id(0) == 0)
    def _():
        st_ref[...] = jnp.zeros_like(st_ref)

    chains = [(d, b) for d in range(2) for b in range(bsz)]
    results = []
    for d, b in chains:
        q_ref, k_ref, v_ref, r_ref = ins[4 * d:4 * d + 4]
        results.append(_gla_chain(q_ref[b], k_ref[b], v_ref[b], r_ref[b], wg_ref[d], gb_ref[d], tri_ref[d],
                                  st_ref[d, b], reverse=d == 1))
    for (d, b), (o, st_new) in zip(chains, results):
        outs[d][b] = o
        st_ref[d, b] = st_new


def _gla_chain(q, k, v, r, wg, gb, tri, st, *, reverse):
    qs = GLA_CHUNK
    zl = _dot3(_split2(r), wg) + gb
    la = (jnp.minimum(zl, 0.0) - jnp.log1p(jnp.exp(-jnp.abs(zl)))) / GLA_TAU
    bc = _dot_sel(tri, la)
    tot = bc[0:1, :] if reverse else bc[qs - 1:qs, :]
    qd = (q * (GLA_DK ** -0.5) * jnp.exp(bc)).astype(BF16)
    ki = (k * jnp.exp(-bc)).astype(BF16)
    ke = (k * jnp.exp(tot - bc)).astype(BF16)
    gam = jnp.exp(tot)
    ti = lax.broadcasted_iota(jnp.int32, (qs, qs), 0)
    si = lax.broadcasted_iota(jnp.int32, (qs, qs), 1)
    mask = (si >= ti) if reverse else (si <= ti)
    pair_w = 2 * GLA_DK
    first_q = lax.broadcasted_iota(jnp.int32, (qs, pair_w), 1) < GLA_DK
    first_s = lax.broadcasted_iota(jnp.int32, (GLA_DV, pair_w), 1) < GLA_DK
    nt_dims = (((1,), (1,)), ((), ()))
    outs, states = [], []
    for pr in range(GLA_HEADS // 2):
        lanes = slice(pr * pair_w, (pr + 1) * pair_w)
        qd_p, ki_p, ke_p, st_p = qd[:, lanes], ki[:, lanes], ke[:, lanes], st[:, lanes]
        st_b = st_p.astype(BF16)
        ds = []
        for j in range(2):
            h = 2 * pr + j
            vh = v[:, h * GLA_DV:(h + 1) * GLA_DV].astype(BF16)
            qm = jnp.where(first_q if j == 0 else jnp.logical_not(first_q), qd_p, jnp.zeros_like(qd_p))
            sc = lax.dot_general(qm, ki_p, nt_dims, preferred_element_type=F32)
            sc = jnp.where(mask, sc, 0.0).astype(BF16)
            o_intra = jnp.dot(sc, vh, preferred_element_type=F32)
            o_inter = lax.dot_general(qm, st_b, nt_dims, preferred_element_type=F32)
            outs.append(o_intra + o_inter)
            ds.append(lax.dot_general(vh, ke_p, (((0,), (0,)), ((), ())), preferred_element_type=F32))
        states.append(st_p * gam[:, lanes] + jnp.where(first_s, ds[0], ds[1]))
    return jnp.concatenate(outs, axis=1), jnp.concatenate(states, axis=1)


def _chunk_order(nchunk, nctx):
    fwd = lambda i: i
    bwd = lambda i: jnp.where(i < nctx, nctx - 1 - i, nchunk + nctx - 1 - i)
    return fwd, bwd


def _cumsum_mats(q):
    tri = np.tril(np.ones((q, q), np.float32))
    return jnp.asarray(np.stack([tri, tri.T]))


def gla_scan(q, k, v, r, wg, gb, n_ctx):
    bsz, t, _ = q.shape
    qs = GLA_CHUNK
    orders = _chunk_order(t // qs, n_ctx // qs)
    blk = lambda w, cidx: pl.BlockSpec((bsz, qs, w), lambda i: (0, cidx(i), 0))
    widths = (GLA_KEY_WIDTH, GLA_KEY_WIDTH, MIX_WIDTH, LANE)
    return pl.pallas_call(
        functools.partial(_gla_kernel, bsz=bsz),
        grid=(t // qs,),
        in_specs=[blk(w, c) for c in orders for w in widths]
        + [pl.BlockSpec((2, LANE, GLA_KEY_WIDTH), lambda i: (0, 0, 0)),
           pl.BlockSpec((2, 1, GLA_KEY_WIDTH), lambda i: (0, 0, 0)),
           pl.BlockSpec((2, qs, qs), lambda i: (0, 0, 0))],
        out_specs=[blk(MIX_WIDTH, c) for c in orders],
        out_shape=[jax.ShapeDtypeStruct((bsz, t, MIX_WIDTH), F32)] * 2,
        scratch_shapes=[pltpu.VMEM((2, bsz, GLA_DV, GLA_KEY_WIDTH), F32)],
        compiler_params=_cparams(("arbitrary",)),
        name="gla_scan",
    )(q, k, v, r, q, k, v, r, wg, gb, _cumsum_mats(qs))


def _gla_out_kernel(of_ref, ob_ref, g_ref, nw_ref, o_ref):
    g = g_ref[...]
    sg = g * jax.nn.sigmoid(g)
    for h in range(GLA_HEADS):
        vs = slice(h * GLA_DV, (h + 1) * GLA_DV)
        o = of_ref[:, vs] + ob_ref[:, vs]
        ms = jnp.mean(o * o, axis=-1, keepdims=True)
        o_ref[:, vs] = (o * lax.rsqrt(ms + NORM_EPS) * nw_ref[...] * sg[:, vs]).astype(o_ref.dtype)


def gla_out(of, ob, g, norm_w, tm=512):
    t, ch = g.shape
    row = pl.BlockSpec((tm, ch), lambda i: (i, 0))
    return pl.pallas_call(
        _gla_out_kernel,
        grid=(t // tm,),
        in_specs=[row, row, row, pl.BlockSpec((1, GLA_DV), lambda i: (0, 0))],
        out_specs=row,
        out_shape=jax.ShapeDtypeStruct((t, ch), BF16),
        compiler_params=_cparams(("parallel",)),
    )(of, ob, g, norm_w.reshape(1, GLA_DV))


def _hy_emb(n):
    t = np.linspace(0.0, 1.0, n)[:, None]
    freqs = np.linspace(1e-4, HY_BANDS - 1, HY_BANDS)
    ang = (2.0 * math.pi / n) * np.arange(n)[:, None] * freqs[None, :]
    emb = np.concatenate([t, np.cos(ang), -np.sin(ang)], axis=-1)
    return np.pad(emb, ((0, 0), (0, LANE - HY_EMB))).astype(np.float32)


def _hy_filter_kernel(emb_ref, w1_ref, b1_ref, f1_ref, w2_ref, b2_ref, f2_ref, w3_ref, dl_ref, o_ref):
    emb = emb_ref[...]
    h = jnp.sin(f1_ref[...] * (jnp.dot(emb, w1_ref[...], preferred_element_type=F32, precision=HI) + b1_ref[...]))
    h = jnp.sin(f2_ref[...] * (jnp.dot(h, w2_ref[...], preferred_element_type=F32, precision=HI) + b2_ref[...]))
    h = jnp.dot(h, w3_ref[...], preferred_element_type=F32, precision=HI)
    o_ref[...] = h * jnp.exp(-emb[:, 0:1] * dl_ref[...])


def hyena_filters(n, w1, b1, f1, w2, b2, f2, w3, tm=256):
    nf = w3.shape[1]
    emb = jnp.asarray(_hy_emb(n))
    deltas = np.abs(np.linspace(HY_MIN_DECAY, HY_MAX_DECAY, MIX_WIDTH)).astype(np.float32)
    dl = jnp.asarray(np.tile(deltas, nf // MIX_WIDTH)[None, :])
    w1p = jnp.pad(w1, ((0, LANE - HY_EMB), (0, 0)))
    fd = HY_FILTER_DIM
    vec = lambda w: pl.BlockSpec((1, w), lambda i: (0, 0))
    return pl.pallas_call(
        _hy_filter_kernel,
        grid=(n // tm,),
        in_specs=[pl.BlockSpec((tm, LANE), lambda i: (i, 0)),
                  pl.BlockSpec((LANE, fd), lambda i: (0, 0)), vec(fd), vec(fd),
                  pl.BlockSpec((fd, fd), lambda i: (0, 0)), vec(fd), vec(fd),
                  pl.BlockSpec((fd, nf), lambda i: (0, 0)), vec(nf)],
        out_specs=pl.BlockSpec((tm, nf), lambda i: (i, 0)),
        out_shape=jax.ShapeDtypeStruct((n, nf), F32),
        compiler_params=_cparams(("parallel",)),
    )(emb, w1p, b1.reshape(1, fd), f1.reshape(1, fd), w2, b2.reshape(1, fd), f2.reshape(1, fd), w3, dl)


class _HyDft:
    def __init__(self, n):
        big = 2 * n
        self.n1 = big // FFT_N2
        self.rows = n // FFT_N2
        self.kh = self.n1 // 2 + 1
        self.khp = -(-self.kh // 8) * 8
        k1 = np.arange(self.kh)
        ang = 2.0 * np.pi * np.outer(k1, np.arange(self.rows)) / self.n1
        fwd = np.zeros((2 * self.khp, self.rows))
        fwd[:self.kh] = np.cos(ang)
        fwd[self.khp:self.khp + self.kh] = -np.sin(ang)
        wgt = np.full(self.kh, 2.0)
        wgt[0] = wgt[-1] = 1.0
        inv = np.zeros((self.rows, 2 * self.khp))
        inv[:, :self.kh] = (wgt[:, None] * np.cos(ang)).T / big
        inv[:, self.khp:self.khp + self.kh] = -(wgt[:, None] * np.sin(ang)).T / big
        tw = 2.0 * np.pi * np.outer(k1, np.arange(FFT_N2)) / big
        f32 = lambda a: np.ascontiguousarray(a, dtype=np.float32)
        self.fwd, self.inv = f32(fwd), f32(inv)
        self.tw_re, self.tw_im = f32(np.cos(tw)[:, :, None]), f32(-np.sin(tw)[:, :, None])
        self.small = self.rows < 8


def _inner_dft_mats():
    k2 = np.arange(FFT_N2)
    f2 = np.exp(-2j * np.pi * np.outer(k2, k2) / FFT_N2)
    w2 = np.block([[f2.real, -f2.imag], [f2.imag, f2.real]])
    w2c = np.block([[f2.real, f2.imag], [-f2.imag, f2.real]])
    return w2.astype(np.float32), w2c.astype(np.float32)


_W2, _W2C = _inner_dft_mats()
HY_UNROLL = 4


def _sc_load(ref, idx):
    return jnp.concatenate([ref[w, idx, :] for w in range(ref.shape[0])], axis=1)


def _sc_store(ref, idx, val):
    for w in range(ref.shape[0]):
        ref[w, idx, :] = val[:, w * LANE:(w + 1) * LANE]


def _hy_outer_fwd(load_rows, cst, fwd_ref, xr_ref, xi_ref):
    if cst.small:
        blocks = [load_rows(n1 * FFT_N2, FFT_N2, None) for n1 in range(cst.rows)]
        for k1 in range(cst.kh):
            xr = sum(float(cst.fwd[k1, n1]) * blocks[n1] for n1 in range(cst.rows))
            xi = sum(float(cst.fwd[cst.khp + k1, n1]) * blocks[n1] for n1 in range(cst.rows))
            _sc_store(xr_ref, pl.ds(k1 * FFT_N2, FFT_N2), xr)
            _sc_store(xi_ref, pl.ds(k1 * FFT_N2, FFT_N2), xi)
        return

    fwd2 = _split2(fwd_ref[...])

    def body(n2, c):
        res = _dot3(fwd2, load_rows(n2, cst.rows, FFT_N2))
        _sc_store(xr_ref, pl.ds(n2, cst.khp, stride=FFT_N2), res[:cst.khp])
        _sc_store(xi_ref, pl.ds(n2, cst.khp, stride=FFT_N2), res[cst.khp:])
        return c
    lax.fori_loop(0, FFT_N2, body, 0, unroll=HY_UNROLL)


def _hy_inner_fwd(k1, xr_ref, xi_ref, twr_ref, twi_ref, w2_ref):
    start = k1 * FFT_N2
    rows = pl.ds(start if isinstance(k1, int) else pl.multiple_of(start, FFT_N2), FFT_N2)
    xr, xi = _sc_load(xr_ref, rows), _sc_load(xi_ref, rows)
    twr, twi = twr_ref[k1], twi_ref[k1]
    ar = xr * twr - xi * twi
    ai = xr * twi + xi * twr
    ys = jnp.dot(w2_ref[...], jnp.concatenate([ar, ai], axis=0).astype(BF16), preferred_element_type=F32)
    return rows, twr, twi, ys[:FFT_N2], ys[FFT_N2:]


def _hy_for_k1(cst, body):
    if cst.small:
        for k1 in range(cst.kh):
            body(k1, 0)
    else:
        lax.fori_loop(0, cst.kh, body, 0, unroll=3 if cst.kh % 3 == 0 else 1)


def _hy_spectrum_part(cst, hf_ref, hb_ref, fwd_ref, twr_ref, twi_ref, w2_ref, hr_ref, hi_ref, xr_ref, xi_ref):
    width = hf_ref.shape[1]

    def load_rows(start, count, stride):
        idx = pl.ds(start, count) if stride is None else pl.ds(start, count, stride=stride)
        return jnp.concatenate([hf_ref[idx, :], hb_ref[idx, :]], axis=1)
    _hy_outer_fwd(load_rows, cst, fwd_ref, xr_ref, xi_ref)
    hb0 = hb_ref[0:1, :]

    def body(k1, c):
        rows, _, _, yr, yi = _hy_inner_fwd(k1, xr_ref, xi_ref, twr_ref, twi_ref, w2_ref)
        hr_ref[rows, :] = yr[:, :width] + yr[:, width:] - hb0
        hi_ref[rows, :] = yi[:, :width] - yi[:, width:]
        return c
    _hy_for_k1(cst, body)


def _hy_spectrum_kernel(*refs, parts):
    w2_ref = refs[0]
    pos = 1
    n_in = 5 * len(parts)
    outs = refs[1 + n_in:1 + n_in + 2 * len(parts)]
    xr_ref, xi_ref = refs[-2:]
    for p, cst in enumerate(parts):
        hf_ref, hb_ref, fwd_ref, twr_ref, twi_ref = refs[pos:pos + 5]
        pos += 5
        _hy_spectrum_part(cst, hf_ref, hb_ref, fwd_ref, twr_ref, twi_ref, w2_ref, outs[2 * p], outs[2 * p + 1],
                          xr_ref, xi_ref)


def hyena_spectrum(filts):
    parts = tuple(_HyDft(f.shape[0]) for f in filts)
    nt = MIX_WIDTH // LANE
    width = HY_ORDER * MIX_WIDTH
    args = [jnp.asarray(_W2).astype(BF16)]
    in_specs = [pl.BlockSpec((2 * FFT_N2, 2 * FFT_N2), lambda t: (0, 0))]
    out_specs, out_shape = [], []
    for f, cst in zip(filts, parts):
        n = f.shape[0]
        args += [f, f, jnp.asarray(cst.fwd), jnp.asarray(cst.tw_re), jnp.asarray(cst.tw_im)]
        in_specs += [pl.BlockSpec((n, LANE), lambda t: (0, (t // nt) * 2 * nt + t % nt)),
                     pl.BlockSpec((n, LANE), lambda t: (0, (t // nt) * 2 * nt + nt + t % nt)),
                     pl.BlockSpec(cst.fwd.shape, lambda t: (0, 0)),
                     pl.BlockSpec(cst.tw_re.shape, lambda t: (0, 0, 0)),
                     pl.BlockSpec(cst.tw_im.shape, lambda t: (0, 0, 0))]
        out_specs += [pl.BlockSpec((cst.kh * FFT_N2, LANE), lambda t: (0, t))] * 2
        out_shape += [jax.ShapeDtypeStruct((cst.kh * FFT_N2, width), F32)] * 2
    rows = max(c.khp for c in parts) * FFT_N2
    outs = pl.pallas_call(
        functools.partial(_hy_spectrum_kernel, parts=parts),
        grid=(width // LANE,),
        in_specs=in_specs, out_specs=out_specs, out_shape=out_shape,
        scratch_shapes=[pltpu.VMEM((2, rows, LANE), F32)] * 2,
        compiler_params=_cparams(("parallel",)),
        name="hyena_spectrum",
    )(*args)
    return [(outs[2 * p], outs[2 * p + 1]) for p in range(len(parts))]


def _hy_conv_part(cst, row0, u_ref, g_ref, bias_ref, hr_ref, hi_ref, fwd_ref, inv_ref, twr_ref, twi_ref,
                  w2_ref, w2c_ref, o_ref, xr_ref, xi_ref):
    bsz = u_ref.shape[0]

    def load_rows(start, count, stride):
        idx = pl.ds(row0 + start, count) if stride is None else pl.ds(row0 + start, count, stride=stride)
        return jnp.concatenate([u_ref[b, idx, :] for b in range(bsz)], axis=1)
    _hy_outer_fwd(load_rows, cst, fwd_ref, xr_ref, xi_ref)

    def body(k1, c):
        rows, twr, twi, yr, yi = _hy_inner_fwd(k1, xr_ref, xi_ref, twr_ref, twi_ref, w2_ref)
        hr = jnp.concatenate([hr_ref[rows, :]] * bsz, axis=1)
        hi = jnp.concatenate([hi_ref[rows, :]] * bsz, axis=1)
        zr = yr * hr - yi * hi
        zi = yr * hi + yi * hr
        vs = jnp.dot(w2c_ref[...], jnp.concatenate([zr, zi], axis=0).astype(BF16), preferred_element_type=F32)
        vr, vi = vs[:FFT_N2], vs[FFT_N2:]
        _sc_store(xr_ref, rows, vr * twr + vi * twi)
        _sc_store(xi_ref, rows, vi * twr - vr * twi)
        return c
    _hy_for_k1(cst, body)

    bias = bias_ref[...]

    def emit(idx, y):
        for b in range(bsz):
            yb = y[:, b * LANE:(b + 1) * LANE]
            o_ref[b, idx, :] = (g_ref[b, idx, :] * (yb + bias * u_ref[b, idx, :])).astype(o_ref.dtype)

    if cst.small:
        for n1 in range(cst.rows):
            y = sum(float(cst.inv[n1, k1]) * _sc_load(xr_ref, pl.ds(k1 * FFT_N2, FFT_N2))
                    + float(cst.inv[n1, cst.khp + k1]) * _sc_load(xi_ref, pl.ds(k1 * FFT_N2, FFT_N2))
                    for k1 in range(cst.kh))
            emit(pl.ds(row0 + n1 * FFT_N2, FFT_N2), y)
        return

    inv2 = _split2(inv_ref[...])

    def out_body(n2, c):
        z = jnp.concatenate([_sc_load(xr_ref, pl.ds(n2, cst.khp, stride=FFT_N2)),
                             _sc_load(xi_ref, pl.ds(n2, cst.khp, stride=FFT_N2))], axis=0)
        y = _dot3(inv2, z)
        emit(pl.ds(row0 + n2, cst.rows, stride=FFT_N2), y)
        return c
    lax.fori_loop(0, FFT_N2, out_body, 0, unroll=HY_UNROLL)


def _hy_conv_kernel(*refs, parts, row0s, zero_rows):
    u_ref, g_ref, bias_ref, w2_ref, w2c_ref = refs[:5]
    o_ref, xr_ref, xi_ref = refs[-3:]
    pos = 5
    for cst, row0 in zip(parts, row0s):
        hr_ref, hi_ref, fwd_ref, inv_ref, twr_ref, twi_ref = refs[pos:pos + 6]
        pos += 6
        _hy_conv_part(cst, row0, u_ref, g_ref, bias_ref, hr_ref, hi_ref, fwd_ref, inv_ref, twr_ref, twi_ref,
                      w2_ref, w2c_ref, o_ref, xr_ref, xi_ref)
    if zero_rows is not None:
        lo, hi = zero_rows
        o_ref[:, lo:hi, :] = jnp.zeros((o_ref.shape[0], hi - lo, o_ref.shape[2]), o_ref.dtype)


def hyena_conv(u, u_blk, gate, g_blk, bias, order, spectra, segs, zero_rows):
    out_dtype = F32
    bsz, t, _ = u.shape
    nt = MIX_WIDTH // LANE
    parts = tuple(_HyDft(n) for _, n in segs)
    args = [u, gate, bias, jnp.asarray(_W2).astype(BF16), jnp.asarray(_W2C).astype(BF16)]
    mat = pl.BlockSpec((2 * FFT_N2, 2 * FFT_N2), lambda j: (0, 0))
    in_specs = [pl.BlockSpec((bsz, t, LANE), lambda j: (0, 0, u_blk + j)),
                pl.BlockSpec((bsz, t, LANE), lambda j: (0, 0, g_blk + j)),
                pl.BlockSpec((1, LANE), lambda j: (0, j)), mat, mat]
    for (hr, hi), cst in zip(spectra, parts):
        args += [hr, hi, jnp.asarray(cst.fwd), jnp.asarray(cst.inv), jnp.asarray(cst.tw_re), jnp.asarray(cst.tw_im)]
        hspec = pl.BlockSpec((cst.kh * FFT_N2, LANE), lambda j: (0, order * nt + j))
        in_specs += [hspec, hspec, pl.BlockSpec(cst.fwd.shape, lambda j: (0, 0)),
                     pl.BlockSpec(cst.inv.shape, lambda j: (0, 0)),
                     pl.BlockSpec(cst.tw_re.shape, lambda j: (0, 0, 0)),
                     pl.BlockSpec(cst.tw_im.shape, lambda j: (0, 0, 0))]
    rows = max(c.khp for c in parts) * FFT_N2
    return pl.pallas_call(
        functools.partial(_hy_conv_kernel, parts=parts, row0s=tuple(r for r, _ in segs), zero_rows=zero_rows),
        grid=(nt,),
        in_specs=in_specs,
        out_specs=pl.BlockSpec((bsz, t, LANE), lambda j: (0, 0, j)),
        out_shape=jax.ShapeDtypeStruct((bsz, t, MIX_WIDTH), out_dtype),
        scratch_shapes=[pltpu.VMEM((bsz, rows, LANE), F32)] * 2,
        compiler_params=_cparams(("parallel",)),
        name="hyena_conv",
    )(*args)


def _merge_kernel(ya_ref, yb_ref, yc_ref, yd_ref, ga_ref, gb_ref, gc_ref, gd_ref, w_ref, o_ref):
    acc = None
    for b, (y_ref, g_ref) in enumerate(((ya_ref, ga_ref), (yb_ref, gb_ref), (yc_ref, gc_ref), (yd_ref, gd_ref))):
        t = jax.nn.sigmoid(g_ref[...].astype(F32)) * jnp.dot(y_ref[...].astype(BF16), w_ref[b], preferred_element_type=F32)
        acc = t if acc is None else acc + t
    o_ref[...] = acc.astype(o_ref.dtype)


def merge_branches(ys, gates, w_branch, tm=512, tn=512):
    t, ch = ys[0].shape
    nj = D_MODEL // tn
    yspec = pl.BlockSpec((tm, ch), lambda j, i: (i, 0))
    gspecs = [pl.BlockSpec((tm, tn), functools.partial(lambda j, i, b: (i, b * nj + j), b=b)) for b in range(N_BRANCH)]
    return pl.pallas_call(
        _merge_kernel,
        grid=(nj, t // tm),
        in_specs=[yspec] * 4 + gspecs + [pl.BlockSpec((N_BRANCH, ch, tn), lambda j, i: (0, 0, j))],
        out_specs=pl.BlockSpec((tm, tn), lambda j, i: (i, j)),
        out_shape=jax.ShapeDtypeStruct((t, D_MODEL), BF16),
        compiler_params=_cparams(("parallel", "parallel")),
    )(*ys, gates, gates, gates, gates, w_branch)


def _mm_resid_kernel(a_ref, w_ref, x_ref, g0_ref, g1_ref, o_ref):
    y = jnp.dot(a_ref[...], w_ref[...], preferred_element_type=F32)
    o_ref[:ROW_TILE] = x_ref[:ROW_TILE] + g0_ref[...] * y[:ROW_TILE]
    o_ref[ROW_TILE:] = x_ref[ROW_TILE:] + g1_ref[...] * y[ROW_TILE:]


def matmul_gated_residual(a, w, x, gate, tiles_per_batch, tn=1024):
    t, k = a.shape
    n = w.shape[1]
    tm = 2 * ROW_TILE
    gspec = lambda h: pl.BlockSpec((None, 1, tn), lambda j, i: (_seg_index(2 * i + h, tiles_per_batch), 0, j))
    return pl.pallas_call(
        _mm_resid_kernel,
        grid=(n // tn, t // tm),
        in_specs=[pl.BlockSpec((tm, k), lambda j, i: (i, 0)),
                  pl.BlockSpec((k, tn), lambda j, i: (0, j)),
                  pl.BlockSpec((tm, tn), lambda j, i: (i, j)),
                  gspec(0), gspec(1)],
        out_specs=pl.BlockSpec((tm, tn), lambda j, i: (i, j)),
        out_shape=jax.ShapeDtypeStruct((t, n), F32),
        compiler_params=_cparams(("parallel", "parallel")),
    )(a, w, x, gate, gate)


ROUTE_OFF = MOE_GROUPS
TOK_S = D_MODEL // LANE
DSP_S = TOK_S + 8
MOE_TM = 512


def _store_token_major(ref, val, slab=TOK_S):
    rows = val.shape[0]
    for s in range(TOK_S):
        ref[pl.ds(s, rows, stride=slab), :] = val[:, s * LANE:(s + 1) * LANE]


def _load_token_major(ref, rows, slab=TOK_S):
    return jnp.concatenate([ref[pl.ds(s, rows, stride=slab), :] for s in range(TOK_S)], axis=1)


def _router_kernel(x_ref, sh_ref, sc_ref, wr_ref, br_ref, h_ref, comb_ref):
    x = x_ref[...]
    ms = jnp.mean(x * x, axis=-1, keepdims=True)
    h = x * lax.rsqrt(ms + NORM_EPS) * (1.0 + sc_ref[...]) + sh_ref[...]
    _store_token_major(h_ref, h, DSP_S)
    logits = _dot3(_split2(h), wr_ref[...]) + br_ref[...]
    lane = lax.broadcasted_iota(jnp.int32, logits.shape, 1).astype(F32)
    neg = -jnp.inf
    gmask = lane < MOE_GROUPS
    glog = jnp.where(gmask, logits, neg)
    gmax = jnp.max(glog, axis=-1, keepdims=True)
    g_idx = jnp.min(jnp.where(glog == gmax, lane, float(LANE)), axis=-1, keepdims=True)
    p_g = 1.0 / jnp.sum(jnp.exp(glog - gmax), axis=-1, keepdims=True)
    e_lo = ROUTE_OFF + g_idx * MOE_PER_GROUP
    emask = (lane >= e_lo) & (lane < e_lo + MOE_PER_GROUP)
    v1 = jnp.max(jnp.where(emask, logits, neg), axis=-1, keepdims=True)
    i1 = jnp.min(jnp.where(emask & (logits == v1), lane, float(LANE)), axis=-1, keepdims=True)
    emask2 = emask & (lane != i1)
    v2 = jnp.max(jnp.where(emask2, logits, neg), axis=-1, keepdims=True)
    i2 = jnp.min(jnp.where(emask2 & (logits == v2), lane, float(LANE)), axis=-1, keepdims=True)
    e21 = jnp.exp(v2 - v1)
    w1 = p_g / (1.0 + e21)
    w2 = p_g * e21 / (1.0 + e21)
    comb = jnp.where(lane == i1, w1, jnp.where(lane == i2, w2, 0.0))
    route = jnp.where(lane == 0, g_idx, comb)
    comb_ref[...] = route
    rows = x.shape[0]
    h_ref[pl.ds(TOK_S, rows, stride=DSP_S), :] = route
    for s in range(TOK_S + 1, DSP_S):
        h_ref[pl.ds(s, rows, stride=DSP_S), :] = jnp.zeros_like(route)


def moe_router(x, shift, scale, group_w, group_b, expert_w, expert_b, tiles_per_batch):
    t, d = x.shape
    wr = _pad_cols(jnp.concatenate([group_w, expert_w], axis=1), LANE)
    br = _pad_cols(jnp.concatenate([group_b, expert_b])[None, :], LANE)
    seg = lambda i: (_seg_index(i, tiles_per_batch), 0, 0)
    return pl.pallas_call(
        _router_kernel,
        grid=(t // ROW_TILE,),
        in_specs=[pl.BlockSpec((ROW_TILE, d), lambda i: (i, 0)),
                  pl.BlockSpec((None, 1, d), seg), pl.BlockSpec((None, 1, d), seg),
                  pl.BlockSpec((d, LANE), lambda i: (0, 0)), pl.BlockSpec((1, LANE), lambda i: (0, 0))],
        out_specs=[pl.BlockSpec((ROW_TILE * DSP_S, LANE), lambda i: (i, 0)),
                   pl.BlockSpec((ROW_TILE, LANE), lambda i: (i, 0))],
        out_shape=[jax.ShapeDtypeStruct((t * DSP_S, LANE), F32), jax.ShapeDtypeStruct((t, LANE), F32)],
        compiler_params=_cparams(("parallel",)),
    )(x, shift, scale, wr, br)


def moe_plan(route, n_tiles):
    g = route[:, 0].astype(jnp.int32)
    oh = (g[:, None] == jnp.arange(MOE_GROUPS, dtype=jnp.int32)[None, :]).astype(jnp.int32)
    cnt = jnp.sum(oh, axis=0)
    rank = jnp.sum((jnp.cumsum(oh, axis=0) - oh) * oh, axis=1)
    ptiles = (cnt + MOE_TM - 1) // MOE_TM
    ends = jnp.cumsum(ptiles)
    off = (ends - ptiles) * MOE_TM
    pos = jnp.sum(oh * off[None, :], axis=1) + rank
    n_act = ends[-1]
    tid = jnp.arange(n_tiles, dtype=jnp.int32)
    tsrc = jnp.minimum(tid, n_act - 1)
    tgrp = jnp.sum((tsrc[:, None] >= ends[None, :]).astype(jnp.int32), axis=1)
    tact = (tid < n_act).astype(jnp.int32)
    return pos, tsrc, tgrp, tact


def _slab(ref, row, size):
    return ref.at[pl.ds(pl.multiple_of(row * size, size), size)]


def _moe_scatter_kernel(pos_ref, src_ref, init_ref, dst_ref, sem):
    del init_ref
    base = pl.program_id(0) * ROW_TILE

    def issue(r, c):
        pltpu.make_async_copy(_slab(src_ref, r, DSP_S), _slab(dst_ref, pos_ref[base + r], DSP_S), sem).start()
        return c
    lax.fori_loop(0, ROW_TILE, issue, 0)
    pltpu.make_async_copy(src_ref, dst_ref.at[pl.ds(0, ROW_TILE * DSP_S)], sem).wait()


def moe_scatter(slabs, pos, n_slots):
    t = pos.shape[0]
    init = jnp.zeros((n_slots * DSP_S, LANE), F32)
    return pl.pallas_call(
        _moe_scatter_kernel,
        grid_spec=pltpu.PrefetchScalarGridSpec(
            num_scalar_prefetch=1, grid=(t // ROW_TILE,),
            in_specs=[pl.BlockSpec((ROW_TILE * DSP_S, LANE), lambda i, p: (i, 0)),
                      pl.BlockSpec(memory_space=pl.ANY)],
            out_specs=pl.BlockSpec(memory_space=pl.ANY),
            scratch_shapes=[pltpu.SemaphoreType.DMA(())]),
        out_shape=jax.ShapeDtypeStruct(init.shape, F32),
        input_output_aliases={2: 0},
        compiler_params=_cparams(("arbitrary",)),
    )(pos, slabs, init)


def _moe_expert_kernel(tsrc_ref, tgrp_ref, tact_ref, xs_ref, wg_ref, wu_ref, wd_ref, ys_ref, x_ref, acc_ref):
    i, e, f = pl.program_id(0), pl.program_id(1), pl.program_id(2)
    active = tact_ref[i] == 1
    first = (e == 0) & (f == 0)
    last = (e == pl.num_programs(1) - 1) & (f == pl.num_programs(2) - 1)

    @pl.when(active & first)
    def _():
        x_ref[...] = _load_token_major(xs_ref, MOE_TM, DSP_S).astype(BF16)
        acc_ref[...] = jnp.zeros_like(acc_ref)

    @pl.when(active)
    def _():
        x = x_ref[...]
        a = jnp.dot(x, wg_ref[...].astype(BF16), preferred_element_type=F32)
        u = jnp.dot(x, wu_ref[...].astype(BF16), preferred_element_type=F32)
        act = (a * jax.nn.sigmoid(a) * u).astype(BF16)
        route = xs_ref[pl.ds(TOK_S, MOE_TM, stride=DSP_S), :]
        lane = lax.broadcasted_iota(jnp.int32, route.shape, 1)
        mine = lane == ROUTE_OFF + tgrp_ref[i] * MOE_PER_GROUP + e
        w = jnp.sum(jnp.where(mine, route, 0.0), axis=-1, keepdims=True)
        acc_ref[...] += w * jnp.dot(act, wd_ref[...].astype(BF16), preferred_element_type=F32)

    @pl.when(active & last)
    def _():
        _store_token_major(ys_ref, acc_ref[...])

    @pl.when(jnp.logical_not(active) & first)
    def _():
        ys_ref[...] = jnp.zeros_like(ys_ref)


def moe_experts(xs, tsrc, tgrp, tact, w_gate, w_up, w_down, li, tf=256):
    n_tiles = tsrc.shape[0]
    _, _, d, ff = w_gate.shape
    wsel = lambda i, e, f, ts, tg, ta: (li, tg[i] * MOE_PER_GROUP + e, 0, f)
    rows = lambda i, e, f, ts, tg, ta: (ts[i], 0)
    return pl.pallas_call(
        _moe_expert_kernel,
        grid_spec=pltpu.PrefetchScalarGridSpec(
            num_scalar_prefetch=3, grid=(n_tiles, MOE_PER_GROUP, ff // tf),
            in_specs=[pl.BlockSpec((MOE_TM * DSP_S, LANE), rows),
                      pl.BlockSpec((None, None, d, tf), wsel),
                      pl.BlockSpec((None, None, d, tf), wsel),
                      pl.BlockSpec((None, None, tf, d),
                                   lambda i, e, f, ts, tg, ta: (li, tg[i] * MOE_PER_GROUP + e, f, 0))],
            out_specs=pl.BlockSpec((MOE_TM * TOK_S, LANE), lambda i, e, f, ts, tg, ta: (i, 0)),
            scratch_shapes=[pltpu.VMEM((MOE_TM, d), BF16), pltpu.VMEM((MOE_TM, d), F32)]),
        out_shape=jax.ShapeDtypeStruct((n_tiles * MOE_TM * TOK_S, LANE), F32),
        compiler_params=_cparams(("arbitrary", "arbitrary", "arbitrary")),
    )(tsrc, tgrp, tact, xs, w_gate, w_up, w_down)


def _moe_gather_kernel(pos_ref, ys_ref, x_ref, g_ref, o_ref, buf_ref, sem):
    base = pl.program_id(0) * ROW_TILE

    def issue(r, c):
        pltpu.make_async_copy(_slab(ys_ref, pos_ref[base + r], TOK_S), _slab(buf_ref, r, TOK_S), sem).start()
        return c
    lax.fori_loop(0, ROW_TILE, issue, 0)
    pltpu.make_async_copy(ys_ref.at[pl.ds(0, ROW_TILE * TOK_S)], buf_ref, sem).wait()
    o_ref[...] = x_ref[...] + g_ref[...] * _load_token_major(buf_ref, ROW_TILE)


def moe_gather_residual(ys, pos, x, gate, tiles_per_batch):
    t, d = x.shape
    return pl.pallas_call(
        _moe_gather_kernel,
        grid_spec=pltpu.PrefetchScalarGridSpec(
            num_scalar_prefetch=1, grid=(t // ROW_TILE,),
            in_specs=[pl.BlockSpec(memory_space=pl.ANY),
                      pl.BlockSpec((ROW_TILE, d), lambda i, p: (i, 0)),
                      pl.BlockSpec((None, 1, d), lambda i, p: (_seg_index(i, tiles_per_batch), 0, 0))],
            out_specs=pl.BlockSpec((ROW_TILE, d), lambda i, p: (i, 0)),
            scratch_shapes=[pltpu.VMEM((ROW_TILE * TOK_S, LANE), F32), pltpu.SemaphoreType.DMA(())]),
        out_shape=jax.ShapeDtypeStruct((t, d), F32),
        compiler_params=_cparams(("arbitrary",)),
    )(pos, ys, x, gate)


def _final_norm_kernel(x_ref, w_ref, o_ref):
    x = x_ref[...]
    ms = jnp.mean(x * x, axis=-1, keepdims=True)
    o_ref[...] = x * lax.rsqrt(ms + NORM_EPS) * w_ref[...]


def final_norm(x, w, n_ctx):
    bsz, t, d = x.shape
    off = n_ctx // ROW_TILE
    return pl.pallas_call(
        _final_norm_kernel,
        grid=(bsz, (t - n_ctx) // ROW_TILE),
        in_specs=[pl.BlockSpec((None, ROW_TILE, d), lambda b, i: (b, i + off, 0)),
                  pl.BlockSpec((1, d), lambda b, i: (0, 0))],
        out_specs=pl.BlockSpec((None, ROW_TILE, d), lambda b, i: (b, i, 0)),
        out_shape=jax.ShapeDtypeStruct((bsz, t - n_ctx, d), F32),
        compiler_params=_cparams(("parallel", "parallel")),
    )(x, w.reshape(1, d))


def _lat_colmajor(t, n_ctx, inverse=False):
    bsz, _, ch = t.shape
    lat = t[:, n_ctx:]
    rows = lat.shape[1] // GRID_W
    shp = (bsz, GRID_W, rows, ch) if inverse else (bsz, rows, GRID_W, ch)
    lat = lat.reshape(shp).transpose(0, 2, 1, 3).reshape(bsz, -1, ch)
    return jnp.concatenate([t[:, :n_ctx], lat], axis=1)


def _seg_table(ctx_vec, lat_mat):
    bsz = lat_mat.shape[0]
    tab = jnp.stack([jnp.broadcast_to(ctx_vec[None, :], lat_mat.shape), lat_mat], axis=1)
    return tab.reshape(2 * bsz, 1, -1)


def kernel(x, c, ctx, c_ctx, ada_w, ada_b, w_in, s5_lambda_re, s5_lambda_im, s5_log_step, s5_b_re, s5_b_im, s5_c_re, s5_c_im, s5_d, s5_glu_w, ssd_conv_w, ssd_conv_b, ssd_a_log, ssd_dt_bias, ssd_d, ssd_norm_w, gla_gate_w, gla_gate_b, gla_norm_w, hy_conv_w, hy_conv_b, hy_w1, hy_b1, hy_freq1, hy_w2, hy_b2, hy_freq2, hy_w3, hy_bias, w_branch, w_out, moe_group_w, moe_group_b, moe_expert_w, moe_expert_b, moe_w_gate, moe_w_up, moe_w_down, final_norm_w):
    bsz, n_lat, d = x.shape
    n_ctx = ctx.shape[1]
    depth = ada_w.shape[0]
    t_b = n_ctx + n_lat
    t_all = bsz * t_b
    tiles_pb = t_b // ROW_TILE
    assert n_ctx == ROW_TILE and n_lat % ROW_TILE == 0 and n_lat == FFT_N // 2 and bsz + 1 <= 8

    xa = jnp.concatenate([ctx, x], axis=1).reshape(t_all, d)
    cc = jnp.pad(jnp.concatenate([c, c_ctx[None, :]], axis=0), ((0, 8 - bsz - 1), (0, 0)))
    mod = adaln(cc, ada_w, ada_b)

    for li in range(depth):
        last = li == depth - 1
        m6 = mod[li].reshape(8, 6, d)
        tabs = [_seg_table(m6[bsz, k], m6[:bsz, k]) for k in range(6)]
        h = modnorm(xa, tabs[0], tabs[1], tiles_pb)

        wl = w_in[li].astype(BF16)
        seg = lambda k0, k1: wl[:, IN_OFFS[k0]:IN_OFFS[k1]]
        (u_s5,) = proj_split(h, seg(0, 1), (MIX_WIDTH,))
        w_ssd = jnp.concatenate([seg(1, 3), _pad_cols(seg(3, 4), LANE)], axis=1)
        z_ssd, xbc_raw, dt_raw = proj_split(h, w_ssd, (MIX_WIDTH, SSD_CONV_CH, LANE))
        w_gla = jnp.concatenate([seg(4, 8), _pad_cols(seg(8, 9), LANE)], axis=1)
        q_g, k_g, v_g, g_g, r_g = proj_split(h, w_gla, (GLA_KEY_WIDTH, GLA_KEY_WIDTH, MIX_WIDTH, MIX_WIDTH, LANE))
        (p_hy,) = proj_split(h, seg(9, 10), (3 * MIX_WIDTH,))
        gates = matmul(h, seg(10, 11), 512, 1024, out_dtype=BF16)

        b3 = lambda a: a.reshape(bsz, t_b, a.shape[-1])

        ys = []
        for dr in range(2):
            wts = s5_weights(s5_lambda_re[li, dr], s5_lambda_im[li, dr], s5_log_step[li, dr], s5_b_re[li, dr],
                             s5_b_im[li, dr], s5_c_re[li, dr], s5_c_im[li, dr], dr == 1)
            ys.append(s5_scan(b3(u_s5), wts, n_ctx, dr == 1).reshape(t_all, MIX_WIDTH))
        ya = s5_out(ys[0], ys[1], u_s5, s5_d[li], s5_glu_w[li])

        z_cm = _lat_colmajor(b3(z_ssd), n_ctx)
        xbc_cm = _lat_colmajor(b3(xbc_raw), n_ctx)
        dt_cm = _lat_colmajor(b3(dt_raw), n_ctx)
        xbc_act = dwconv_seq(xbc_cm, ssd_conv_w[li], ssd_conv_b[li], n_ctx, silu=True)
        bias_row = _pad_cols(ssd_dt_bias[li].reshape(1, -1), LANE)
        a_row = _pad_cols(-jnp.exp(ssd_a_log[li]).reshape(1, -1), LANE)
        yd_ssd = [y.reshape(t_all, MIX_WIDTH) for y in ssd_scan(xbc_act, dt_cm, bias_row, a_row, n_ctx)]
        d_row = jnp.repeat(ssd_d[li], SSD_HEAD_DIM)[None, :]
        yb_cm = ssd_out(yd_ssd[0], yd_ssd[1], xbc_act.reshape(t_all, SSD_CONV_CH), z_cm.reshape(t_all, MIX_WIDTH),
                        d_row, ssd_norm_w[li])
        yb = _lat_colmajor(b3(yb_cm), n_ctx, inverse=True).reshape(t_all, MIX_WIDTH)

        wg = jnp.stack([jnp.pad(gla_gate_w[li, dr], ((dr * GLA_GATE_RANK, LANE - (dr + 1) * GLA_GATE_RANK), (0, 0)))
                        for dr in range(2)])
        os_ = [o.reshape(t_all, MIX_WIDTH)
               for o in gla_scan(b3(q_g), b3(k_g), b3(v_g), b3(r_g), wg, gla_gate_b[li][:, None, :], n_ctx)]
        yc = gla_out(os_[0], os_[1], g_g, gla_norm_w[li])

        u_hy = dwconv_seq(b3(p_hy), hy_conv_w[li], hy_conv_b[li], n_ctx, silu=False)
        hy_p = (hy_w1[li], hy_b1[li], hy_freq1[li], hy_w2[li], hy_b2[li], hy_freq2[li], hy_w3[li])
        filts = [hyena_filters(n_lat, *hy_p)]
        segs = [(n_ctx, n_lat)]
        if not last:
            filts.append(hyena_filters(n_ctx, *hy_p))
            segs.append((0, n_ctx))
        spectra = hyena_spectrum(filts)
        zero_rows = (0, n_ctx) if last else None
        nt = MIX_WIDTH // LANE
        y1 = hyena_conv(u_hy, 0, u_hy, nt, hy_bias[li][0:1], 0, spectra, segs, zero_rows)
        yd = hyena_conv(y1, 0, u_hy, 2 * nt, hy_bias[li][1:2], 1, spectra, segs, zero_rows)
        yd = yd.reshape(t_all, MIX_WIDTH)

        merged = merge_branches((ya, yb, yc, yd), gates, w_branch[li].astype(BF16))
        xa = matmul_gated_residual(merged, w_out[li].astype(BF16), xa, tabs[2], tiles_pb)

        h2, route = moe_router(xa, tabs[3], tabs[4], moe_group_w[li], moe_group_b[li], moe_expert_w[li],
                               moe_expert_b[li], tiles_pb)
        n_tiles = -(-t_all // MOE_TM) + MOE_GROUPS
        pos, tsrc, tgrp, tact = moe_plan(route, n_tiles)
        xs = moe_scatter(h2, pos, n_tiles * MOE_TM)
        ys = moe_experts(xs, tsrc, tgrp, tact, moe_w_gate, moe_w_up, moe_w_down, li)
        xa = moe_gather_residual(ys, pos, xa, tabs[5], tiles_pb)

    return final_norm(xa.reshape(bsz, t_b, d), final_norm_w, n_ctx)
```

```python
import functools
import math

import numpy as np
import jax
import jax.numpy as jnp
from jax import lax
from jax.experimental import pallas as pl
from jax.experimental.pallas import tpu as pltpu

F32 = jnp.float32
BF16 = jnp.bfloat16
HI = lax.Precision.HIGHEST

D_MODEL = 2048
GRID_W = 64
NORM_EPS = 1e-6
N_BRANCH = 4
MIX_WIDTH = 768
S5_GROUP = 16
S5_GROUPS = MIX_WIDTH // S5_GROUP
S5_STATE = 64
SSD_HEAD_DIM = 64
SSD_HEADS = MIX_WIDTH // SSD_HEAD_DIM
SSD_GROUPS = 2
SSD_STATE = 64
SSD_CHUNK = 128
SSD_CONV_CH = MIX_WIDTH + 2 * SSD_GROUPS * SSD_STATE
GLA_HEADS = 6
GLA_KEY_WIDTH = MIX_WIDTH // 2
GLA_DK = GLA_KEY_WIDTH // GLA_HEADS
GLA_DV = MIX_WIDTH // GLA_HEADS
GLA_GATE_RANK = 16
GLA_TAU = 16.0
GLA_CHUNK = 64
HY_ORDER = 2
HY_FILTER_DIM = 64
HY_BANDS = 16
HY_EMB = 2 * HY_BANDS + 1
HY_MAX_DECAY = math.log(1e-2) / 0.3
HY_MIN_DECAY = math.log(1e-2) / 1.5
MOE_GROUPS = 4
MOE_PER_GROUP = 4
MOE_EXPERTS = MOE_GROUPS * MOE_PER_GROUP
MOE_FF = 1024

IN_SIZES = (MIX_WIDTH, MIX_WIDTH, SSD_CONV_CH, 2 * SSD_HEADS, GLA_KEY_WIDTH, GLA_KEY_WIDTH, MIX_WIDTH, MIX_WIDTH,
            2 * GLA_GATE_RANK, (HY_ORDER + 1) * MIX_WIDTH, N_BRANCH * D_MODEL)
IN_OFFS = tuple(int(v) for v in np.cumsum((0,) + IN_SIZES))

LANE = 128
ROW_TILE = 256
VMEM_LIMIT = 56 * 1024 * 1024

FFT_N1 = 64
FFT_N2 = 128
FFT_N = FFT_N1 * FFT_N2


def _cparams(sem):
    return pltpu.CompilerParams(dimension_semantics=sem, vmem_limit_bytes=VMEM_LIMIT)


def _split3(a):
    hi = a.astype(BF16)
    r = a - hi.astype(F32)
    mid = r.astype(BF16)
    lo = (r - mid.astype(F32)).astype(BF16)
    return hi, mid, lo


def _dot_sel(sel, x):
    s = sel.astype(BF16)
    return sum(jnp.dot(s, t, preferred_element_type=F32) for t in _split3(x))


def _dot_by_sel(x, sel):
    s = sel.astype(BF16)
    return sum(jnp.dot(t, s, preferred_element_type=F32) for t in _split3(x))


def _split2(a):
    hi = a.astype(BF16)
    return hi, (a - hi.astype(F32)).astype(BF16)


def _dot3(a2, b):
    b_hi, b_lo = _split2(b)
    return (jnp.dot(a2[0], b_hi, preferred_element_type=F32) + jnp.dot(a2[0], b_lo, preferred_element_type=F32)
            + jnp.dot(a2[1], b_hi, preferred_element_type=F32))


def _pad_cols(a, width):
    return jnp.pad(a, [(0, 0)] * (a.ndim - 1) + [(0, width - a.shape[-1])])


def _mm_kernel(x_ref, w_ref, o_ref, *, precision):
    o_ref[...] = jnp.dot(x_ref[...], w_ref[...], preferred_element_type=F32,
                         precision=precision).astype(o_ref.dtype)


def matmul(x, w, tm, tn, out_dtype=F32, precision=None):
    m, k = x.shape
    n = w.shape[1]
    return pl.pallas_call(
        functools.partial(_mm_kernel, precision=precision),
        grid=(n // tn, m // tm),
        in_specs=[pl.BlockSpec((tm, k), lambda j, i: (i, 0)), pl.BlockSpec((k, tn), lambda j, i: (0, j))],
        out_specs=pl.BlockSpec((tm, tn), lambda j, i: (i, j)),
        out_shape=jax.ShapeDtypeStruct((m, n), out_dtype),
        compiler_params=_cparams(("parallel", "parallel")),
    )(x, w)


def _proj_kernel(x_ref, w_ref, *o_refs):
    acc = jnp.dot(x_ref[...], w_ref[...], preferred_element_type=F32)
    off = 0
    for o_ref in o_refs:
        wd = o_ref.shape[-1]
        o_ref[...] = acc[:, off:off + wd].astype(o_ref.dtype)
        off += wd


def proj_split(x, w, widths, tm=512):
    m, k = x.shape
    n = w.shape[1]
    assert sum(widths) == n
    return pl.pallas_call(
        _proj_kernel,
        grid=(m // tm,),
        in_specs=[pl.BlockSpec((tm, k), lambda i: (i, 0)), pl.BlockSpec((k, n), lambda i: (0, 0))],
        out_specs=[pl.BlockSpec((tm, wd), lambda i: (i, 0)) for wd in widths],
        out_shape=[jax.ShapeDtypeStruct((m, wd), F32) for wd in widths],
        compiler_params=_cparams(("parallel",)),
    )(x, w)


def _adaln_kernel(c_ref, w_ref, b_ref, o_ref):
    c = c_ref[...]
    s = c * jax.nn.sigmoid(c)
    o_ref[...] = jnp.dot(s, w_ref[...], preferred_element_type=F32, precision=HI) + b_ref[...]


def adaln(cc, ada_w, ada_b, tn=1024):
    depth, d, n = ada_w.shape
    return pl.pallas_call(
        _adaln_kernel,
        grid=(depth, n // tn),
        in_specs=[pl.BlockSpec((8, d), lambda l, j: (0, 0)),
                  pl.BlockSpec((None, d, tn), lambda l, j: (l, 0, j)),
                  pl.BlockSpec((None, 1, tn), lambda l, j: (l, 0, j))],
        out_specs=pl.BlockSpec((None, 8, tn), lambda l, j: (l, 0, j)),
        out_shape=jax.ShapeDtypeStruct((depth, 8, n), F32),
        compiler_params=_cparams(("parallel", "parallel")),
    )(cc, ada_w, ada_b.reshape(depth, 1, n))


def _seg_index(i, tiles_per_batch):
    b = i // tiles_per_batch
    return 2 * b + jnp.where(i % tiles_per_batch == 0, 0, 1)


def _modnorm_kernel(x_ref, sh_ref, sc_ref, o_ref):
    x = x_ref[...]
    ms = jnp.mean(x * x, axis=-1, keepdims=True)
    o_ref[...] = (x * lax.rsqrt(ms + NORM_EPS) * (1.0 + sc_ref[...]) + sh_ref[...]).astype(o_ref.dtype)


def modnorm(x, shift, scale, tiles_per_batch):
    t, d = x.shape
    seg = lambda i: (_seg_index(i, tiles_per_batch), 0, 0)
    return pl.pallas_call(
        _modnorm_kernel,
        grid=(t // ROW_TILE,),
        in_specs=[pl.BlockSpec((ROW_TILE, d), lambda i: (i, 0)),
                  pl.BlockSpec((None, 1, d), seg), pl.BlockSpec((None, 1, d), seg)],
        out_specs=pl.BlockSpec((ROW_TILE, d), lambda i: (i, 0)),
        out_shape=jax.ShapeDtypeStruct((t, d), BF16),
        compiler_params=_cparams(("parallel",)),
    )(x, shift, scale)


def _dwconv_kernel(x_ref, w_ref, b_ref, o_ref, *, n_ctx, silu):
    x = x_ref[...]
    t = x.shape[0]
    rows = lax.broadcasted_iota(jnp.int32, x.shape, 0)
    prev = jnp.where((rows == 0) | (rows == n_ctx), 0.0, pltpu.roll(x, 1, 0))
    nxt = jnp.where((rows == n_ctx - 1) | (rows == t - 1), 0.0, pltpu.roll(x, t - 1, 0))
    y = w_ref[0:1, :] * prev + w_ref[1:2, :] * x + w_ref[2:3, :] * nxt + b_ref[...]
    if silu:
        y = y * jax.nn.sigmoid(y)
    o_ref[...] = y


def dwconv_seq(x, w, b, n_ctx, silu, tc=LANE):
    bsz, t, ch = x.shape
    return pl.pallas_call(
        functools.partial(_dwconv_kernel, n_ctx=n_ctx, silu=silu),
        grid=(bsz, ch // tc),
        in_specs=[pl.BlockSpec((None, t, tc), lambda b_, j: (b_, 0, j)),
                  pl.BlockSpec((3, tc), lambda b_, j: (0, j)),
                  pl.BlockSpec((1, tc), lambda b_, j: (0, j))],
        out_specs=pl.BlockSpec((None, t, tc), lambda b_, j: (b_, 0, j)),
        out_shape=jax.ShapeDtypeStruct((bsz, t, ch), F32),
        compiler_params=_cparams(("parallel", "parallel")),
    )(x, w, b.reshape(1, ch))


S5_Q = 8
S5_TG = LANE // S5_GROUP
S5_TS = S5_TG * S5_STATE
S5_NT = MIX_WIDTH // LANE


def _cmul(ar, ai, br, bi):
    return ar * br - ai * bi, ar * bi + ai * br


def s5_weights(lam_re, lam_im, log_step, b_re, b_im, c_re, c_im, reverse):
    g, p, q = S5_GROUPS, S5_STATE, S5_Q
    step = jnp.exp(log_step)[:, None]
    mag = jnp.exp(lam_re * step)
    ab_re = mag * jnp.cos(lam_im * step)
    ab_im = mag * jnp.sin(lam_im * step)
    den = lam_re * lam_re + lam_im * lam_im
    zr, zi = _cmul(ab_re - 1.0, ab_im, lam_re, -lam_im)
    zr, zi = zr / den, zi / den
    bb_re, bb_im = _cmul(zr[..., None], zi[..., None], b_re, b_im)
    pw_re, pw_im = [jnp.ones_like(ab_re)], [jnp.zeros_like(ab_im)]
    for _ in range(q):
        r, i = _cmul(pw_re[-1], pw_im[-1], ab_re, ab_im)
        pw_re.append(r)
        pw_im.append(i)
    pw_re, pw_im = jnp.stack(pw_re), jnp.stack(pw_im)
    cp_re = c_re[None] * pw_re[:, :, None, :] - c_im[None] * pw_im[:, :, None, :]
    cp_im = c_re[None] * pw_im[:, :, None, :] + c_im[None] * pw_re[:, :, None, :]
    kd = (jnp.einsum('dghp,gpk->dghk', cp_re[:q], bb_re, precision=HI)
          - jnp.einsum('dghp,gpk->dghk', cp_im[:q], bb_im, precision=HI))
    def group_blocks(tab, w):
        lead = tab.shape[:-3]
        rows = tab.shape[-2]
        expand = jnp.asarray(np.tile(np.eye(w, dtype=np.float32), (1, S5_TG)))
        wide = jnp.dot(tab, expand, precision=HI).reshape(lead + (S5_NT, S5_TG, rows, S5_TG * w))
        own = (np.arange(S5_TG * w)[None, None, :] // w) == np.arange(S5_TG)[:, None, None]
        return jnp.where(jnp.asarray(own), wide, 0.0).reshape(lead + (S5_NT, S5_TG * rows, S5_TG * w))

    kblk = group_blocks(jnp.swapaxes(kd, -1, -2), S5_GROUP)
    zero_blk = jnp.zeros_like(kblk[0])
    lag = (lambda s, r: s - r) if reverse else (lambda s, r: r - s)
    ktoep = jnp.concatenate(
        [jnp.concatenate([kblk[lag(s, r)] if lag(s, r) >= 0 else zero_blk for r in range(q)], axis=-1)
         for s in range(q)], axis=-2)
    e_of_s = (np.arange(q) if reverse else (q - 1 - np.arange(q)))
    pb_re = pw_re[e_of_s][..., None] * bb_re[None] - pw_im[e_of_s][..., None] * bb_im[None]
    pb_im = pw_re[e_of_s][..., None] * bb_im[None] + pw_im[e_of_s][..., None] * bb_re[None]
    pblk = [group_blocks(jnp.swapaxes(t, -1, -2), p) for t in (pb_re, pb_im)]
    bw = jnp.concatenate([jnp.concatenate([pblk[0][s], pblk[1][s]], axis=-1) for s in range(q)], axis=-2)
    f_of_r = (q - np.arange(q)) if reverse else (np.arange(q) + 1)
    cblk = [group_blocks(jnp.swapaxes(t[f_of_r], -1, -2), S5_GROUP) for t in (cp_re, -cp_im)]
    cw = jnp.concatenate([jnp.concatenate([cb[r] for r in range(q)], axis=-1) for cb in cblk], axis=-2)
    ar, ai = pw_re[q], pw_im[q]
    lv = []
    for _ in range(16):
        lv.append(jnp.concatenate([ar.reshape(S5_NT, S5_TS), ai.reshape(S5_NT, S5_TS)], axis=-1))
        ar, ai = _cmul(ar, ai, ar, ai)
    a2 = jnp.stack(lv, axis=1)
    return ktoep.astype(BF16), bw.astype(BF16), cw.astype(BF16), a2


def _s5_seg_scan(xr, xi, a2_ref, reverse):
    n = xr.shape[0]
    rows = lax.broadcasted_iota(jnp.int32, xr.shape, 0)
    k, shift = 0, 1
    while shift < n:
        ar = a2_ref[k:k + 1, :S5_TS]
        ai = a2_ref[k:k + 1, S5_TS:]
        if reverse:
            keep = rows < n - shift
            sr, si = pltpu.roll(xr, n - shift, 0), pltpu.roll(xi, n - shift, 0)
        else:
            keep = rows >= shift
            sr, si = pltpu.roll(xr, shift, 0), pltpu.roll(xi, shift, 0)
        sr = jnp.where(keep, sr, 0.0)
        si = jnp.where(keep, si, 0.0)
        xr, xi = xr + ar * sr - ai * si, xi + ar * si + ai * sr
        k += 1
        shift *= 2
    return xr, xi


def _s5_entering(sr, si, carry, a2_ref, reverse):
    n = sr.shape[0]
    rows = lax.broadcasted_iota(jnp.int32, sr.shape, 0)
    first = (n - 1) if reverse else 0
    if carry is not None:
        cr, ci = carry
        ar, ai = a2_ref[0:1, :S5_TS], a2_ref[0:1, S5_TS:]
        ir, ii = _cmul(ar, ai, cr, ci)
        sr = jnp.where(rows == first, sr + ir, sr)
        si = jnp.where(rows == first, si + ii, si)
    hr, hi = _s5_seg_scan(sr, si, a2_ref, reverse)
    if reverse:
        out = (hr[0:1], hi[0:1])
        er, ei = pltpu.roll(hr, n - 1, 0), pltpu.roll(hi, n - 1, 0)
    else:
        out = (hr[n - 1:n], hi[n - 1:n])
        er, ei = pltpu.roll(hr, 1, 0), pltpu.roll(hi, 1, 0)
    if carry is None:
        er = jnp.where(rows == first, 0.0, er)
        ei = jnp.where(rows == first, 0.0, ei)
    else:
        er = jnp.where(rows == first, cr, er)
        ei = jnp.where(rows == first, ci, ei)
    return er, ei, out


def _s5_kernel(u_ref, kt_ref, bw_ref, cw_ref, a2_ref, y_ref, *, n_ctx_rows, reverse):
    n = u_ref.shape[0] // S5_Q
    x = jnp.concatenate([u_ref[pl.ds(s, n, stride=S5_Q), :] for s in range(S5_Q)], axis=1).astype(BF16)
    y = jnp.dot(x, kt_ref[...], preferred_element_type=F32)
    st = jnp.dot(x, bw_ref[...], preferred_element_type=F32)
    sr, si = st[:, :S5_TS], st[:, S5_TS:]
    cr, ci, carry = _s5_entering(sr[:n_ctx_rows], si[:n_ctx_rows], None, a2_ref, reverse)
    lr, li, _ = _s5_entering(sr[n_ctx_rows:], si[n_ctx_rows:], carry, a2_ref, reverse)
    h = jnp.concatenate([jnp.concatenate([cr, lr], axis=0), jnp.concatenate([ci, li], axis=0)], axis=1)
    y = y + jnp.dot(h.astype(BF16), cw_ref[...], preferred_element_type=F32)
    for r in range(S5_Q):
        y_ref[pl.ds(r, n, stride=S5_Q), :] = y[:, r * LANE:(r + 1) * LANE]


def s5_scan(u, weights, n_ctx, reverse):
    bsz, t, ch = u.shape
    ktoep, bw, cw, a2 = weights
    qk = S5_Q * LANE
    return pl.pallas_call(
        functools.partial(_s5_kernel, n_ctx_rows=n_ctx // S5_Q, reverse=reverse),
        grid=(S5_NT, bsz),
        in_specs=[pl.BlockSpec((None, t, LANE), lambda j, b: (b, 0, j)),
                  pl.BlockSpec((None, qk, qk), lambda j, b: (j, 0, 0)),
                  pl.BlockSpec((None, qk, 2 * S5_TS), lambda j, b: (j, 0, 0)),
                  pl.BlockSpec((None, 2 * S5_TS, qk), lambda j, b: (j, 0, 0)),
                  pl.BlockSpec((None, 16, 2 * S5_TS), lambda j, b: (j, 0, 0))],
        out_specs=pl.BlockSpec((None, t, LANE), lambda j, b: (b, 0, j)),
        out_shape=jax.ShapeDtypeStruct((bsz, t, ch), F32),
        compiler_params=_cparams(("parallel", "parallel")),
    )(u, ktoep, bw, cw, a2)


def _s5_out_kernel(yf_ref, yb_ref, u_ref, d_ref, w_ref, o_ref):
    y = yf_ref[...] + yb_ref[...] + d_ref[...] * u_ref[...]
    g = jax.nn.gelu(y)
    z = jnp.dot(g.astype(BF16), w_ref[...], preferred_element_type=F32)
    o_ref[...] = (g * jax.nn.sigmoid(z)).astype(o_ref.dtype)


def s5_out(yf, yb, u, d_skip, glu_w, tm=512):
    t, ch = u.shape
    row = pl.BlockSpec((tm, ch), lambda i: (i, 0))
    return pl.pallas_call(
        _s5_out_kernel,
        grid=(t // tm,),
        in_specs=[row, row, row, pl.BlockSpec((1, ch), lambda i: (0, 0)), pl.BlockSpec((ch, ch), lambda i: (0, 0))],
        out_specs=row,
        out_shape=jax.ShapeDtypeStruct((t, ch), BF16),
        compiler_params=_cparams(("parallel",)),
    )(yf, yb, u, d_skip.reshape(1, ch), glu_w.astype(BF16))


def _softplus(x):
    return jnp.maximum(x, 0.0) + jnp.log1p(jnp.exp(-jnp.abs(x)))


def _ssd_kernel(xf_ref, dtf_ref, xb_ref, dtb_ref, bias_ref, a_ref, spread_ref, tri_ref, yf_ref, yb_ref, st_ref, *,
                bsz):
    @pl.when(pl.program_id(0) == 0)
    def _():
        st_ref[...] = jnp.zeros_like(st_ref)

    srcs = ((xf_ref, dtf_ref, yf_ref), (xb_ref, dtb_ref, yb_ref))
    chains = [(d, b) for d in range(2) for b in range(bsz)]
    results = [_ssd_step(srcs[d][0].at[b], srcs[d][1].at[b], bias_ref, a_ref, spread_ref.at[d], tri_ref.at[d],
                         st_ref[d, b], d=d, reverse=d == 1) for d, b in chains]
    for (d, b), (y, st_new) in zip(chains, results):
        srcs[d][2][b] = y
        st_ref[d, b] = st_new


def _ssd_head_spread():
    m = np.zeros((2, LANE, MIX_WIDTH), np.float32)
    for d in range(2):
        for h in range(SSD_HEADS):
            m[d, d * SSD_HEADS + h, h * SSD_HEAD_DIM:(h + 1) * SSD_HEAD_DIM] = 1.0
    return m


def _ssd_step(xbc_ref, dt_ref, bias_ref, a_ref, spread_ref, tri_ref, st, *, d, reverse):
    q = SSD_CHUNK
    hp = SSD_HEAD_DIM
    hg = SSD_HEADS // SSD_GROUPS
    dt = _softplus(dt_ref[...] + bias_ref[...])
    adt = dt * a_ref[...]
    cs = _dot_sel(tri_ref[...], adt)
    cs_t = cs.T
    spread = spread_ref[...]
    dt_x = _dot_by_sel(dt, spread)
    cs_x = _dot_by_sel(cs, spread)
    tot_x = cs_x[0:1, :] if reverse else cs_x[q - 1:q, :]
    ecs_x = jnp.exp(cs_x)
    dec_x = jnp.exp(tot_x - cs_x)
    etot_x = jnp.exp(tot_x)
    xdt = xbc_ref[:, :MIX_WIDTH] * dt_x
    xw = (xdt * dec_x).astype(BF16)
    xdt = xdt.astype(BF16)
    bm = xbc_ref[:, MIX_WIDTH:MIX_WIDTH + SSD_GROUPS * SSD_STATE]
    cm = xbc_ref[:, MIX_WIDTH + SSD_GROUPS * SSD_STATE:]
    bm_t = bm.T
    li = lax.broadcasted_iota(jnp.int32, (q, q), 0)
    si = lax.broadcasted_iota(jnp.int32, (q, q), 1)
    mask = (si >= li) if reverse else (si <= li)
    first_of_pair = lax.broadcasted_iota(jnp.int32, (q, 2 * hp), 1) < hp
    gw = hg * hp
    ys, states = [], []
    for g in range(SSD_GROUPS):
        cg = cm[:, g * SSD_STATE:(g + 1) * SSD_STATE].astype(BF16)
        bg = bm[:, g * SSD_STATE:(g + 1) * SSD_STATE].astype(BF16)
        gmat = lax.dot_general(cg, bg, (((1,), (1,)), ((), ())), preferred_element_type=F32)
        yoff = jnp.dot(cg, st[g].astype(BF16), preferred_element_type=F32) * ecs_x[:, g * gw:(g + 1) * gw]
        for pr in range(hg // 2):
            lanes = slice(g * gw + pr * 2 * hp, g * gw + (pr + 1) * 2 * hp)
            xp = xdt[:, lanes]
            yd = []
            for hl in (2 * pr, 2 * pr + 1):
                col = d * SSD_HEADS + g * hg + hl
                seg = cs[:, col:col + 1] - cs_t[col:col + 1, :]
                lmat = jnp.exp(jnp.where(mask, seg, -jnp.inf))
                yd.append(jnp.dot((gmat * lmat).astype(BF16), xp, preferred_element_type=F32))
            ys.append(jnp.where(first_of_pair, yd[0], yd[1]) + yoff[:, pr * 2 * hp:(pr + 1) * 2 * hp])
        new = jnp.dot(bm_t[g * SSD_STATE:(g + 1) * SSD_STATE, :].astype(BF16), xw[:, g * gw:(g + 1) * gw],
                      preferred_element_type=F32)
        states.append(st[g] * etot_x[:, g * gw:(g + 1) * gw] + new)
    return jnp.concatenate(ys, axis=1), jnp.stack(states)


def ssd_scan(xbc, dt_raw, dt_bias_row, a_row, n_ctx):
    bsz, t, _ = xbc.shape
    q = SSD_CHUNK
    orders = _chunk_order(t // q, n_ctx // q)
    blk = lambda w, cidx: pl.BlockSpec((bsz, q, w), lambda i: (0, cidx(i), 0))
    vec = pl.BlockSpec((1, LANE), lambda i: (0, 0))
    return pl.pallas_call(
        functools.partial(_ssd_kernel, bsz=bsz),
        grid=(t // q,),
        in_specs=[blk(w, c) for c in orders for w in (SSD_CONV_CH, LANE)]
        + [vec, vec, pl.BlockSpec((2, LANE, MIX_WIDTH), lambda i: (0, 0, 0)),
           pl.BlockSpec((2, q, q), lambda i: (0, 0, 0))],
        out_specs=[blk(MIX_WIDTH, c) for c in orders],
        out_shape=[jax.ShapeDtypeStruct((bsz, t, MIX_WIDTH), F32)] * 2,
        scratch_shapes=[pltpu.VMEM((2, bsz, SSD_GROUPS, SSD_STATE, (SSD_HEADS // SSD_GROUPS) * SSD_HEAD_DIM), F32)],
        compiler_params=_cparams(("arbitrary",)),
        name="ssd_scan",
    )(xbc, dt_raw, xbc, dt_raw, dt_bias_row, a_row, jnp.asarray(_ssd_head_spread()), _cumsum_mats(q))


def _ssd_out_kernel(yf_ref, yb_ref, xbc_ref, z_ref, d_ref, nw_ref, o_ref):
    z = z_ref[...]
    y = (d_ref[...] * xbc_ref[...] + yf_ref[...] + yb_ref[...]) * (z * jax.nn.sigmoid(z))
    ms = jnp.mean(y * y, axis=-1, keepdims=True)
    o_ref[...] = (y * lax.rsqrt(ms + NORM_EPS) * nw_ref[...]).astype(o_ref.dtype)


def ssd_out(yf, yb, xbc, z, d_row, norm_w, tm=512):
    t, ch = z.shape
    row = pl.BlockSpec((tm, ch), lambda i: (i, 0))
    vec = pl.BlockSpec((1, ch), lambda i: (0, 0))
    return pl.pallas_call(
        _ssd_out_kernel,
        grid=(t // tm,),
        in_specs=[row, row, pl.BlockSpec((tm, ch), lambda i: (i, 0)), row, vec, vec],
        out_specs=row,
        out_shape=jax.ShapeDtypeStruct((t, ch), BF16),
        compiler_params=_cparams(("parallel",)),
    )(yf, yb, xbc, z, d_row, norm_w.reshape(1, ch))


def _gla_kernel(*refs, bsz):
    ins, (wg_ref, gb_ref, tri_ref), outs, st_ref = refs[:8], refs[8:11], refs[11:13], refs[13]

    @pl.when(pl.program_id(0) == 0)
    def _():
        st_ref[...] = jnp.zeros_like(st_ref)

    chains = [(d, b) for d in range(2) for b in range(bsz)]
    results = []
    for d, b in chains:
        q_ref, k_ref, v_ref, r_ref = ins[4 * d:4 * d + 4]
        results.append(_gla_chain(q_ref[b], k_ref[b], v_ref[b], r_ref[b], wg_ref[d], gb_ref[d], tri_ref[d],
                                  st_ref[d, b], reverse=d == 1))
    for (d, b), (o, st_new) in zip(chains, results):
        outs[d][b] = o
        st_ref[d, b] = st_new


def _gla_chain(q, k, v, r, wg, gb, tri, st, *, reverse):
    qs = GLA_CHUNK
    zl = _dot3(_split2(r), wg) + gb
    la = (jnp.minimum(zl, 0.0) - jnp.log1p(jnp.exp(-jnp.abs(zl)))) / GLA_TAU
    bc = _dot_sel(tri, la)
    tot = bc[0:1, :] if reverse else bc[qs - 1:qs, :]
    qd = (q * (GLA_DK ** -0.5) * jnp.exp(bc)).astype(BF16)
    ki = (k * jnp.exp(-bc)).astype(BF16)
    ke = (k * jnp.exp(tot - bc)).astype(BF16)
    gam = jnp.exp(tot)
    ti = lax.broadcasted_iota(jnp.int32, (qs, qs), 0)
    si = lax.broadcasted_iota(jnp.int32, (qs, qs), 1)
    mask = (si >= ti) if reverse else (si <= ti)
    pair_w = 2 * GLA_DK
    first_q = lax.broadcasted_iota(jnp.int32, (qs, pair_w), 1) < GLA_DK
    first_s = lax.broadcasted_iota(jnp.int32, (GLA_DV, pair_w), 1) < GLA_DK
    nt_dims = (((1,), (1,)), ((), ()))
    outs, states = [], []
    for pr in range(GLA_HEADS // 2):
        lanes = slice(pr * pair_w, (pr + 1) * pair_w)
        qd_p, ki_p, ke_p, st_p = qd[:, lanes], ki[:, lanes], ke[:, lanes], st[:, lanes]
        st_b = st_p.astype(BF16)
        ds = []
        for j in range(2):
            h = 2 * pr + j
            vh = v[:, h * GLA_DV:(h + 1) * GLA_DV].astype(BF16)
            qm = jnp.where(first_q if j == 0 else jnp.logical_not(first_q), qd_p, jnp.zeros_like(qd_p))
            sc = lax.dot_general(qm, ki_p, nt_dims, preferred_element_type=F32)
            sc = jnp.where(mask, sc, 0.0).astype(BF16)
            o_intra = jnp.dot(sc, vh, preferred_element_type=F32)
            o_inter = lax.dot_general(qm, st_b, nt_dims, preferred_element_type=F32)
            outs.append(o_intra + o_inter)
            ds.append(lax.dot_general(vh, ke_p, (((0,), (0,)), ((), ())), preferred_element_type=F32))
        states.append(st_p * gam[:, lanes] + jnp.where(first_s, ds[0], ds[1]))
    return jnp.concatenate(outs, axis=1), jnp.concatenate(states, axis=1)


def _chunk_order(nchunk, nctx):
    fwd = lambda i: i
    bwd = lambda i: jnp.where(i < nctx, nctx - 1 - i, nchunk + nctx - 1 - i)
    return fwd, bwd


def _cumsum_mats(q):
    tri = np.tril(np.ones((q, q), np.float32))
    return jnp.asarray(np.stack([tri, tri.T]))


def gla_scan(q, k, v, r, wg, gb, n_ctx):
    bsz, t, _ = q.shape
    qs = GLA_CHUNK
    orders = _chunk_order(t // qs, n_ctx // qs)
    blk = lambda w, cidx: pl.BlockSpec((bsz, qs, w), lambda i: (0, cidx(i), 0))
    widths = (GLA_KEY_WIDTH, GLA_KEY_WIDTH, MIX_WIDTH, LANE)
    return pl.pallas_call(
        functools.partial(_gla_kernel, bsz=bsz),
        grid=(t // qs,),
        in_specs=[blk(w, c) for c in orders for w in widths]
        + [pl.BlockSpec((2, LANE, GLA_KEY_WIDTH), lambda i: (0, 0, 0)),
           pl.BlockSpec((2, 1, GLA_KEY_WIDTH), lambda i: (0, 0, 0)),
           pl.BlockSpec((2, qs, qs), lambda i: (0, 0, 0))],
        out_specs=[blk(MIX_WIDTH, c) for c in orders],
        out_shape=[jax.ShapeDtypeStruct((bsz, t, MIX_WIDTH), F32)] * 2,
        scratch_shapes=[pltpu.VMEM((2, bsz, GLA_DV, GLA_KEY_WIDTH), F32)],
        compiler_params=_cparams(("arbitrary",)),
        name="gla_scan",
    )(q, k, v, r, q, k, v, r, wg, gb, _cumsum_mats(qs))


def _gla_out_kernel(of_ref, ob_ref, g_ref, nw_ref, o_ref):
    g = g_ref[...]
    sg = g * jax.nn.sigmoid(g)
    for h in range(GLA_HEADS):
        vs = slice(h * GLA_DV, (h + 1) * GLA_DV)
        o = of_ref[:, vs] + ob_ref[:, vs]
        ms = jnp.mean(o * o, axis=-1, keepdims=True)
        o_ref[:, vs] = (o * lax.rsqrt(ms + NORM_EPS) * nw_ref[...] * sg[:, vs]).astype(o_ref.dtype)


def gla_out(of, ob, g, norm_w, tm=512):
    t, ch = g.shape
    row = pl.BlockSpec((tm, ch), lambda i: (i, 0))
    return pl.pallas_call(
        _gla_out_kernel,
        grid=(t // tm,),
        in_specs=[row, row, row, pl.BlockSpec((1, GLA_DV), lambda i: (0, 0))],
        out_specs=row,
        out_shape=jax.ShapeDtypeStruct((t, ch), BF16),
        compiler_params=_cparams(("parallel",)),
    )(of, ob, g, norm_w.reshape(1, GLA_DV))


def _hy_emb(n):
    t = np.linspace(0.0, 1.0, n)[:, None]
    freqs = np.linspace(1e-4, HY_BANDS - 1, HY_BANDS)
    ang = (2.0 * math.pi / n) * np.arange(n)[:, None] * freqs[None, :]
    emb = np.concatenate([t, np.cos(ang), -np.sin(ang)], axis=-1)
    return np.pad(emb, ((0, 0), (0, LANE - HY_EMB))).astype(np.float32)


def _hy_filter_kernel(emb_ref, w1_ref, b1_ref, f1_ref, w2_ref, b2_ref, f2_ref, w3_ref, dl_ref, o_ref):
    emb = emb_ref[...]
    h = jnp.sin(f1_ref[...] * (jnp.dot(emb, w1_ref[...], preferred_element_type=F32, precision=HI) + b1_ref[...]))
    h = jnp.sin(f2_ref[...] * (jnp.dot(h, w2_ref[...], preferred_element_type=F32, precision=HI) + b2_ref[...]))
    h = jnp.dot(h, w3_ref[...], preferred_element_type=F32, precision=HI)
    o_ref[...] = h * jnp.exp(-emb[:, 0:1] * dl_ref[...])


def hyena_filters(n, w1, b1, f1, w2, b2, f2, w3, tm=256):
    nf = w3.shape[1]
    emb = jnp.asarray(_hy_emb(n))
    deltas = np.abs(np.linspace(HY_MIN_DECAY, HY_MAX_DECAY, MIX_WIDTH)).astype(np.float32)
    dl = jnp.asarray(np.tile(deltas, nf // MIX_WIDTH)[None, :])
    w1p = jnp.pad(w1, ((0, LANE - HY_EMB), (0, 0)))
    fd = HY_FILTER_DIM
    vec = lambda w: pl.BlockSpec((1, w), lambda i: (0, 0))
    return pl.pallas_call(
        _hy_filter_kernel,
        grid=(n // tm,),
        in_specs=[pl.BlockSpec((tm, LANE), lambda i: (i, 0)),
                  pl.BlockSpec((LANE, fd), lambda i: (0, 0)), vec(fd), vec(fd),
                  pl.BlockSpec((fd, fd), lambda i: (0, 0)), vec(fd), vec(fd),
                  pl.BlockSpec((fd, nf), lambda i: (0, 0)), vec(nf)],
        out_specs=pl.BlockSpec((tm, nf), lambda i: (i, 0)),
        out_shape=jax.ShapeDtypeStruct((n, nf), F32),
        compiler_params=_cparams(("parallel",)),
    )(emb, w1p, b1.reshape(1, fd), f1.reshape(1, fd), w2, b2.reshape(1, fd), f2.reshape(1, fd), w3, dl)


class _HyDft:
    def __init__(self, n):
        big = 2 * n
        self.n1 = big // FFT_N2
        self.rows = n // FFT_N2
        self.kh = self.n1 // 2 + 1
        self.khp = -(-self.kh // 8) * 8
        k1 = np.arange(self.kh)
        ang = 2.0 * np.pi * np.outer(k1, np.arange(self.rows)) / self.n1
        fwd = np.zeros((2 * self.khp, self.rows))
        fwd[:self.kh] = np.cos(ang)
        fwd[self.khp:self.khp + self.kh] = -np.sin(ang)
        wgt = np.full(self.kh, 2.0)
        wgt[0] = wgt[-1] = 1.0
        inv = np.zeros((self.rows, 2 * self.khp))
        inv[:, :self.kh] = (wgt[:, None] * np.cos(ang)).T / big
        inv[:, self.khp:self.khp + self.kh] = -(wgt[:, None] * np.sin(ang)).T / big
        tw = 2.0 * np.pi * np.outer(k1, np.arange(FFT_N2)) / big
        f32 = lambda a: np.ascontiguousarray(a, dtype=np.float32)
        self.fwd, self.inv = f32(fwd), f32(inv)
        self.tw_re, self.tw_im = f32(np.cos(tw)[:, :, None]), f32(-np.sin(tw)[:, :, None])
        self.small = self.rows < 8


def _inner_dft_mats():
    k2 = np.arange(FFT_N2)
    f2 = np.exp(-2j * np.pi * np.outer(k2, k2) / FFT_N2)
    w2 = np.block([[f2.real, -f2.imag], [f2.imag, f2.real]])
    w2c = np.block([[f2.real, f2.imag], [-f2.imag, f2.real]])
    return w2.astype(np.float32), w2c.astype(np.float32)


_W2, _W2C = _inner_dft_mats()
HY_UNROLL = 4


def _sc_load(ref, idx):
    return jnp.concatenate([ref[w, idx, :] for w in range(ref.shape[0])], axis=1)


def _sc_store(ref, idx, val):
    for w in range(ref.shape[0]):
        ref[w, idx, :] = val[:, w * LANE:(w + 1) * LANE]


def _hy_outer_fwd(load_rows, cst, fwd_ref, xr_ref, xi_ref):
    if cst.small:
        blocks = [load_rows(n1 * FFT_N2, FFT_N2, None) for n1 in range(cst.rows)]
        for k1 in range(cst.kh):
            xr = sum(float(cst.fwd[k1, n1]) * blocks[n1] for n1 in range(cst.rows))
            xi = sum(float(cst.fwd[cst.khp + k1, n1]) * blocks[n1] for n1 in range(cst.rows))
            _sc_store(xr_ref, pl.ds(k1 * FFT_N2, FFT_N2), xr)
            _sc_store(xi_ref, pl.ds(k1 * FFT_N2, FFT_N2), xi)
        return

    fwd2 = _split2(fwd_ref[...])

    def body(n2, c):
        res = _dot3(fwd2, load_rows(n2, cst.rows, FFT_N2))
        _sc_store(xr_ref, pl.ds(n2, cst.khp, stride=FFT_N2), res[:cst.khp])
        _sc_store(xi_ref, pl.ds(n2, cst.khp, stride=FFT_N2), res[cst.khp:])
        return c
    lax.fori_loop(0, FFT_N2, body, 0, unroll=HY_UNROLL)


def _hy_inner_fwd(k1, xr_ref, xi_ref, twr_ref, twi_ref, w2_ref):
    start = k1 * FFT_N2
    rows = pl.ds(start if isinstance(k1, int) else pl.multiple_of(start, FFT_N2), FFT_N2)
    xr, xi = _sc_load(xr_ref, rows), _sc_load(xi_ref, rows)
    twr, twi = twr_ref[k1], twi_ref[k1]
    ar = xr * twr - xi * twi
    ai = xr * twi + xi * twr
    ys = jnp.dot(w2_ref[...], jnp.concatenate([ar, ai], axis=0).astype(BF16), preferred_element_type=F32)
    return rows, twr, twi, ys[:FFT_N2], ys[FFT_N2:]


def _hy_for_k1(cst, body):
    if cst.small:
        for k1 in range(cst.kh):
            body(k1, 0)
    else:
        lax.fori_loop(0, cst.kh, body, 0, unroll=3 if cst.kh % 3 == 0 else 1)


def _hy_spectrum_part(cst, hf_ref, hb_ref, fwd_ref, twr_ref, twi_ref, w2_ref, hr_ref, hi_ref, xr_ref, xi_ref):
    width = hf_ref.shape[1]

    def load_rows(start, count, stride):
        idx = pl.ds(start, count) if stride is None else pl.ds(start, count, stride=stride)
        return jnp.concatenate([hf_ref[idx, :], hb_ref[idx, :]], axis=1)
    _hy_outer_fwd(load_rows, cst, fwd_ref, xr_ref, xi_ref)
    hb0 = hb_ref[0:1, :]

    def body(k1, c):
        rows, _, _, yr, yi = _hy_inner_fwd(k1, xr_ref, xi_ref, twr_ref, twi_ref, w2_ref)
        hr_ref[rows, :] = yr[:, :width] + yr[:, width:] - hb0
        hi_ref[rows, :] = yi[:, :width] - yi[:, width:]
        return c
    _hy_for_k1(cst, body)


def _hy_spectrum_kernel(*refs, parts):
    w2_ref = refs[0]
    pos = 1
    n_in = 5 * len(parts)
    outs = refs[1 + n_in:1 + n_in + 2 * len(parts)]
    xr_ref, xi_ref = refs[-2:]
    for p, cst in enumerate(parts):
        hf_ref, hb_ref, fwd_ref, twr_ref, twi_ref = refs[pos:pos + 5]
        pos += 5
        _hy_spectrum_part(cst, hf_ref, hb_ref, fwd_ref, twr_ref, twi_ref, w2_ref, outs[2 * p], outs[2 * p + 1],
                          xr_ref, xi_ref)


def hyena_spectrum(filts):
    parts = tuple(_HyDft(f.shape[0]) for f in filts)
    nt = MIX_WIDTH // LANE
    width = HY_ORDER * MIX_WIDTH
    args = [jnp.asarray(_W2).astype(BF16)]
    in_specs = [pl.BlockSpec((2 * FFT_N2, 2 * FFT_N2), lambda t: (0, 0))]
    out_specs, out_shape = [], []
    for f, cst in zip(filts, parts):
        n = f.shape[0]
        args += [f, f, jnp.asarray(cst.fwd), jnp.asarray(cst.tw_re), jnp.asarray(cst.tw_im)]
        in_specs += [pl.BlockSpec((n, LANE), lambda t: (0, (t // nt) * 2 * nt + t % nt)),
                     pl.BlockSpec((n, LANE), lambda t: (0, (t // nt) * 2 * nt + nt + t % nt)),
                     pl.BlockSpec(cst.fwd.shape, lambda t: (0, 0)),
                     pl.BlockSpec(cst.tw_re.shape, lambda t: (0, 0, 0)),
                     pl.BlockSpec(cst.tw_im.shape, lambda t: (0, 0, 0))]
        out_specs += [pl.BlockSpec((cst.kh * FFT_N2, LANE), lambda t: (0, t))] * 2
        out_shape += [jax.ShapeDtypeStruct((cst.kh * FFT_N2, width), F32)] * 2
    rows = max(c.khp for c in parts) * FFT_N2
    outs = pl.pallas_call(
        functools.partial(_hy_spectrum_kernel, parts=parts),
        grid=(width // LANE,),
        in_specs=in_specs, out_specs=out_specs, out_shape=out_shape,
        scratch_shapes=[pltpu.VMEM((2, rows, LANE), F32)] * 2,
        compiler_params=_cparams(("parallel",)),
        name="hyena_spectrum",
    )(*args)
    return [(outs[2 * p], outs[2 * p + 1]) for p in range(len(parts))]


def _hy_conv_part(cst, row0, u_ref, g_ref, bias_ref, hr_ref, hi_ref, fwd_ref, inv_ref, twr_ref, twi_ref,
                  w2_ref, w2c_ref, o_ref, xr_ref, xi_ref):
    bsz = u_ref.shape[0]

    def load_rows(start, count, stride):
        idx = pl.ds(row0 + start, count) if stride is None else pl.ds(row0 + start, count, stride=stride)
        return jnp.concatenate([u_ref[b, idx, :] for b in range(bsz)], axis=1)
    _hy_outer_fwd(load_rows, cst, fwd_ref, xr_ref, xi_ref)

    def body(k1, c):
        rows, twr, twi, yr, yi = _hy_inner_fwd(k1, xr_ref, xi_ref, twr_ref, twi_ref, w2_ref)
        hr = jnp.concatenate([hr_ref[rows, :]] * bsz, axis=1)
        hi = jnp.concatenate([hi_ref[rows, :]] * bsz, axis=1)
        zr = yr * hr - yi * hi
        zi = yr * hi + yi * hr
        vs = jnp.dot(w2c_ref[...], jnp.concatenate([zr, zi], axis=0).astype(BF16), preferred_element_type=F32)
        vr, vi = vs[:FFT_N2], vs[FFT_N2:]
        _sc_store(xr_ref, rows, vr * twr + vi * twi)
        _sc_store(xi_ref, rows, vi * twr - vr * twi)
        return c
    _hy_for_k1(cst, body)

    bias = bias_ref[...]

    def emit(idx, y):
        for b in range(bsz):
            yb = y[:, b * LANE:(b + 1) * LANE]
            o_ref[b, idx, :] = (g_ref[b, idx, :] * (yb + bias * u_ref[b, idx, :])).astype(o_ref.dtype)

    if cst.small:
        for n1 in range(cst.rows):
            y = sum(float(cst.inv[n1, k1]) * _sc_load(xr_ref, pl.ds(k1 * FFT_N2, FFT_N2))
                    + float(cst.inv[n1, cst.khp + k1]) * _sc_load(xi_ref, pl.ds(k1 * FFT_N2, FFT_N2))
                    for k1 in range(cst.kh))
            emit(pl.ds(row0 + n1 * FFT_N2, FFT_N2), y)
        return

    inv2 = _split2(inv_ref[...])

    def out_body(n2, c):
        z = jnp.concatenate([_sc_load(xr_ref, pl.ds(n2, cst.khp, stride=FFT_N2)),
                             _sc_load(xi_ref, pl.ds(n2, cst.khp, stride=FFT_N2))], axis=0)
        y = _dot3(inv2, z)
        emit(pl.ds(row0 + n2, cst.rows, stride=FFT_N2), y)
        return c
    lax.fori_loop(0, FFT_N2, out_body, 0, unroll=HY_UNROLL)


def _hy_conv_kernel(*refs, parts, row0s, zero_rows):
    u_ref, g_ref, bias_ref, w2_ref, w2c_ref = refs[:5]
    o_ref, xr_ref, xi_ref = refs[-3:]
    pos = 5
    for cst, row0 in zip(parts, row0s):
        hr_ref, hi_ref, fwd_ref, inv_ref, twr_ref, twi_ref = refs[pos:pos + 6]
        pos += 6
        _hy_conv_part(cst, row0, u_ref, g_ref, bias_ref, hr_ref, hi_ref, fwd_ref, inv_ref, twr_ref, twi_ref,
                      w2_ref, w2c_ref, o_ref, xr_ref, xi_ref)
    if zero_rows is not None:
        lo, hi = zero_rows
        o_ref[:, lo:hi, :] = jnp.zeros((o_ref.shape[0], hi - lo, o_ref.shape[2]), o_ref.dtype)


def hyena_conv(u, u_blk, gate, g_blk, bias, order, spectra, segs, zero_rows):
    out_dtype = F32
    bsz, t, _ = u.shape
    nt = MIX_WIDTH // LANE
    parts = tuple(_HyDft(n) for _, n in segs)
    args = [u, gate, bias, jnp.asarray(_W2).astype(BF16), jnp.asarray(_W2C).astype(BF16)]
    mat = pl.BlockSpec((2 * FFT_N2, 2 * FFT_N2), lambda j: (0, 0))
    in_specs = [pl.BlockSpec((bsz, t, LANE), lambda j: (0, 0, u_blk + j)),
                pl.BlockSpec((bsz, t, LANE), lambda j: (0, 0, g_blk + j)),
                pl.BlockSpec((1, LANE), lambda j: (0, j)), mat, mat]
    for (hr, hi), cst in zip(spectra, parts):
        args += [hr, hi, jnp.asarray(cst.fwd), jnp.asarray(cst.inv), jnp.asarray(cst.tw_re), jnp.asarray(cst.tw_im)]
        hspec = pl.BlockSpec((cst.kh * FFT_N2, LANE), lambda j: (0, order * nt + j))
        in_specs += [hspec, hspec, pl.BlockSpec(cst.fwd.shape, lambda j: (0, 0)),
                     pl.BlockSpec(cst.inv.shape, lambda j: (0, 0)),
                     pl.BlockSpec(cst.tw_re.shape, lambda j: (0, 0, 0)),
                     pl.BlockSpec(cst.tw_im.shape, lambda j: (0, 0, 0))]
    rows = max(c.khp for c in parts) * FFT_N2
    return pl.pallas_call(
        functools.partial(_hy_conv_kernel, parts=parts, row0s=tuple(r for r, _ in segs), zero_rows=zero_rows),
        grid=(nt,),
        in_specs=in_specs,
        out_specs=pl.BlockSpec((bsz, t, LANE), lambda j: (0, 0, j)),
        out_shape=jax.ShapeDtypeStruct((bsz, t, MIX_WIDTH), out_dtype),
        scratch_shapes=[pltpu.VMEM((bsz, rows, LANE), F32)] * 2,
        compiler_params=_cparams(("parallel",)),
        name="hyena_conv",
    )(*args)


def _merge_kernel(ya_ref, yb_ref, yc_ref, yd_ref, ga_ref, gb_ref, gc_ref, gd_ref, w_ref, o_ref):
    acc = None
    for b, (y_ref, g_ref) in enumerate(((ya_ref, ga_ref), (yb_ref, gb_ref), (yc_ref, gc_ref), (yd_ref, gd_ref))):
        t = jax.nn.sigmoid(g_ref[...].astype(F32)) * jnp.dot(y_ref[...].astype(BF16), w_ref[b], preferred_element_type=F32)
        acc = t if acc is None else acc + t
    o_ref[...] = acc.astype(o_ref.dtype)


def merge_branches(ys, gates, w_branch, tm=512, tn=512):
    t, ch = ys[0].shape
    nj = D_MODEL // tn
    yspec = pl.BlockSpec((tm, ch), lambda j, i: (i, 0))
    gspecs = [pl.BlockSpec((tm, tn), functools.partial(lambda j, i, b: (i, b * nj + j), b=b)) for b in range(N_BRANCH)]
    return pl.pallas_call(
        _merge_kernel,
        grid=(nj, t // tm),
        in_specs=[yspec] * 4 + gspecs + [pl.BlockSpec((N_BRANCH, ch, tn), lambda j, i: (0, 0, j))],
        out_specs=pl.BlockSpec((tm, tn), lambda j, i: (i, j)),
        out_shape=jax.ShapeDtypeStruct((t, D_MODEL), BF16),
        compiler_params=_cparams(("parallel", "parallel")),
    )(*ys, gates, gates, gates, gates, w_branch)


def _mm_resid_kernel(a_ref, w_ref, x_ref, g0_ref, g1_ref, o_ref):
    y = jnp.dot(a_ref[...], w_ref[...], preferred_element_type=F32)
    o_ref[:ROW_TILE] = x_ref[:ROW_TILE] + g0_ref[...] * y[:ROW_TILE]
    o_ref[ROW_TILE:] = x_ref[ROW_TILE:] + g1_ref[...] * y[ROW_TILE:]


def matmul_gated_residual(a, w, x, gate, tiles_per_batch, tn=1024):
    t, k = a.shape
    n = w.shape[1]
    tm = 2 * ROW_TILE
    gspec = lambda h: pl.BlockSpec((None, 1, tn), lambda j, i: (_seg_index(2 * i + h, tiles_per_batch), 0, j))
    return pl.pallas_call(
        _mm_resid_kernel,
        grid=(n // tn, t // tm),
        in_specs=[pl.BlockSpec((tm, k), lambda j, i: (i, 0)),
                  pl.BlockSpec((k, tn), lambda j, i: (0, j)),
                  pl.BlockSpec((tm, tn), lambda j, i: (i, j)),
                  gspec(0), gspec(1)],
        out_specs=pl.BlockSpec((tm, tn), lambda j, i: (i, j)),
        out_shape=jax.ShapeDtypeStruct((t, n), F32),
        compiler_params=_cparams(("parallel", "parallel")),
    )(a, w, x, gate, gate)


ROUTE_OFF = MOE_GROUPS
TOK_S = D_MODEL // LANE
MOE_PAIRS = MOE_PER_GROUP * (MOE_PER_GROUP - 1) // 2
MOE_CLASSES = MOE_GROUPS * MOE_PAIRS
DSP_S = TOK_S + 8
MOE_TM = 512


def _store_token_major(ref, val, slab=TOK_S):
    rows = val.shape[0]
    for s in range(TOK_S):
        ref[pl.ds(s, rows, stride=slab), :] = val[:, s * LANE:(s + 1) * LANE]


def _load_token_major(ref, rows, slab=TOK_S):
    return jnp.concatenate([ref[pl.ds(s, rows, stride=slab), :] for s in range(TOK_S)], axis=1)


def _router_kernel(x_ref, sh_ref, sc_ref, wr_ref, br_ref, h_ref, comb_ref):
    x = x_ref[...]
    ms = jnp.mean(x * x, axis=-1, keepdims=True)
    h = x * lax.rsqrt(ms + NORM_EPS) * (1.0 + sc_ref[...]) + sh_ref[...]
    _store_token_major(h_ref, h, DSP_S)
    logits = _dot3(_split2(h), wr_ref[...]) + br_ref[...]
    lane = lax.broadcasted_iota(jnp.int32, logits.shape, 1).astype(F32)
    neg = -jnp.inf
    gmask = lane < MOE_GROUPS
    glog = jnp.where(gmask, logits, neg)
    gmax = jnp.max(glog, axis=-1, keepdims=True)
    g_idx = jnp.min(jnp.where(glog == gmax, lane, float(LANE)), axis=-1, keepdims=True)
    p_g = 1.0 / jnp.sum(jnp.exp(glog - gmax), axis=-1, keepdims=True)
    e_lo = ROUTE_OFF + g_idx * MOE_PER_GROUP
    emask = (lane >= e_lo) & (lane < e_lo + MOE_PER_GROUP)
    v1 = jnp.max(jnp.where(emask, logits, neg), axis=-1, keepdims=True)
    i1 = jnp.min(jnp.where(emask & (logits == v1), lane, float(LANE)), axis=-1, keepdims=True)
    emask2 = emask & (lane != i1)
    v2 = jnp.max(jnp.where(emask2, logits, neg), axis=-1, keepdims=True)
    i2 = jnp.min(jnp.where(emask2 & (logits == v2), lane, float(LANE)), axis=-1, keepdims=True)
    e21 = jnp.exp(v2 - v1)
    w1 = p_g / (1.0 + e21)
    w2 = p_g * e21 / (1.0 + e21)
    comb = jnp.where(lane == i1, w1, jnp.where(lane == i2, w2, 0.0))
    lo = jnp.minimum(i1, i2) - e_lo
    hi = jnp.maximum(i1, i2) - e_lo
    cls = g_idx * MOE_PAIRS + lo * (7.0 - lo) * 0.5 + (hi - lo - 1.0)
    route = jnp.where(lane == 0, cls, comb)
    comb_ref[...] = route
    rows = x.shape[0]
    h_ref[pl.ds(TOK_S, rows, stride=DSP_S), :] = route
    for s in range(TOK_S + 1, DSP_S):
        h_ref[pl.ds(s, rows, stride=DSP_S), :] = jnp.zeros_like(route)


def moe_router(x, shift, scale, group_w, group_b, expert_w, expert_b, tiles_per_batch):
    t, d = x.shape
    wr = _pad_cols(jnp.concatenate([group_w, expert_w], axis=1), LANE)
    br = _pad_cols(jnp.concatenate([group_b, expert_b])[None, :], LANE)
    seg = lambda i: (_seg_index(i, tiles_per_batch), 0, 0)
    return pl.pallas_call(
        _router_kernel,
        grid=(t // ROW_TILE,),
        in_specs=[pl.BlockSpec((ROW_TILE, d), lambda i: (i, 0)),
                  pl.BlockSpec((None, 1, d), seg), pl.BlockSpec((None, 1, d), seg),
                  pl.BlockSpec((d, LANE), lambda i: (0, 0)), pl.BlockSpec((1, LANE), lambda i: (0, 0))],
        out_specs=[pl.BlockSpec((ROW_TILE * DSP_S, LANE), lambda i: (i, 0)),
                   pl.BlockSpec((ROW_TILE, LANE), lambda i: (i, 0))],
        out_shape=[jax.ShapeDtypeStruct((t * DSP_S, LANE), F32), jax.ShapeDtypeStruct((t, LANE), F32)],
        compiler_params=_cparams(("parallel",)),
    )(x, shift, scale, wr, br)


def moe_plan(route, n_tiles):
    g = route[:, 0].astype(jnp.int32)
    oh = (g[:, None] == jnp.arange(MOE_CLASSES, dtype=jnp.int32)[None, :]).astype(jnp.int32)
    cnt = jnp.sum(oh, axis=0)
    rank = jnp.sum((jnp.cumsum(oh, axis=0) - oh) * oh, axis=1)
    ptiles = (cnt + MOE_TM - 1) // MOE_TM
    ends = jnp.cumsum(ptiles)
    off = (ends - ptiles) * MOE_TM
    pos = jnp.sum(oh * off[None, :], axis=1) + rank
    n_act = ends[-1]
    tid = jnp.arange(n_tiles, dtype=jnp.int32)
    tsrc = jnp.minimum(tid, n_act - 1)
    tcls = jnp.sum((tsrc[:, None] >= ends[None, :]).astype(jnp.int32), axis=1)
    pair_lo = jnp.asarray([lo for lo in range(MOE_PER_GROUP) for _ in range(lo + 1, MOE_PER_GROUP)], jnp.int32)
    pair_hi = jnp.asarray([hi for lo in range(MOE_PER_GROUP) for hi in range(lo + 1, MOE_PER_GROUP)], jnp.int32)
    base = (tcls // MOE_PAIRS) * MOE_PER_GROUP
    texp = jnp.stack([base + pair_lo[tcls % MOE_PAIRS], base + pair_hi[tcls % MOE_PAIRS]], axis=1).reshape(-1)
    tact = (tid < n_act).astype(jnp.int32)
    return pos, tsrc, texp, tact


def _slab(ref, row, size):
    return ref.at[pl.ds(pl.multiple_of(row * size, size), size)]


def _moe_scatter_kernel(pos_ref, src_ref, init_ref, dst_ref, sem):
    del init_ref
    base = pl.program_id(0) * ROW_TILE

    def issue(r, c):
        pltpu.make_async_copy(_slab(src_ref, r, DSP_S), _slab(dst_ref, pos_ref[base + r], DSP_S), sem).start()
        return c
    lax.fori_loop(0, ROW_TILE, issue, 0)
    pltpu.make_async_copy(src_ref, dst_ref.at[pl.ds(0, ROW_TILE * DSP_S)], sem).wait()


def moe_scatter(slabs, pos, n_slots):
    t = pos.shape[0]
    init = jnp.zeros((n_slots * DSP_S, LANE), F32)
    return pl.pallas_call(
        _moe_scatter_kernel,
        grid_spec=pltpu.PrefetchScalarGridSpec(
            num_scalar_prefetch=1, grid=(t // ROW_TILE,),
            in_specs=[pl.BlockSpec((ROW_TILE * DSP_S, LANE), lambda i, p: (i, 0)),
                      pl.BlockSpec(memory_space=pl.ANY)],
            out_specs=pl.BlockSpec(memory_space=pl.ANY),
            scratch_shapes=[pltpu.SemaphoreType.DMA(())]),
        out_shape=jax.ShapeDtypeStruct(init.shape, F32),
        input_output_aliases={2: 0},
        compiler_params=_cparams(("arbitrary",)),
    )(pos, slabs, init)


def _moe_expert_kernel(tsrc_ref, texp_ref, tact_ref, xs_ref, wg_ref, wu_ref, wd_ref, ys_ref, x_ref, acc_ref):
    i, e, f = pl.program_id(0), pl.program_id(1), pl.program_id(2)
    active = tact_ref[i] == 1
    first = (e == 0) & (f == 0)
    last = (e == pl.num_programs(1) - 1) & (f == pl.num_programs(2) - 1)

    @pl.when(active & first)
    def _():
        x_ref[...] = _load_token_major(xs_ref, MOE_TM, DSP_S).astype(BF16)
        acc_ref[...] = jnp.zeros_like(acc_ref)

    @pl.when(active)
    def _():
        x = x_ref[...]
        a = jnp.dot(x, wg_ref[...].astype(BF16), preferred_element_type=F32)
        u = jnp.dot(x, wu_ref[...].astype(BF16), preferred_element_type=F32)
        act = (a * jax.nn.sigmoid(a) * u).astype(BF16)
        route = xs_ref[pl.ds(TOK_S, MOE_TM, stride=DSP_S), :]
        lane = lax.broadcasted_iota(jnp.int32, route.shape, 1)
        mine = lane == ROUTE_OFF + texp_ref[2 * i + e]
        w = jnp.sum(jnp.where(mine, route, 0.0), axis=-1, keepdims=True)
        acc_ref[...] += w * jnp.dot(act, wd_ref[...].astype(BF16), preferred_element_type=F32)

    @pl.when(active & last)
    def _():
        _store_token_major(ys_ref, acc_ref[...])

    @pl.when(jnp.logical_not(active) & first)
    def _():
        ys_ref[...] = jnp.zeros_like(ys_ref)


def moe_experts(xs, tsrc, texp, tact, w_gate, w_up, w_down, li, tf=256):
    n_tiles = tsrc.shape[0]
    _, _, d, ff = w_gate.shape
    wsel = lambda i, e, f, ts, te, ta: (li, te[2 * i + e], 0, f)
    rows = lambda i, e, f, ts, te, ta: (ts[i], 0)
    return pl.pallas_call(
        _moe_expert_kernel,
        grid_spec=pltpu.PrefetchScalarGridSpec(
            num_scalar_prefetch=3, grid=(n_tiles, 2, ff // tf),
            in_specs=[pl.BlockSpec((MOE_TM * DSP_S, LANE), rows),
                      pl.BlockSpec((None, None, d, tf), wsel),
                      pl.BlockSpec((None, None, d, tf), wsel),
                      pl.BlockSpec((None, None, tf, d), lambda i, e, f, ts, te, ta: (li, te[2 * i + e], f, 0))],
            out_specs=pl.BlockSpec((MOE_TM * TOK_S, LANE), lambda i, e, f, ts, te, ta: (i, 0)),
            scratch_shapes=[pltpu.VMEM((MOE_TM, d), BF16), pltpu.VMEM((MOE_TM, d), F32)]),
        out_shape=jax.ShapeDtypeStruct((n_tiles * MOE_TM * TOK_S, LANE), F32),
        compiler_params=_cparams(("arbitrary", "arbitrary", "arbitrary")),
        name="moe_experts",
    )(tsrc, texp, tact, xs, w_gate, w_up, w_down)


def _moe_gather_kernel(pos_ref, ys_ref, x_ref, g_ref, o_ref, buf_ref, sem):
    base = pl.program_id(0) * ROW_TILE

    def issue(r, c):
        pltpu.make_async_copy(_slab(ys_ref, pos_ref[base + r], TOK_S), _slab(buf_ref, r, TOK_S), sem).start()
        return c
    lax.fori_loop(0, ROW_TILE, issue, 0)
    pltpu.make_async_copy(ys_ref.at[pl.ds(0, ROW_TILE * TOK_S)], buf_ref, sem).wait()
    o_ref[...] = x_ref[...] + g_ref[...] * _load_token_major(buf_ref, ROW_TILE)


def moe_gather_residual(ys, pos, x, gate, tiles_per_batch):
    t, d = x.shape
    return pl.pallas_call(
        _moe_gather_kernel,
        grid_spec=pltpu.PrefetchScalarGridSpec(
            num_scalar_prefetch=1, grid=(t // ROW_TILE,),
            in_specs=[pl.BlockSpec(memory_space=pl.ANY),
                      pl.BlockSpec((ROW_TILE, d), lambda i, p: (i, 0)),
                      pl.BlockSpec((None, 1, d), lambda i, p: (_seg_index(i, tiles_per_batch), 0, 0))],
            out_specs=pl.BlockSpec((ROW_TILE, d), lambda i, p: (i, 0)),
            scratch_shapes=[pltpu.VMEM((ROW_TILE * TOK_S, LANE), F32), pltpu.SemaphoreType.DMA(())]),
        out_shape=jax.ShapeDtypeStruct((t, d), F32),
        compiler_params=_cparams(("arbitrary",)),
    )(pos, ys, x, gate)


def _final_norm_kernel(x_ref, w_ref, o_ref):
    x = x_ref[...]
    ms = jnp.mean(x * x, axis=-1, keepdims=True)
    o_ref[...] = x * lax.rsqrt(ms + NORM_EPS) * w_ref[...]


def final_norm(x, w, n_ctx):
    bsz, t, d = x.shape
    off = n_ctx // ROW_TILE
    return pl.pallas_call(
        _final_norm_kernel,
        grid=(bsz, (t - n_ctx) // ROW_TILE),
        in_specs=[pl.BlockSpec((None, ROW_TILE, d), lambda b, i: (b, i + off, 0)),
                  pl.BlockSpec((1, d), lambda b, i: (0, 0))],
        out_specs=pl.BlockSpec((None, ROW_TILE, d), lambda b, i: (b, i, 0)),
        out_shape=jax.ShapeDtypeStruct((bsz, t - n_ctx, d), F32),
        compiler_params=_cparams(("parallel", "parallel")),
    )(x, w.reshape(1, d))


def _lat_colmajor(t, n_ctx, inverse=False):
    bsz, _, ch = t.shape
    lat = t[:, n_ctx:]
    rows = lat.shape[1] // GRID_W
    shp = (bsz, GRID_W, rows, ch) if inverse else (bsz, rows, GRID_W, ch)
    lat = lat.reshape(shp).transpose(0, 2, 1, 3).reshape(bsz, -1, ch)
    return jnp.concatenate([t[:, :n_ctx], lat], axis=1)


def _seg_table(ctx_vec, lat_mat):
    bsz = lat_mat.shape[0]
    tab = jnp.stack([jnp.broadcast_to(ctx_vec[None, :], lat_mat.shape), lat_mat], axis=1)
    return tab.reshape(2 * bsz, 1, -1)


def kernel(x, c, ctx, c_ctx, ada_w, ada_b, w_in, s5_lambda_re, s5_lambda_im, s5_log_step, s5_b_re, s5_b_im, s5_c_re, s5_c_im, s5_d, s5_glu_w, ssd_conv_w, ssd_conv_b, ssd_a_log, ssd_dt_bias, ssd_d, ssd_norm_w, gla_gate_w, gla_gate_b, gla_norm_w, hy_conv_w, hy_conv_b, hy_w1, hy_b1, hy_freq1, hy_w2, hy_b2, hy_freq2, hy_w3, hy_bias, w_branch, w_out, moe_group_w, moe_group_b, moe_expert_w, moe_expert_b, moe_w_gate, moe_w_up, moe_w_down, final_norm_w):
    bsz, n_lat, d = x.shape
    n_ctx = ctx.shape[1]
    depth = ada_w.shape[0]
    t_b = n_ctx + n_lat
    t_all = bsz * t_b
    tiles_pb = t_b // ROW_TILE
    assert n_ctx == ROW_TILE and n_lat % ROW_TILE == 0 and n_lat == FFT_N // 2 and bsz + 1 <= 8

    xa = jnp.concatenate([ctx, x], axis=1).reshape(t_all, d)
    cc = jnp.pad(jnp.concatenate([c, c_ctx[None, :]], axis=0), ((0, 8 - bsz - 1), (0, 0)))
    mod = adaln(cc, ada_w, ada_b)

    for li in range(depth):
        last = li == depth - 1
        m6 = mod[li].reshape(8, 6, d)
        tabs = [_seg_table(m6[bsz, k], m6[:bsz, k]) for k in range(6)]
        h = modnorm(xa, tabs[0], tabs[1], tiles_pb)

        wl = w_in[li].astype(BF16)
        seg = lambda k0, k1: wl[:, IN_OFFS[k0]:IN_OFFS[k1]]
        (u_s5,) = proj_split(h, seg(0, 1), (MIX_WIDTH,))
        w_ssd = jnp.concatenate([seg(1, 3), _pad_cols(seg(3, 4), LANE)], axis=1)
        z_ssd, xbc_raw, dt_raw = proj_split(h, w_ssd, (MIX_WIDTH, SSD_CONV_CH, LANE))
        w_gla = jnp.concatenate([seg(4, 8), _pad_cols(seg(8, 9), LANE)], axis=1)
        q_g, k_g, v_g, g_g, r_g = proj_split(h, w_gla, (GLA_KEY_WIDTH, GLA_KEY_WIDTH, MIX_WIDTH, MIX_WIDTH, LANE))
        (p_hy,) = proj_split(h, seg(9, 10), (3 * MIX_WIDTH,))
        gates = matmul(h, seg(10, 11), 512, 1024, out_dtype=BF16)

        b3 = lambda a: a.reshape(bsz, t_b, a.shape[-1])

        ys = []
        for dr in range(2):
            wts = s5_weights(s5_lambda_re[li, dr], s5_lambda_im[li, dr], s5_log_step[li, dr], s5_b_re[li, dr],
                             s5_b_im[li, dr], s5_c_re[li, dr], s5_c_im[li, dr], dr == 1)
            ys.append(s5_scan(b3(u_s5), wts, n_ctx, dr == 1).reshape(t_all, MIX_WIDTH))
        ya = s5_out(ys[0], ys[1], u_s5, s5_d[li], s5_glu_w[li])

        z_cm = _lat_colmajor(b3(z_ssd), n_ctx)
        xbc_cm = _lat_colmajor(b3(xbc_raw), n_ctx)
        dt_cm = _lat_colmajor(b3(dt_raw), n_ctx)
        xbc_act = dwconv_seq(xbc_cm, ssd_conv_w[li], ssd_conv_b[li], n_ctx, silu=True)
        bias_row = _pad_cols(ssd_dt_bias[li].reshape(1, -1), LANE)
        a_row = _pad_cols(-jnp.exp(ssd_a_log[li]).reshape(1, -1), LANE)
        yd_ssd = [y.reshape(t_all, MIX_WIDTH) for y in ssd_scan(xbc_act, dt_cm, bias_row, a_row, n_ctx)]
        d_row = jnp.repeat(ssd_d[li], SSD_HEAD_DIM)[None, :]
        yb_cm = ssd_out(yd_ssd[0], yd_ssd[1], xbc_act.reshape(t_all, SSD_CONV_CH), z_cm.reshape(t_all, MIX_WIDTH),
                        d_row, ssd_norm_w[li])
        yb = _lat_colmajor(b3(yb_cm), n_ctx, inverse=True).reshape(t_all, MIX_WIDTH)

        wg = jnp.stack([jnp.pad(gla_gate_w[li, dr], ((dr * GLA_GATE_RANK, LANE - (dr + 1) * GLA_GATE_RANK), (0, 0)))
                        for dr in range(2)])
        os_ = [o.reshape(t_all, MIX_WIDTH)
               for o in gla_scan(b3(q_g), b3(k_g), b3(v_g), b3(r_g), wg, gla_gate_b[li][:, None, :], n_ctx)]
        yc = gla_out(os_[0], os_[1], g_g, gla_norm_w[li])

        u_hy = dwconv_seq(b3(p_hy), hy_conv_w[li], hy_conv_b[li], n_ctx, silu=False)
        hy_p = (hy_w1[li], hy_b1[li], hy_freq1[li], hy_w2[li], hy_b2[li], hy_freq2[li], hy_w3[li])
        filts = [hyena_filters(n_lat, *hy_p)]
        segs = [(n_ctx, n_lat)]
        if not last:
            filts.append(hyena_filters(n_ctx, *hy_p))
            segs.append((0, n_ctx))
        spectra = hyena_spectrum(filts)
        zero_rows = (0, n_ctx) if last else None
        nt = MIX_WIDTH // LANE
        y1 = hyena_conv(u_hy, 0, u_hy, nt, hy_bias[li][0:1], 0, spectra, segs, zero_rows)
        yd = hyena_conv(y1, 0, u_hy, 2 * nt, hy_bias[li][1:2], 1, spectra, segs, zero_rows)
        yd = yd.reshape(t_all, MIX_WIDTH)

        merged = merge_branches((ya, yb, yc, yd), gates, w_branch[li].astype(BF16))
        xa = matmul_gated_residual(merged, w_out[li].astype(BF16), xa, tabs[2], tiles_pb)

        h2, route = moe_router(xa, tabs[3], tabs[4], moe_group_w[li], moe_group_b[li], moe_expert_w[li],
                               moe_expert_b[li], tiles_pb)
        n_tiles = -(-t_all // MOE_TM) + MOE_CLASSES
        pos, tsrc, texp, tact = moe_plan(route, n_tiles)
        xs = moe_scatter(h2, pos, n_tiles * MOE_TM)
        ys = moe_experts(xs, tsrc, texp, tact, moe_w_gate, moe_w_up, moe_w_down, li)
        xa = moe_gather_residual(ys, pos, xa, tabs[5], tiles_pb)

    return final_norm(xa.reshape(bsz, t_b, d), final_norm_w, n_ctx)
```

```python
import functools
import math

import numpy as np
import jax
import jax.numpy as jnp
from jax import lax
from jax.experimental import pallas as pl
from jax.experimental.pallas import tpu as pltpu

F32 = jnp.float32
BF16 = jnp.bfloat16
HI = lax.Precision.HIGHEST

D_MODEL = 2048
GRID_W = 64
NORM_EPS = 1e-6
N_BRANCH = 4
MIX_WIDTH = 768
S5_GROUP = 16
S5_GROUPS = MIX_WIDTH // S5_GROUP
S5_STATE = 64
SSD_HEAD_DIM = 64
SSD_HEADS = MIX_WIDTH // SSD_HEAD_DIM
SSD_GROUPS = 2
SSD_STATE = 64
SSD_CHUNK = 128
SSD_CONV_CH = MIX_WIDTH + 2 * SSD_GROUPS * SSD_STATE
GLA_HEADS = 6
GLA_KEY_WIDTH = MIX_WIDTH // 2
GLA_DK = GLA_KEY_WIDTH // GLA_HEADS
GLA_DV = MIX_WIDTH // GLA_HEADS
GLA_GATE_RANK = 16
GLA_TAU = 16.0
GLA_CHUNK = 64
HY_ORDER = 2
HY_FILTER_DIM = 64
HY_BANDS = 16
HY_EMB = 2 * HY_BANDS + 1
HY_MAX_DECAY = math.log(1e-2) / 0.3
HY_MIN_DECAY = math.log(1e-2) / 1.5
MOE_GROUPS = 4
MOE_PER_GROUP = 4
MOE_EXPERTS = MOE_GROUPS * MOE_PER_GROUP
MOE_FF = 1024

IN_SIZES = (MIX_WIDTH, MIX_WIDTH, SSD_CONV_CH, 2 * SSD_HEADS, GLA_KEY_WIDTH, GLA_KEY_WIDTH, MIX_WIDTH, MIX_WIDTH,
            2 * GLA_GATE_RANK, (HY_ORDER + 1) * MIX_WIDTH, N_BRANCH * D_MODEL)
IN_OFFS = tuple(int(v) for v in np.cumsum((0,) + IN_SIZES))

LANE = 128
ROW_TILE = 256
VMEM_LIMIT = 56 * 1024 * 1024

FFT_N1 = 64
FFT_N2 = 128
FFT_N = FFT_N1 * FFT_N2


def _cparams(sem):
    return pltpu.CompilerParams(dimension_semantics=sem, vmem_limit_bytes=VMEM_LIMIT)


def _split3(a):
    hi = a.astype(BF16)
    r = a - hi.astype(F32)
    mid = r.astype(BF16)
    lo = (r - mid.astype(F32)).astype(BF16)
    return hi, mid, lo


def _dot_sel(sel, x):
    s = sel.astype(BF16)
    return sum(jnp.dot(s, t, preferred_element_type=F32) for t in _split3(x))


def _dot_by_sel(x, sel):
    s = sel.astype(BF16)
    return sum(jnp.dot(t, s, preferred_element_type=F32) for t in _split3(x))


def _split2(a):
    hi = a.astype(BF16)
    return hi, (a - hi.astype(F32)).astype(BF16)


def _dot3(a2, b):
    b_hi, b_lo = _split2(b)
    return (jnp.dot(a2[0], b_hi, preferred_element_type=F32) + jnp.dot(a2[0], b_lo, preferred_element_type=F32)
            + jnp.dot(a2[1], b_hi, preferred_element_type=F32))


def _pad_cols(a, width):
    return jnp.pad(a, [(0, 0)] * (a.ndim - 1) + [(0, width - a.shape[-1])])


def _mm_kernel(x_ref, w_ref, o_ref, *, precision):
    o_ref[...] = jnp.dot(x_ref[...], w_ref[...], preferred_element_type=F32,
                         precision=precision).astype(o_ref.dtype)


def matmul(x, w, tm, tn, out_dtype=F32, precision=None):
    m, k = x.shape
    n = w.shape[1]
    return pl.pallas_call(
        functools.partial(_mm_kernel, precision=precision),
        grid=(n // tn, m // tm),
        in_specs=[pl.BlockSpec((tm, k), lambda j, i: (i, 0)), pl.BlockSpec((k, tn), lambda j, i: (0, j))],
        out_specs=pl.BlockSpec((tm, tn), lambda j, i: (i, j)),
        out_shape=jax.ShapeDtypeStruct((m, n), out_dtype),
        compiler_params=_cparams(("parallel", "parallel")),
    )(x, w)


def _proj_kernel(x_ref, w_ref, *o_refs):
    acc = jnp.dot(x_ref[...], w_ref[...], preferred_element_type=F32)
    off = 0
    for o_ref in o_refs:
        wd = o_ref.shape[-1]
        o_ref[...] = acc[:, off:off + wd].astype(o_ref.dtype)
        off += wd


def proj_split(x, w, widths, tm=512):
    m, k = x.shape
    n = w.shape[1]
    assert sum(widths) == n
    return pl.pallas_call(
        _proj_kernel,
        grid=(m // tm,),
        in_specs=[pl.BlockSpec((tm, k), lambda i: (i, 0)), pl.BlockSpec((k, n), lambda i: (0, 0))],
        out_specs=[pl.BlockSpec((tm, wd), lambda i: (i, 0)) for wd in widths],
        out_shape=[jax.ShapeDtypeStruct((m, wd), F32) for wd in widths],
        compiler_params=_cparams(("parallel",)),
    )(x, w)


def _adaln_kernel(c_ref, w_ref, b_ref, o_ref):
    c = c_ref[...]
    s = c * jax.nn.sigmoid(c)
    o_ref[...] = jnp.dot(s, w_ref[...], preferred_element_type=F32, precision=HI) + b_ref[...]


def adaln(cc, ada_w, ada_b, tn=1024):
    depth, d, n = ada_w.shape
    return pl.pallas_call(
        _adaln_kernel,
        grid=(depth, n // tn),
        in_specs=[pl.BlockSpec((8, d), lambda l, j: (0, 0)),
                  pl.BlockSpec((None, d, tn), lambda l, j: (l, 0, j)),
                  pl.BlockSpec((None, 1, tn), lambda l, j: (l, 0, j))],
        out_specs=pl.BlockSpec((None, 8, tn), lambda l, j: (l, 0, j)),
        out_shape=jax.ShapeDtypeStruct((depth, 8, n), F32),
        compiler_params=_cparams(("parallel", "parallel")),
    )(cc, ada_w, ada_b.reshape(depth, 1, n))


def _seg_index(i, tiles_per_batch):
    b = i // tiles_per_batch
    return 2 * b + jnp.where(i % tiles_per_batch == 0, 0, 1)


def _modnorm_kernel(x_ref, sh_ref, sc_ref, o_ref):
    x = x_ref[...]
    ms = jnp.mean(x * x, axis=-1, keepdims=True)
    o_ref[...] = (x * lax.rsqrt(ms + NORM_EPS) * (1.0 + sc_ref[...]) + sh_ref[...]).astype(o_ref.dtype)


def modnorm(x, shift, scale, tiles_per_batch):
    t, d = x.shape
    seg = lambda i: (_seg_index(i, tiles_per_batch), 0, 0)
    return pl.pallas_call(
        _modnorm_kernel,
        grid=(t // ROW_TILE,),
        in_specs=[pl.BlockSpec((ROW_TILE, d), lambda i: (i, 0)),
                  pl.BlockSpec((None, 1, d), seg), pl.BlockSpec((None, 1, d), seg)],
        out_specs=pl.BlockSpec((ROW_TILE, d), lambda i: (i, 0)),
        out_shape=jax.ShapeDtypeStruct((t, d), BF16),
        compiler_params=_cparams(("parallel",)),
    )(x, shift, scale)


def _dwconv_kernel(x_ref, w_ref, b_ref, o_ref, *, n_ctx, silu):
    x = x_ref[...]
    t = x.shape[0]
    rows = lax.broadcasted_iota(jnp.int32, x.shape, 0)
    prev = jnp.where((rows == 0) | (rows == n_ctx), 0.0, pltpu.roll(x, 1, 0))
    nxt = jnp.where((rows == n_ctx - 1) | (rows == t - 1), 0.0, pltpu.roll(x, t - 1, 0))
    y = w_ref[0:1, :] * prev + w_ref[1:2, :] * x + w_ref[2:3, :] * nxt + b_ref[...]
    if silu:
        y = y * jax.nn.sigmoid(y)
    o_ref[...] = y


def dwconv_seq(x, w, b, n_ctx, silu, tc=LANE):
    bsz, t, ch = x.shape
    return pl.pallas_call(
        functools.partial(_dwconv_kernel, n_ctx=n_ctx, silu=silu),
        grid=(bsz, ch // tc),
        in_specs=[pl.BlockSpec((None, t, tc), lambda b_, j: (b_, 0, j)),
                  pl.BlockSpec((3, tc), lambda b_, j: (0, j)),
                  pl.BlockSpec((1, tc), lambda b_, j: (0, j))],
        out_specs=pl.BlockSpec((None, t, tc), lambda b_, j: (b_, 0, j)),
        out_shape=jax.ShapeDtypeStruct((bsz, t, ch), F32),
        compiler_params=_cparams(("parallel", "parallel")),
    )(x, w, b.reshape(1, ch))


S5_Q = 8
S5_TG = LANE // S5_GROUP
S5_TS = S5_TG * S5_STATE
S5_NT = MIX_WIDTH // LANE


def _cmul(ar, ai, br, bi):
    return ar * br - ai * bi, ar * bi + ai * br


def s5_weights(lam_re, lam_im, log_step, b_re, b_im, c_re, c_im, reverse):
    g, p, q = S5_GROUPS, S5_STATE, S5_Q
    step = jnp.exp(log_step)[:, None]
    mag = jnp.exp(lam_re * step)
    ab_re = mag * jnp.cos(lam_im * step)
    ab_im = mag * jnp.sin(lam_im * step)
    den = lam_re * lam_re + lam_im * lam_im
    zr, zi = _cmul(ab_re - 1.0, ab_im, lam_re, -lam_im)
    zr, zi = zr / den, zi / den
    bb_re, bb_im = _cmul(zr[..., None], zi[..., None], b_re, b_im)
    pw_re, pw_im = [jnp.ones_like(ab_re)], [jnp.zeros_like(ab_im)]
    for _ in range(q):
        r, i = _cmul(pw_re[-1], pw_im[-1], ab_re, ab_im)
        pw_re.append(r)
        pw_im.append(i)
    pw_re, pw_im = jnp.stack(pw_re), jnp.stack(pw_im)
    cp_re = c_re[None] * pw_re[:, :, None, :] - c_im[None] * pw_im[:, :, None, :]
    cp_im = c_re[None] * pw_im[:, :, None, :] + c_im[None] * pw_re[:, :, None, :]
    kd = (jnp.einsum('dghp,gpk->dghk', cp_re[:q], bb_re, precision=HI)
          - jnp.einsum('dghp,gpk->dghk', cp_im[:q], bb_im, precision=HI))
    def group_blocks(tab, w):
        lead = tab.shape[:-3]
        rows = tab.shape[-2]
        expand = jnp.asarray(np.tile(np.eye(w, dtype=np.float32), (1, S5_TG)))
        wide = jnp.dot(tab, expand, precision=HI).reshape(lead + (S5_NT, S5_TG, rows, S5_TG * w))
        own = (np.arange(S5_TG * w)[None, None, :] // w) == np.arange(S5_TG)[:, None, None]
        return jnp.where(jnp.asarray(own), wide, 0.0).reshape(lead + (S5_NT, S5_TG * rows, S5_TG * w))

    kblk = group_blocks(jnp.swapaxes(kd, -1, -2), S5_GROUP)
    zero_blk = jnp.zeros_like(kblk[0])
    lag = (lambda s, r: s - r) if reverse else (lambda s, r: r - s)
    ktoep = jnp.concatenate(
        [jnp.concatenate([kblk[lag(s, r)] if lag(s, r) >= 0 else zero_blk for r in range(q)], axis=-1)
         for s in range(q)], axis=-2)
    e_of_s = (np.arange(q) if reverse else (q - 1 - np.arange(q)))
    pb_re = pw_re[e_of_s][..., None] * bb_re[None] - pw_im[e_of_s][..., None] * bb_im[None]
    pb_im = pw_re[e_of_s][..., None] * bb_im[None] + pw_im[e_of_s][..., None] * bb_re[None]
    pblk = [group_blocks(jnp.swapaxes(t, -1, -2), p) for t in (pb_re, pb_im)]
    bw = jnp.concatenate([jnp.concatenate([pblk[0][s], pblk[1][s]], axis=-1) for s in range(q)], axis=-2)
    f_of_r = (q - np.arange(q)) if reverse else (np.arange(q) + 1)
    cblk = [group_blocks(jnp.swapaxes(t[f_of_r], -1, -2), S5_GROUP) for t in (cp_re, -cp_im)]
    cw = jnp.concatenate([jnp.concatenate([cb[r] for r in range(q)], axis=-1) for cb in cblk], axis=-2)
    ar, ai = pw_re[q], pw_im[q]
    lv = []
    for _ in range(16):
        lv.append(jnp.concatenate([ar.reshape(S5_NT, S5_TS), ai.reshape(S5_NT, S5_TS)], axis=-1))
        ar, ai = _cmul(ar, ai, ar, ai)
    a2 = jnp.stack(lv, axis=1)
    return ktoep.astype(BF16), bw.astype(BF16), cw.astype(BF16), a2


def _s5_seg_scan(xr, xi, a2_ref, reverse):
    n = xr.shape[0]
    rows = lax.broadcasted_iota(jnp.int32, xr.shape, 0)
    k, shift = 0, 1
    while shift < n:
        ar = a2_ref[k:k + 1, :S5_TS]
        ai = a2_ref[k:k + 1, S5_TS:]
        if reverse:
            keep = rows < n - shift
            sr, si = pltpu.roll(xr, n - shift, 0), pltpu.roll(xi, n - shift, 0)
        else:
            keep = rows >= shift
            sr, si = pltpu.roll(xr, shift, 0), pltpu.roll(xi, shift, 0)
        sr = jnp.where(keep, sr, 0.0)
        si = jnp.where(keep, si, 0.0)
        xr, xi = xr + ar * sr - ai * si, xi + ar * si + ai * sr
        k += 1
        shift *= 2
    return xr, xi


def _s5_entering(sr, si, carry, a2_ref, reverse):
    n = sr.shape[0]
    rows = lax.broadcasted_iota(jnp.int32, sr.shape, 0)
    first = (n - 1) if reverse else 0
    if carry is not None:
        cr, ci = carry
        ar, ai = a2_ref[0:1, :S5_TS], a2_ref[0:1, S5_TS:]
        ir, ii = _cmul(ar, ai, cr, ci)
        sr = jnp.where(rows == first, sr + ir, sr)
        si = jnp.where(rows == first, si + ii, si)
    hr, hi = _s5_seg_scan(sr, si, a2_ref, reverse)
    if reverse:
        out = (hr[0:1], hi[0:1])
        er, ei = pltpu.roll(hr, n - 1, 0), pltpu.roll(hi, n - 1, 0)
    else:
        out = (hr[n - 1:n], hi[n - 1:n])
        er, ei = pltpu.roll(hr, 1, 0), pltpu.roll(hi, 1, 0)
    if carry is None:
        er = jnp.where(rows == first, 0.0, er)
        ei = jnp.where(rows == first, 0.0, ei)
    else:
        er = jnp.where(rows == first, cr, er)
        ei = jnp.where(rows == first, ci, ei)
    return er, ei, out


def _s5_kernel(u_ref, kt_ref, bw_ref, cw_ref, a2_ref, y_ref, *, n_ctx_rows, reverse):
    n = u_ref.shape[0] // S5_Q
    x = jnp.concatenate([u_ref[pl.ds(s, n, stride=S5_Q), :] for s in range(S5_Q)], axis=1).astype(BF16)
    y = jnp.dot(x, kt_ref[...], preferred_element_type=F32)
    st = jnp.dot(x, bw_ref[...], preferred_element_type=F32)
    sr, si = st[:, :S5_TS], st[:, S5_TS:]
    cr, ci, carry = _s5_entering(sr[:n_ctx_rows], si[:n_ctx_rows], None, a2_ref, reverse)
    lr, li, _ = _s5_entering(sr[n_ctx_rows:], si[n_ctx_rows:], carry, a2_ref, reverse)
    h = jnp.concatenate([jnp.concatenate([cr, lr], axis=0), jnp.concatenate([ci, li], axis=0)], axis=1)
    y = y + jnp.dot(h.astype(BF16), cw_ref[...], preferred_element_type=F32)
    for r in range(S5_Q):
        y_ref[pl.ds(r, n, stride=S5_Q), :] = y[:, r * LANE:(r + 1) * LANE]


def s5_scan(u, weights, n_ctx, reverse):
    bsz, t, ch = u.shape
    ktoep, bw, cw, a2 = weights
    qk = S5_Q * LANE
    return pl.pallas_call(
        functools.partial(_s5_kernel, n_ctx_rows=n_ctx // S5_Q, reverse=reverse),
        grid=(S5_NT, bsz),
        in_specs=[pl.BlockSpec((None, t, LANE), lambda j, b: (b, 0, j)),
                  pl.BlockSpec((None, qk, qk), lambda j, b: (j, 0, 0)),
                  pl.BlockSpec((None, qk, 2 * S5_TS), lambda j, b: (j, 0, 0)),
                  pl.BlockSpec((None, 2 * S5_TS, qk), lambda j, b: (j, 0, 0)),
                  pl.BlockSpec((None, 16, 2 * S5_TS), lambda j, b: (j, 0, 0))],
        out_specs=pl.BlockSpec((None, t, LANE), lambda j, b: (b, 0, j)),
        out_shape=jax.ShapeDtypeStruct((bsz, t, ch), F32),
        compiler_params=_cparams(("parallel", "parallel")),
    )(u, ktoep, bw, cw, a2)


def _s5_out_kernel(yf_ref, yb_ref, u_ref, d_ref, w_ref, o_ref):
    y = yf_ref[...] + yb_ref[...] + d_ref[...] * u_ref[...]
    g = jax.nn.gelu(y)
    z = jnp.dot(g.astype(BF16), w_ref[...], preferred_element_type=F32)
    o_ref[...] = (g * jax.nn.sigmoid(z)).astype(o_ref.dtype)


def s5_out(yf, yb, u, d_skip, glu_w, tm=512):
    t, ch = u.shape
    row = pl.BlockSpec((tm, ch), lambda i: (i, 0))
    return pl.pallas_call(
        _s5_out_kernel,
        grid=(t // tm,),
        in_specs=[row, row, row, pl.BlockSpec((1, ch), lambda i: (0, 0)), pl.BlockSpec((ch, ch), lambda i: (0, 0))],
        out_specs=row,
        out_shape=jax.ShapeDtypeStruct((t, ch), BF16),
        compiler_params=_cparams(("parallel",)),
    )(yf, yb, u, d_skip.reshape(1, ch), glu_w.astype(BF16))


def _softplus(x):
    return jnp.maximum(x, 0.0) + jnp.log1p(jnp.exp(-jnp.abs(x)))


def _ssd_kernel(xf_ref, dtf_ref, xb_ref, dtb_ref, bias_ref, a_ref, spread_ref, tri_ref, yf_ref, yb_ref, st_ref, *,
                bsz):
    @pl.when(pl.program_id(0) == 0)
    def _():
        st_ref[...] = jnp.zeros_like(st_ref)

    srcs = ((xf_ref, dtf_ref, yf_ref), (xb_ref, dtb_ref, yb_ref))
    chains = [(d, b) for d in range(2) for b in range(bsz)]
    results = [_ssd_step(srcs[d][0].at[b], srcs[d][1].at[b], bias_ref, a_ref, spread_ref.at[d], tri_ref.at[d],
                         st_ref[d, b], d=d, reverse=d == 1) for d, b in chains]
    for (d, b), (y, st_new) in zip(chains, results):
        srcs[d][2][b] = y
        st_ref[d, b] = st_new


def _ssd_head_spread():
    m = np.zeros((2, LANE, MIX_WIDTH), np.float32)
    for d in range(2):
        for h in range(SSD_HEADS):
            m[d, d * SSD_HEADS + h, h * SSD_HEAD_DIM:(h + 1) * SSD_HEAD_DIM] = 1.0
    return m


def _ssd_step(xbc_ref, dt_ref, bias_ref, a_ref, spread_ref, tri_ref, st, *, d, reverse):
    q = SSD_CHUNK
    hp = SSD_HEAD_DIM
    hg = SSD_HEADS // SSD_GROUPS
    dt = _softplus(dt_ref[...] + bias_ref[...])
    adt = dt * a_ref[...]
    cs = _dot_sel(tri_ref[...], adt)
    cs_t = cs.T
    spread = spread_ref[...]
    dt_x = _dot_by_sel(dt, spread)
    cs_x = _dot_by_sel(cs, spread)
    tot_x = cs_x[0:1, :] if reverse else cs_x[q - 1:q, :]
    ecs_x = jnp.exp(cs_x)
    dec_x = jnp.exp(tot_x - cs_x)
    etot_x = jnp.exp(tot_x)
    xdt = xbc_ref[:, :MIX_WIDTH] * dt_x
    xw = (xdt * dec_x).astype(BF16)
    xdt = xdt.astype(BF16)
    bm = xbc_ref[:, MIX_WIDTH:MIX_WIDTH + SSD_GROUPS * SSD_STATE]
    cm = xbc_ref[:, MIX_WIDTH + SSD_GROUPS * SSD_STATE:]
    bm_t = bm.T
    li = lax.broadcasted_iota(jnp.int32, (q, q), 0)
    si = lax.broadcasted_iota(jnp.int32, (q, q), 1)
    mask = (si >= li) if reverse else (si <= li)
    first_of_pair = lax.broadcasted_iota(jnp.int32, (q, 2 * hp), 1) < hp
    gw = hg * hp
    ys, states = [], []
    for g in range(SSD_GROUPS):
        cg = cm[:, g * SSD_STATE:(g + 1) * SSD_STATE].astype(BF16)
        bg = bm[:, g * SSD_STATE:(g + 1) * SSD_STATE].astype(BF16)
        gmat = lax.dot_general(cg, bg, (((1,), (1,)), ((), ())), preferred_element_type=F32)
        yoff = jnp.dot(cg, st[g].astype(BF16), preferred_element_type=F32) * ecs_x[:, g * gw:(g + 1) * gw]
        for pr in range(hg // 2):
            lanes = slice(g * gw + pr * 2 * hp, g * gw + (pr + 1) * 2 * hp)
            xp = xdt[:, lanes]
            yd = []
            for hl in (2 * pr, 2 * pr + 1):
                col = d * SSD_HEADS + g * hg + hl
                seg = cs[:, col:col + 1] - cs_t[col:col + 1, :]
                lmat = jnp.exp(jnp.where(mask, seg, -jnp.inf))
                yd.append(jnp.dot((gmat * lmat).astype(BF16), xp, preferred_element_type=F32))
            ys.append(jnp.where(first_of_pair, yd[0], yd[1]) + yoff[:, pr * 2 * hp:(pr + 1) * 2 * hp])
        new = jnp.dot(bm_t[g * SSD_STATE:(g + 1) * SSD_STATE, :].astype(BF16), xw[:, g * gw:(g + 1) * gw],
                      preferred_element_type=F32)
        states.append(st[g] * etot_x[:, g * gw:(g + 1) * gw] + new)
    return jnp.concatenate(ys, axis=1), jnp.stack(states)


def ssd_scan(xbc, dt_raw, dt_bias_row, a_row, n_ctx):
    bsz, t, _ = xbc.shape
    q = SSD_CHUNK
    orders = _chunk_order(t // q, n_ctx // q)
    blk = lambda w, cidx: pl.BlockSpec((bsz, q, w), lambda i: (0, cidx(i), 0))
    vec = pl.BlockSpec((1, LANE), lambda i: (0, 0))
    return pl.pallas_call(
        functools.partial(_ssd_kernel, bsz=bsz),
        grid=(t // q,),
        in_specs=[blk(w, c) for c in orders for w in (SSD_CONV_CH, LANE)]
        + [vec, vec, pl.BlockSpec((2, LANE, MIX_WIDTH), lambda i: (0, 0, 0)),
           pl.BlockSpec((2, q, q), lambda i: (0, 0, 0))],
        out_specs=[blk(MIX_WIDTH, c) for c in orders],
        out_shape=[jax.ShapeDtypeStruct((bsz, t, MIX_WIDTH), F32)] * 2,
        scratch_shapes=[pltpu.VMEM((2, bsz, SSD_GROUPS, SSD_STATE, (SSD_HEADS // SSD_GROUPS) * SSD_HEAD_DIM), F32)],
        compiler_params=_cparams(("arbitrary",)),
        name="ssd_scan",
    )(xbc, dt_raw, xbc, dt_raw, dt_bias_row, a_row, jnp.asarray(_ssd_head_spread()), _cumsum_mats(q))


def _ssd_out_kernel(yf_ref, yb_ref, xbc_ref, z_ref, d_ref, nw_ref, o_ref):
    z = z_ref[...]
    y = (d_ref[...] * xbc_ref[...] + yf_ref[...] + yb_ref[...]) * (z * jax.nn.sigmoid(z))
    ms = jnp.mean(y * y, axis=-1, keepdims=True)
    o_ref[...] = (y * lax.rsqrt(ms + NORM_EPS) * nw_ref[...]).astype(o_ref.dtype)


def ssd_out(yf, yb, xbc, z, d_row, norm_w, tm=512):
    t, ch = z.shape
    row = pl.BlockSpec((tm, ch), lambda i: (i, 0))
    vec = pl.BlockSpec((1, ch), lambda i: (0, 0))
    return pl.pallas_call(
        _ssd_out_kernel,
        grid=(t // tm,),
        in_specs=[row, row, pl.BlockSpec((tm, ch), lambda i: (i, 0)), row, vec, vec],
        out_specs=row,
        out_shape=jax.ShapeDtypeStruct((t, ch), BF16),
        compiler_params=_cparams(("parallel",)),
    )(yf, yb, xbc, z, d_row, norm_w.reshape(1, ch))


def _gla_kernel(*refs, bsz):
    ins, (wg_ref, gb_ref, tri_ref), outs, st_ref = refs[:8], refs[8:11], refs[11:13], refs[13]

    @pl.when(pl.program_id(0) == 0)
    def _():
        st_ref[...] = jnp.zeros_like(st_ref)

    chains = [(d, b) for d in range(2) for b in range(bsz)]
    results = []
    for d, b in chains:
        q_ref, k_ref, v_ref, r_ref = ins[4 * d:4 * d + 4]
        results.append(_gla_chain(q_ref[b], k_ref[b], v_ref[b], r_ref[b], wg_ref[d], gb_ref[d], tri_ref[d],
                                  st_ref[d, b], reverse=d == 1))
    for (d, b), (o, st_new) in zip(chains, results):
        outs[d][b] = o
        st_ref[d, b] = st_new


def _gla_chain(q, k, v, r, wg, gb, tri, st, *, reverse):
    qs = GLA_CHUNK
    zl = _dot3(_split2(r), wg) + gb
    la = (jnp.minimum(zl, 0.0) - jnp.log1p(jnp.exp(-jnp.abs(zl)))) / GLA_TAU
    bc = _dot_sel(tri, la)
    tot = bc[0:1, :] if reverse else bc[qs - 1:qs, :]
    qd = (q * (GLA_DK ** -0.5) * jnp.exp(bc)).astype(BF16)
    ki = (k * jnp.exp(-bc)).astype(BF16)
    ke = (k * jnp.exp(tot - bc)).astype(BF16)
    gam = jnp.exp(tot)
    ti = lax.broadcasted_iota(jnp.int32, (qs, qs), 0)
    si = lax.broadcasted_iota(jnp.int32, (qs, qs), 1)
    mask = (si >= ti) if reverse else (si <= ti)
    pair_w = 2 * GLA_DK
    first_q = lax.broadcasted_iota(jnp.int32, (qs, pair_w), 1) < GLA_DK
    first_s = lax.broadcasted_iota(jnp.int32, (GLA_DV, pair_w), 1) < GLA_DK
    nt_dims = (((1,), (1,)), ((), ()))
    outs, states = [], []
    for pr in range(GLA_HEADS // 2):
        lanes = slice(pr * pair_w, (pr + 1) * pair_w)
        qd_p, ki_p, ke_p, st_p = qd[:, lanes], ki[:, lanes], ke[:, lanes], st[:, lanes]
        st_b = st_p.astype(BF16)
        ds = []
        for j in range(2):
            h = 2 * pr + j
            vh = v[:, h * GLA_DV:(h + 1) * GLA_DV].astype(BF16)
            qm = jnp.where(first_q if j == 0 else jnp.logical_not(first_q), qd_p, jnp.zeros_like(qd_p))
            sc = lax.dot_general(qm, ki_p, nt_dims, preferred_element_type=F32)
            sc = jnp.where(mask, sc, 0.0).astype(BF16)
            o_intra = jnp.dot(sc, vh, preferred_element_type=F32)
            o_inter = lax.dot_general(qm, st_b, nt_dims, preferred_element_type=F32)
            outs.append(o_intra + o_inter)
            ds.append(lax.dot_general(vh, ke_p, (((0,), (0,)), ((), ())), preferred_element_type=F32))
        states.append(st_p * gam[:, lanes] + jnp.where(first_s, ds[0], ds[1]))
    return jnp.concatenate(outs, axis=1), jnp.concatenate(states, axis=1)


def _chunk_order(nchunk, nctx):
    fwd = lambda i: i
    bwd = lambda i: jnp.where(i < nctx, nctx - 1 - i, nchunk + nctx - 1 - i)
    return fwd, bwd


def _cumsum_mats(q):
    tri = np.tril(np.ones((q, q), np.float32))
    return jnp.asarray(np.stack([tri, tri.T]))


def gla_scan(q, k, v, r, wg, gb, n_ctx):
    bsz, t, _ = q.shape
    qs = GLA_CHUNK
    orders = _chunk_order(t // qs, n_ctx // qs)
    blk = lambda w, cidx: pl.BlockSpec((bsz, qs, w), lambda i: (0, cidx(i), 0))
    widths = (GLA_KEY_WIDTH, GLA_KEY_WIDTH, MIX_WIDTH, LANE)
    return pl.pallas_call(
        functools.partial(_gla_kernel, bsz=bsz),
        grid=(t // qs,),
        in_specs=[blk(w, c) for c in orders for w in widths]
        + [pl.BlockSpec((2, LANE, GLA_KEY_WIDTH), lambda i: (0, 0, 0)),
           pl.BlockSpec((2, 1, GLA_KEY_WIDTH), lambda i: (0, 0, 0)),
           pl.BlockSpec((2, qs, qs), lambda i: (0, 0, 0))],
        out_specs=[blk(MIX_WIDTH, c) for c in orders],
        out_shape=[jax.ShapeDtypeStruct((bsz, t, MIX_WIDTH), F32)] * 2,
        scratch_shapes=[pltpu.VMEM((2, bsz, GLA_DV, GLA_KEY_WIDTH), F32)],
        compiler_params=_cparams(("arbitrary",)),
        name="gla_scan",
    )(q, k, v, r, q, k, v, r, wg, gb, _cumsum_mats(qs))


def _gla_out_kernel(of_ref, ob_ref, g_ref, nw_ref, o_ref):
    g = g_ref[...]
    sg = g * jax.nn.sigmoid(g)
    for h in range(GLA_HEADS):
        vs = slice(h * GLA_DV, (h + 1) * GLA_DV)
        o = of_ref[:, vs] + ob_ref[:, vs]
        ms = jnp.mean(o * o, axis=-1, keepdims=True)
        o_ref[:, vs] = (o * lax.rsqrt(ms + NORM_EPS) * nw_ref[...] * sg[:, vs]).astype(o_ref.dtype)


def gla_out(of, ob, g, norm_w, tm=512):
    t, ch = g.shape
    row = pl.BlockSpec((tm, ch), lambda i: (i, 0))
    return pl.pallas_call(
        _gla_out_kernel,
        grid=(t // tm,),
        in_specs=[row, row, row, pl.BlockSpec((1, GLA_DV), lambda i: (0, 0))],
        out_specs=row,
        out_shape=jax.ShapeDtypeStruct((t, ch), BF16),
        compiler_params=_cparams(("parallel",)),
    )(of, ob, g, norm_w.reshape(1, GLA_DV))


def _hy_emb(n):
    t = np.linspace(0.0, 1.0, n)[:, None]
    freqs = np.linspace(1e-4, HY_BANDS - 1, HY_BANDS)
    ang = (2.0 * math.pi / n) * np.arange(n)[:, None] * freqs[None, :]
    emb = np.concatenate([t, np.cos(ang), -np.sin(ang)], axis=-1)
    return np.pad(emb, ((0, 0), (0, LANE - HY_EMB))).astype(np.float32)


def _hy_filter_kernel(emb_ref, w1_ref, b1_ref, f1_ref, w2_ref, b2_ref, f2_ref, w3_ref, dl_ref, o_ref):
    emb = emb_ref[...]
    h = jnp.sin(f1_ref[...] * (jnp.dot(emb, w1_ref[...], preferred_element_type=F32, precision=HI) + b1_ref[...]))
    h = jnp.sin(f2_ref[...] * (jnp.dot(h, w2_ref[...], preferred_element_type=F32, precision=HI) + b2_ref[...]))
    h = jnp.dot(h, w3_ref[...], preferred_element_type=F32, precision=HI)
    o_ref[...] = h * jnp.exp(-emb[:, 0:1] * dl_ref[...])


def hyena_filters(n, w1, b1, f1, w2, b2, f2, w3, tm=256):
    nf = w3.shape[1]
    emb = jnp.asarray(_hy_emb(n))
    deltas = np.abs(np.linspace(HY_MIN_DECAY, HY_MAX_DECAY, MIX_WIDTH)).astype(np.float32)
    dl = jnp.asarray(np.tile(deltas, nf // MIX_WIDTH)[None, :])
    w1p = jnp.pad(w1, ((0, LANE - HY_EMB), (0, 0)))
    fd = HY_FILTER_DIM
    vec = lambda w: pl.BlockSpec((1, w), lambda i: (0, 0))
    return pl.pallas_call(
        _hy_filter_kernel,
        grid=(n // tm,),
        in_specs=[pl.BlockSpec((tm, LANE), lambda i: (i, 0)),
                  pl.BlockSpec((LANE, fd), lambda i: (0, 0)), vec(fd), vec(fd),
                  pl.BlockSpec((fd, fd), lambda i: (0, 0)), vec(fd), vec(fd),
                  pl.BlockSpec((fd, nf), lambda i: (0, 0)), vec(nf)],
        out_specs=pl.BlockSpec((tm, nf), lambda i: (i, 0)),
        out_shape=jax.ShapeDtypeStruct((n, nf), F32),
        compiler_params=_cparams(("parallel",)),
    )(emb, w1p, b1.reshape(1, fd), f1.reshape(1, fd), w2, b2.reshape(1, fd), f2.reshape(1, fd), w3, dl)


class _HyDft:
    def __init__(self, n):
        big = 2 * n
        self.n1 = big // FFT_N2
        self.rows = n // FFT_N2
        self.kh = self.n1 // 2 + 1
        self.khp = -(-self.kh // 8) * 8
        k1 = np.arange(self.kh)
        ang = 2.0 * np.pi * np.outer(k1, np.arange(self.rows)) / self.n1
        fwd = np.zeros((2 * self.khp, self.rows))
        fwd[:self.kh] = np.cos(ang)
        fwd[self.khp:self.khp + self.kh] = -np.sin(ang)
        wgt = np.full(self.kh, 2.0)
        wgt[0] = wgt[-1] = 1.0
        inv = np.zeros((self.rows, 2 * self.khp))
        inv[:, :self.kh] = (wgt[:, None] * np.cos(ang)).T / big
        inv[:, self.khp:self.khp + self.kh] = -(wgt[:, None] * np.sin(ang)).T / big
        tw = 2.0 * np.pi * np.outer(k1, np.arange(FFT_N2)) / big
        f32 = lambda a: np.ascontiguousarray(a, dtype=np.float32)
        self.fwd, self.inv = f32(fwd), f32(inv)
        self.tw_re, self.tw_im = f32(np.cos(tw)[:, :, None]), f32(-np.sin(tw)[:, :, None])
        self.small = self.rows < 8


def _inner_dft_mats():
    k2 = np.arange(FFT_N2)
    f2 = np.exp(-2j * np.pi * np.outer(k2, k2) / FFT_N2)
    w2 = np.block([[f2.real, -f2.imag], [f2.imag, f2.real]])
    w2c = np.block([[f2.real, f2.imag], [-f2.imag, f2.real]])
    return w2.astype(np.float32), w2c.astype(np.float32)


_W2, _W2C = _inner_dft_mats()
HY_UNROLL = 4


def _sc_load(ref, idx):
    return jnp.concatenate([ref[w, idx, :] for w in range(ref.shape[0])], axis=1)


def _sc_store(ref, idx, val):
    for w in range(ref.shape[0]):
        ref[w, idx, :] = val[:, w * LANE:(w + 1) * LANE]


def _hy_outer_fwd(load_rows, cst, fwd_ref, xr_ref, xi_ref):
    if cst.small:
        blocks = [load_rows(n1 * FFT_N2, FFT_N2, None) for n1 in range(cst.rows)]
        for k1 in range(cst.kh):
            xr = sum(float(cst.fwd[k1, n1]) * blocks[n1] for n1 in range(cst.rows))
            xi = sum(float(cst.fwd[cst.khp + k1, n1]) * blocks[n1] for n1 in range(cst.rows))
            _sc_store(xr_ref, pl.ds(k1 * FFT_N2, FFT_N2), xr)
            _sc_store(xi_ref, pl.ds(k1 * FFT_N2, FFT_N2), xi)
        return

    fwd2 = _split2(fwd_ref[...])

    def body(n2, c):
        res = _dot3(fwd2, load_rows(n2, cst.rows, FFT_N2))
        _sc_store(xr_ref, pl.ds(n2, cst.khp, stride=FFT_N2), res[:cst.khp])
        _sc_store(xi_ref, pl.ds(n2, cst.khp, stride=FFT_N2), res[cst.khp:])
        return c
    lax.fori_loop(0, FFT_N2, body, 0, unroll=HY_UNROLL)


def _hy_inner_fwd(k1, xr_ref, xi_ref, twr_ref, twi_ref, w2_ref):
    start = k1 * FFT_N2
    rows = pl.ds(start if isinstance(k1, int) else pl.multiple_of(start, FFT_N2), FFT_N2)
    xr, xi = _sc_load(xr_ref, rows), _sc_load(xi_ref, rows)
    twr, twi = twr_ref[k1], twi_ref[k1]
    ar = xr * twr - xi * twi
    ai = xr * twi + xi * twr
    ys = jnp.dot(w2_ref[...], jnp.concatenate([ar, ai], axis=0).astype(BF16), preferred_element_type=F32)
    return rows, twr, twi, ys[:FFT_N2], ys[FFT_N2:]


def _hy_for_k1(cst, body):
    if cst.small:
        for k1 in range(cst.kh):
            body(k1, 0)
    else:
        lax.fori_loop(0, cst.kh, body, 0, unroll=3 if cst.kh % 3 == 0 else 1)


def _hy_spectrum_part(cst, hf_ref, hb_ref, fwd_ref, twr_ref, twi_ref, w2_ref, hr_ref, hi_ref, xr_ref, xi_ref):
    width = hf_ref.shape[1]

    def load_rows(start, count, stride):
        idx = pl.ds(start, count) if stride is None else pl.ds(start, count, stride=stride)
        return jnp.concatenate([hf_ref[idx, :], hb_ref[idx, :]], axis=1)
    _hy_outer_fwd(load_rows, cst, fwd_ref, xr_ref, xi_ref)
    hb0 = hb_ref[0:1, :]

    def body(k1, c):
        rows, _, _, yr, yi = _hy_inner_fwd(k1, xr_ref, xi_ref, twr_ref, twi_ref, w2_ref)
        hr_ref[rows, :] = yr[:, :width] + yr[:, width:] - hb0
        hi_ref[rows, :] = yi[:, :width] - yi[:, width:]
        return c
    _hy_for_k1(cst, body)


def _hy_spectrum_kernel(*refs, parts):
    w2_ref = refs[0]
    pos = 1
    n_in = 5 * len(parts)
    outs = refs[1 + n_in:1 + n_in + 2 * len(parts)]
    xr_ref, xi_ref = refs[-2:]
    for p, cst in enumerate(parts):
        hf_ref, hb_ref, fwd_ref, twr_ref, twi_ref = refs[pos:pos + 5]
        pos += 5
        _hy_spectrum_part(cst, hf_ref, hb_ref, fwd_ref, twr_ref, twi_ref, w2_ref, outs[2 * p], outs[2 * p + 1],
                          xr_ref, xi_ref)


def hyena_spectrum(filts):
    parts = tuple(_HyDft(f.shape[0]) for f in filts)
    nt = MIX_WIDTH // LANE
    width = HY_ORDER * MIX_WIDTH
    args = [jnp.asarray(_W2).astype(BF16)]
    in_specs = [pl.BlockSpec((2 * FFT_N2, 2 * FFT_N2), lambda t: (0, 0))]
    out_specs, out_shape = [], []
    for f, cst in zip(filts, parts):
        n = f.shape[0]
        args += [f, f, jnp.asarray(cst.fwd), jnp.asarray(cst.tw_re), jnp.asarray(cst.tw_im)]
        in_specs += [pl.BlockSpec((n, LANE), lambda t: (0, (t // nt) * 2 * nt + t % nt)),
                     pl.BlockSpec((n, LANE), lambda t: (0, (t // nt) * 2 * nt + nt + t % nt)),
                     pl.BlockSpec(cst.fwd.shape, lambda t: (0, 0)),
                     pl.BlockSpec(cst.tw_re.shape, lambda t: (0, 0, 0)),
                     pl.BlockSpec(cst.tw_im.shape, lambda t: (0, 0, 0))]
        out_specs += [pl.BlockSpec((cst.kh * FFT_N2, LANE), lambda t: (0, t))] * 2
        out_shape += [jax.ShapeDtypeStruct((cst.kh * FFT_N2, width), F32)] * 2
    rows = max(c.khp for c in parts) * FFT_N2
    outs = pl.pallas_call(
        functools.partial(_hy_spectrum_kernel, parts=parts),
        grid=(width // LANE,),
        in_specs=in_specs, out_specs=out_specs, out_shape=out_shape,
        scratch_shapes=[pltpu.VMEM((2, rows, LANE), F32)] * 2,
        compiler_params=_cparams(("parallel",)),
        name="hyena_spectrum",
    )(*args)
    return [(outs[2 * p], outs[2 * p + 1]) for p in range(len(parts))]


def _hy_conv_part(cst, row0, u_ref, g_ref, bias_ref, hr_ref, hi_ref, fwd_ref, inv_ref, twr_ref, twi_ref,
                  w2_ref, w2c_ref, o_ref, xr_ref, xi_ref):
    bsz = u_ref.shape[0]

    def load_rows(start, count, stride):
        idx = pl.ds(row0 + start, count) if stride is None else pl.ds(row0 + start, count, stride=stride)
        return jnp.concatenate([u_ref[b, idx, :] for b in range(bsz)], axis=1)
    _hy_outer_fwd(load_rows, cst, fwd_ref, xr_ref, xi_ref)

    def body(k1, c):
        rows, twr, twi, yr, yi = _hy_inner_fwd(k1, xr_ref, xi_ref, twr_ref, twi_ref, w2_ref)
        hr = jnp.concatenate([hr_ref[rows, :]] * bsz, axis=1)
        hi = jnp.concatenate([hi_ref[rows, :]] * bsz, axis=1)
        zr = yr * hr - yi * hi
        zi = yr * hi + yi * hr
        vs = jnp.dot(w2c_ref[...], jnp.concatenate([zr, zi], axis=0).astype(BF16), preferred_element_type=F32)
        vr, vi = vs[:FFT_N2], vs[FFT_N2:]
        _sc_store(xr_ref, rows, vr * twr + vi * twi)
        _sc_store(xi_ref, rows, vi * twr - vr * twi)
        return c
    _hy_for_k1(cst, body)

    bias = bias_ref[...]

    def emit(idx, y):
        for b in range(bsz):
            yb = y[:, b * LANE:(b + 1) * LANE]
            o_ref[b, idx, :] = (g_ref[b, idx, :] * (yb + bias * u_ref[b, idx, :])).astype(o_ref.dtype)

    if cst.small:
        for n1 in range(cst.rows):
            y = sum(float(cst.inv[n1, k1]) * _sc_load(xr_ref, pl.ds(k1 * FFT_N2, FFT_N2))
                    + float(cst.inv[n1, cst.khp + k1]) * _sc_load(xi_ref, pl.ds(k1 * FFT_N2, FFT_N2))
                    for k1 in range(cst.kh))
            emit(pl.ds(row0 + n1 * FFT_N2, FFT_N2), y)
        return

    inv2 = _split2(inv_ref[...])

    def out_body(n2, c):
        z = jnp.concatenate([_sc_load(xr_ref, pl.ds(n2, cst.khp, stride=FFT_N2)),
                             _sc_load(xi_ref, pl.ds(n2, cst.khp, stride=FFT_N2))], axis=0)
        y = _dot3(inv2, z)
        emit(pl.ds(row0 + n2, cst.rows, stride=FFT_N2), y)
        return c
    lax.fori_loop(0, FFT_N2, out_body, 0, unroll=HY_UNROLL)


def _hy_conv_kernel(*refs, parts, row0s, zero_rows):
    u_ref, g_ref, bias_ref, w2_ref, w2c_ref = refs[:5]
    o_ref, xr_ref, xi_ref = refs[-3:]
    pos = 5
    for cst, row0 in zip(parts, row0s):
        hr_ref, hi_ref, fwd_ref, inv_ref, twr_ref, twi_ref = refs[pos:pos + 6]
        pos += 6
        _hy_conv_part(cst, row0, u_ref, g_ref, bias_ref, hr_ref, hi_ref, fwd_ref, inv_ref, twr_ref, twi_ref,
                      w2_ref, w2c_ref, o_ref, xr_ref, xi_ref)
    if zero_rows is not None:
        lo, hi = zero_rows
        o_ref[:, lo:hi, :] = jnp.zeros((o_ref.shape[0], hi - lo, o_ref.shape[2]), o_ref.dtype)


def hyena_conv(u, u_blk, gate, g_blk, bias, order, spectra, segs, zero_rows):
    out_dtype = F32
    bsz, t, _ = u.shape
    nt = MIX_WIDTH // LANE
    parts = tuple(_HyDft(n) for _, n in segs)
    args = [u, gate, bias, jnp.asarray(_W2).astype(BF16), jnp.asarray(_W2C).astype(BF16)]
    mat = pl.BlockSpec((2 * FFT_N2, 2 * FFT_N2), lambda j: (0, 0))
    in_specs = [pl.BlockSpec((bsz, t, LANE), lambda j: (0, 0, u_blk + j)),
                pl.BlockSpec((bsz, t, LANE), lambda j: (0, 0, g_blk + j)),
                pl.BlockSpec((1, LANE), lambda j: (0, j)), mat, mat]
    for (hr, hi), cst in zip(spectra, parts):
        args += [hr, hi, jnp.asarray(cst.fwd), jnp.asarray(cst.inv), jnp.asarray(cst.tw_re), jnp.asarray(cst.tw_im)]
        hspec = pl.BlockSpec((cst.kh * FFT_N2, LANE), lambda j: (0, order * nt + j))
        in_specs += [hspec, hspec, pl.BlockSpec(cst.fwd.shape, lambda j: (0, 0)),
                     pl.BlockSpec(cst.inv.shape, lambda j: (0, 0)),
                     pl.BlockSpec(cst.tw_re.shape, lambda j: (0, 0, 0)),
                     pl.BlockSpec(cst.tw_im.shape, lambda j: (0, 0, 0))]
    rows = max(c.khp for c in parts) * FFT_N2
    return pl.pallas_call(
        functools.partial(_hy_conv_kernel, parts=parts, row0s=tuple(r for r, _ in segs), zero_rows=zero_rows),
        grid=(nt,),
        in_specs=in_specs,
        out_specs=pl.BlockSpec((bsz, t, LANE), lambda j: (0, 0, j)),
        out_shape=jax.ShapeDtypeStruct((bsz, t, MIX_WIDTH), out_dtype),
        scratch_shapes=[pltpu.VMEM((bsz, rows, LANE), F32)] * 2,
        compiler_params=_cparams(("parallel",)),
        name="hyena_conv",
    )(*args)


def _merge_kernel(ya_ref, yb_ref, yc_ref, yd_ref, ga_ref, gb_ref, gc_ref, gd_ref, w_ref, o_ref):
    acc = None
    for b, (y_ref, g_ref) in enumerate(((ya_ref, ga_ref), (yb_ref, gb_ref), (yc_ref, gc_ref), (yd_ref, gd_ref))):
        t = jax.nn.sigmoid(g_ref[...].astype(F32)) * jnp.dot(y_ref[...].astype(BF16), w_ref[b], preferred_element_type=F32)
        acc = t if acc is None else acc + t
    o_ref[...] = acc.astype(o_ref.dtype)


def merge_branches(ys, gates, w_branch, tm=512, tn=512):
    t, ch = ys[0].shape
    nj = D_MODEL // tn
    yspec = pl.BlockSpec((tm, ch), lambda j, i: (i, 0))
    gspecs = [pl.BlockSpec((tm, tn), functools.partial(lambda j, i, b: (i, b * nj + j), b=b)) for b in range(N_BRANCH)]
    return pl.pallas_call(
        _merge_kernel,
        grid=(nj, t // tm),
        in_specs=[yspec] * 4 + gspecs + [pl.BlockSpec((N_BRANCH, ch, tn), lambda j, i: (0, 0, j))],
        out_specs=pl.BlockSpec((tm, tn), lambda j, i: (i, j)),
        out_shape=jax.ShapeDtypeStruct((t, D_MODEL), BF16),
        compiler_params=_cparams(("parallel", "parallel")),
    )(*ys, gates, gates, gates, gates, w_branch)


def _mm_resid_kernel(a_ref, w_ref, x_ref, g0_ref, g1_ref, o_ref):
    y = jnp.dot(a_ref[...], w_ref[...], preferred_element_type=F32)
    o_ref[:ROW_TILE] = x_ref[:ROW_TILE] + g0_ref[...] * y[:ROW_TILE]
    o_ref[ROW_TILE:] = x_ref[ROW_TILE:] + g1_ref[...] * y[ROW_TILE:]


def matmul_gated_residual(a, w, x, gate, tiles_per_batch, tn=1024):
    t, k = a.shape
    n = w.shape[1]
    tm = 2 * ROW_TILE
    gspec = lambda h: pl.BlockSpec((None, 1, tn), lambda j, i: (_seg_index(2 * i + h, tiles_per_batch), 0, j))
    return pl.pallas_call(
        _mm_resid_kernel,
        grid=(n // tn, t // tm),
        in_specs=[pl.BlockSpec((tm, k), lambda j, i: (i, 0)),
                  pl.BlockSpec((k, tn), lambda j, i: (0, j)),
                  pl.BlockSpec((tm, tn), lambda j, i: (i, j)),
                  gspec(0), gspec(1)],
        out_specs=pl.BlockSpec((tm, tn), lambda j, i: (i, j)),
        out_shape=jax.ShapeDtypeStruct((t, n), F32),
        compiler_params=_cparams(("parallel", "parallel")),
    )(a, w, x, gate, gate)


ROUTE_OFF = MOE_GROUPS
TOK_S = D_MODEL // LANE
MOE_PAIRS = MOE_PER_GROUP * (MOE_PER_GROUP - 1) // 2
MOE_CLASSES = MOE_GROUPS * MOE_PAIRS
DSP_S = TOK_S + 8
MOE_TM = 512


def _store_token_major(ref, val, slab=TOK_S):
    rows = val.shape[0]
    for s in range(TOK_S):
        ref[pl.ds(s, rows, stride=slab), :] = val[:, s * LANE:(s + 1) * LANE]


def _load_token_major(ref, rows, slab=TOK_S):
    return jnp.concatenate([ref[pl.ds(s, rows, stride=slab), :] for s in range(TOK_S)], axis=1)


def _router_kernel(x_ref, sh_ref, sc_ref, wr_ref, br_ref, h_ref, comb_ref):
    x = x_ref[...]
    ms = jnp.mean(x * x, axis=-1, keepdims=True)
    h = x * lax.rsqrt(ms + NORM_EPS) * (1.0 + sc_ref[...]) + sh_ref[...]
    _store_token_major(h_ref, h, DSP_S)
    logits = _dot3(_split2(h), wr_ref[...]) + br_ref[...]
    lane = lax.broadcasted_iota(jnp.int32, logits.shape, 1).astype(F32)
    neg = -jnp.inf
    gmask = lane < MOE_GROUPS
    glog = jnp.where(gmask, logits, neg)
    gmax = jnp.max(glog, axis=-1, keepdims=True)
    g_idx = jnp.min(jnp.where(glog == gmax, lane, float(LANE)), axis=-1, keepdims=True)
    p_g = 1.0 / jnp.sum(jnp.exp(glog - gmax), axis=-1, keepdims=True)
    e_lo = ROUTE_OFF + g_idx * MOE_PER_GROUP
    emask = (lane >= e_lo) & (lane < e_lo + MOE_PER_GROUP)
    v1 = jnp.max(jnp.where(emask, logits, neg), axis=-1, keepdims=True)
    i1 = jnp.min(jnp.where(emask & (logits == v1), lane, float(LANE)), axis=-1, keepdims=True)
    emask2 = emask & (lane != i1)
    v2 = jnp.max(jnp.where(emask2, logits, neg), axis=-1, keepdims=True)
    i2 = jnp.min(jnp.where(emask2 & (logits == v2), lane, float(LANE)), axis=-1, keepdims=True)
    e21 = jnp.exp(v2 - v1)
    w1 = p_g / (1.0 + e21)
    w2 = p_g * e21 / (1.0 + e21)
    comb = jnp.where(lane == i1, w1, jnp.where(lane == i2, w2, 0.0))
    lo = jnp.minimum(i1, i2) - e_lo
    hi = jnp.maximum(i1, i2) - e_lo
    cls = g_idx * MOE_PAIRS + lo * (7.0 - lo) * 0.5 + (hi - lo - 1.0)
    route = jnp.where(lane == 0, cls, comb)
    comb_ref[...] = route
    rows = x.shape[0]
    h_ref[pl.ds(TOK_S, rows, stride=DSP_S), :] = route
    for s in range(TOK_S + 1, DSP_S):
        h_ref[pl.ds(s, rows, stride=DSP_S), :] = jnp.zeros_like(route)


def moe_router(x, shift, scale, group_w, group_b, expert_w, expert_b, tiles_per_batch):
    t, d = x.shape
    wr = _pad_cols(jnp.concatenate([group_w, expert_w], axis=1), LANE)
    br = _pad_cols(jnp.concatenate([group_b, expert_b])[None, :], LANE)
    seg = lambda i: (_seg_index(i, tiles_per_batch), 0, 0)
    return pl.pallas_call(
        _router_kernel,
        grid=(t // ROW_TILE,),
        in_specs=[pl.BlockSpec((ROW_TILE, d), lambda i: (i, 0)),
                  pl.BlockSpec((None, 1, d), seg), pl.BlockSpec((None, 1, d), seg),
                  pl.BlockSpec((d, LANE), lambda i: (0, 0)), pl.BlockSpec((1, LANE), lambda i: (0, 0))],
        out_specs=[pl.BlockSpec((ROW_TILE * DSP_S, LANE), lambda i: (i, 0)),
                   pl.BlockSpec((ROW_TILE, LANE), lambda i: (i, 0))],
        out_shape=[jax.ShapeDtypeStruct((t * DSP_S, LANE), F32), jax.ShapeDtypeStruct((t, LANE), F32)],
        compiler_params=_cparams(("parallel",)),
    )(x, shift, scale, wr, br)


def moe_plan(route, n_tiles):
    g = route[:, 0].astype(jnp.int32)
    oh = (g[:, None] == jnp.arange(MOE_CLASSES, dtype=jnp.int32)[None, :]).astype(jnp.int32)
    cnt = jnp.sum(oh, axis=0)
    rank = jnp.sum((jnp.cumsum(oh, axis=0) - oh) * oh, axis=1)
    ptiles = (cnt + MOE_TM - 1) // MOE_TM
    ends = jnp.cumsum(ptiles)
    off = (ends - ptiles) * MOE_TM
    pos = jnp.sum(oh * off[None, :], axis=1) + rank
    n_act = ends[-1]
    tid = jnp.arange(n_tiles, dtype=jnp.int32)
    tsrc = jnp.minimum(tid, n_act - 1)
    tcls = jnp.sum((tsrc[:, None] >= ends[None, :]).astype(jnp.int32), axis=1)
    pair_lo = jnp.asarray([lo for lo in range(MOE_PER_GROUP) for _ in range(lo + 1, MOE_PER_GROUP)], jnp.int32)
    pair_hi = jnp.asarray([hi for lo in range(MOE_PER_GROUP) for hi in range(lo + 1, MOE_PER_GROUP)], jnp.int32)
    base = (tcls // MOE_PAIRS) * MOE_PER_GROUP
    texp = jnp.stack([base + pair_lo[tcls % MOE_PAIRS], base + pair_hi[tcls % MOE_PAIRS]], axis=1).reshape(-1)
    tact = (tid < n_act).astype(jnp.int32)
    return pos, tsrc, texp, tact


def _slab(ref, row, size):
    return ref.at[pl.ds(pl.multiple_of(row * size, size), size)]


def _moe_scatter_kernel(pos_ref, src_ref, init_ref, dst_ref, sem):
    del init_ref
    base = pl.program_id(0) * ROW_TILE

    def issue(r, c):
        pltpu.make_async_copy(_slab(src_ref, r, DSP_S), _slab(dst_ref, pos_ref[base + r], DSP_S), sem).start()
        return c
    lax.fori_loop(0, ROW_TILE, issue, 0)
    pltpu.make_async_copy(src_ref, dst_ref.at[pl.ds(0, ROW_TILE * DSP_S)], sem).wait()


def moe_scatter(slabs, pos, n_slots):
    t = pos.shape[0]
    init = jnp.zeros((n_slots * DSP_S, LANE), F32)
    return pl.pallas_call(
        _moe_scatter_kernel,
        grid_spec=pltpu.PrefetchScalarGridSpec(
            num_scalar_prefetch=1, grid=(t // ROW_TILE,),
            in_specs=[pl.BlockSpec((ROW_TILE * DSP_S, LANE), lambda i, p: (i, 0)),
                      pl.BlockSpec(memory_space=pl.ANY)],
            out_specs=pl.BlockSpec(memory_space=pl.ANY),
            scratch_shapes=[pltpu.SemaphoreType.DMA(())]),
        out_shape=jax.ShapeDtypeStruct(init.shape, F32),
        input_output_aliases={2: 0},
        compiler_params=_cparams(("arbitrary",)),
    )(pos, slabs, init)


def _moe_expert_kernel(tsrc_ref, texp_ref, tact_ref, xs_ref, wg_ref, wu_ref, wd_ref, ys_ref, x_ref, acc_ref):
    i, e, f = pl.program_id(0), pl.program_id(1), pl.program_id(2)
    active = tact_ref[i] == 1
    first = (e == 0) & (f == 0)
    last = (e == pl.num_programs(1) - 1) & (f == pl.num_programs(2) - 1)

    @pl.when(active & first)
    def _():
        x_ref[...] = _load_token_major(xs_ref, MOE_TM, DSP_S).astype(BF16)
        acc_ref[...] = jnp.zeros_like(acc_ref)

    @pl.when(active)
    def _():
        x = x_ref[...]
        a = jnp.dot(x, wg_ref[...].astype(BF16), preferred_element_type=F32)
        u = jnp.dot(x, wu_ref[...].astype(BF16), preferred_element_type=F32)
        act = (a * jax.nn.sigmoid(a) * u).astype(BF16)
        route = xs_ref[pl.ds(TOK_S, MOE_TM, stride=DSP_S), :]
        lane = lax.broadcasted_iota(jnp.int32, route.shape, 1)
        mine = lane == ROUTE_OFF + texp_ref[2 * i + e]
        w = jnp.sum(jnp.where(mine, route, 0.0), axis=-1, keepdims=True)
        acc_ref[...] += w * jnp.dot(act, wd_ref[...].astype(BF16), preferred_element_type=F32)

    @pl.when(active & last)
    def _():
        _store_token_major(ys_ref, acc_ref[...])

    @pl.when(jnp.logical_not(active) & first)
    def _():
        ys_ref[...] = jnp.zeros_like(ys_ref)


def moe_experts(xs, tsrc, texp, tact, w_gate, w_up, w_down, li, tf=512):
    n_tiles = tsrc.shape[0]
    _, _, d, ff = w_gate.shape
    wsel = lambda i, e, f, ts, te, ta: (li, te[2 * i + e], 0, f)
    rows = lambda i, e, f, ts, te, ta: (ts[i], 0)
    return pl.pallas_call(
        _moe_expert_kernel,
        grid_spec=pltpu.PrefetchScalarGridSpec(
            num_scalar_prefetch=3, grid=(n_tiles, 2, ff // tf),
            in_specs=[pl.BlockSpec((MOE_TM * DSP_S, LANE), rows, pipeline_mode=pl.Buffered(1)),
                      pl.BlockSpec((None, None, d, tf), wsel),
                      pl.BlockSpec((None, None, d, tf), wsel),
                      pl.BlockSpec((None, None, tf, d), lambda i, e, f, ts, te, ta: (li, te[2 * i + e], f, 0))],
            out_specs=pl.BlockSpec((MOE_TM * TOK_S, LANE), lambda i, e, f, ts, te, ta: (i, 0),
                                   pipeline_mode=pl.Buffered(1)),
            scratch_shapes=[pltpu.VMEM((MOE_TM, d), BF16), pltpu.VMEM((MOE_TM, d), F32)]),
        out_shape=jax.ShapeDtypeStruct((n_tiles * MOE_TM * TOK_S, LANE), F32),
        compiler_params=_cparams(("arbitrary", "arbitrary", "arbitrary")),
        name="moe_experts",
    )(tsrc, texp, tact, xs, w_gate, w_up, w_down)


def _moe_gather_kernel(pos_ref, ys_ref, x_ref, g_ref, o_ref, buf_ref, sem):
    base = pl.program_id(0) * ROW_TILE

    def issue(r, c):
        pltpu.make_async_copy(_slab(ys_ref, pos_ref[base + r], TOK_S), _slab(buf_ref, r, TOK_S), sem).start()
        return c
    lax.fori_loop(0, ROW_TILE, issue, 0)
    pltpu.make_async_copy(ys_ref.at[pl.ds(0, ROW_TILE * TOK_S)], buf_ref, sem).wait()
    o_ref[...] = x_ref[...] + g_ref[...] * _load_token_major(buf_ref, ROW_TILE)


def moe_gather_residual(ys, pos, x, gate, tiles_per_batch):
    t, d = x.shape
    return pl.pallas_call(
        _moe_gather_kernel,
        grid_spec=pltpu.PrefetchScalarGridSpec(
            num_scalar_prefetch=1, grid=(t // ROW_TILE,),
            in_specs=[pl.BlockSpec(memory_space=pl.ANY),
                      pl.BlockSpec((ROW_TILE, d), lambda i, p: (i, 0)),
                      pl.BlockSpec((None, 1, d), lambda i, p: (_seg_index(i, tiles_per_batch), 0, 0))],
            out_specs=pl.BlockSpec((ROW_TILE, d), lambda i, p: (i, 0)),
            scratch_shapes=[pltpu.VMEM((ROW_TILE * TOK_S, LANE), F32), pltpu.SemaphoreType.DMA(())]),
        out_shape=jax.ShapeDtypeStruct((t, d), F32),
        compiler_params=_cparams(("arbitrary",)),
    )(pos, ys, x, gate)


def _final_norm_kernel(x_ref, w_ref, o_ref):
    x = x_ref[...]
    ms = jnp.mean(x * x, axis=-1, keepdims=True)
    o_ref[...] = x * lax.rsqrt(ms + NORM_EPS) * w_ref[...]


def final_norm(x, w, n_ctx):
    bsz, t, d = x.shape
    off = n_ctx // ROW_TILE
    return pl.pallas_call(
        _final_norm_kernel,
        grid=(bsz, (t - n_ctx) // ROW_TILE),
        in_specs=[pl.BlockSpec((None, ROW_TILE, d), lambda b, i: (b, i + off, 0)),
                  pl.BlockSpec((1, d), lambda b, i: (0, 0))],
        out_specs=pl.BlockSpec((None, ROW_TILE, d), lambda b, i: (b, i, 0)),
        out_shape=jax.ShapeDtypeStruct((bsz, t - n_ctx, d), F32),
        compiler_params=_cparams(("parallel", "parallel")),
    )(x, w.reshape(1, d))


def _lat_colmajor(t, n_ctx, inverse=False):
    bsz, _, ch = t.shape
    lat = t[:, n_ctx:]
    rows = lat.shape[1] // GRID_W
    shp = (bsz, GRID_W, rows, ch) if inverse else (bsz, rows, GRID_W, ch)
    lat = lat.reshape(shp).transpose(0, 2, 1, 3).reshape(bsz, -1, ch)
    return jnp.concatenate([t[:, :n_ctx], lat], axis=1)


def _seg_table(ctx_vec, lat_mat):
    bsz = lat_mat.shape[0]
    tab = jnp.stack([jnp.broadcast_to(ctx_vec[None, :], lat_mat.shape), lat_mat], axis=1)
    return tab.reshape(2 * bsz, 1, -1)


def kernel(x, c, ctx, c_ctx, ada_w, ada_b, w_in, s5_lambda_re, s5_lambda_im, s5_log_step, s5_b_re, s5_b_im, s5_c_re, s5_c_im, s5_d, s5_glu_w, ssd_conv_w, ssd_conv_b, ssd_a_log, ssd_dt_bias, ssd_d, ssd_norm_w, gla_gate_w, gla_gate_b, gla_norm_w, hy_conv_w, hy_conv_b, hy_w1, hy_b1, hy_freq1, hy_w2, hy_b2, hy_freq2, hy_w3, hy_bias, w_branch, w_out, moe_group_w, moe_group_b, moe_expert_w, moe_expert_b, moe_w_gate, moe_w_up, moe_w_down, final_norm_w):
    bsz, n_lat, d = x.shape
    n_ctx = ctx.shape[1]
    depth = ada_w.shape[0]
    t_b = n_ctx + n_lat
    t_all = bsz * t_b
    tiles_pb = t_b // ROW_TILE
    assert n_ctx == ROW_TILE and n_lat % ROW_TILE == 0 and n_lat == FFT_N // 2 and bsz + 1 <= 8

    xa = jnp.concatenate([ctx, x], axis=1).reshape(t_all, d)
    cc = jnp.pad(jnp.concatenate([c, c_ctx[None, :]], axis=0), ((0, 8 - bsz - 1), (0, 0)))
    mod = adaln(cc, ada_w, ada_b)

    for li in range(depth):
        last = li == depth - 1
        m6 = mod[li].reshape(8, 6, d)
        tabs = [_seg_table(m6[bsz, k], m6[:bsz, k]) for k in range(6)]
        h = modnorm(xa, tabs[0], tabs[1], tiles_pb)

        wl = w_in[li].astype(BF16)
        seg = lambda k0, k1: wl[:, IN_OFFS[k0]:IN_OFFS[k1]]
        (u_s5,) = proj_split(h, seg(0, 1), (MIX_WIDTH,))
        w_ssd = jnp.concatenate([seg(1, 3), _pad_cols(seg(3, 4), LANE)], axis=1)
        z_ssd, xbc_raw, dt_raw = proj_split(h, w_ssd, (MIX_WIDTH, SSD_CONV_CH, LANE))
        w_gla = jnp.concatenate([seg(4, 8), _pad_cols(seg(8, 9), LANE)], axis=1)
        q_g, k_g, v_g, g_g, r_g = proj_split(h, w_gla, (GLA_KEY_WIDTH, GLA_KEY_WIDTH, MIX_WIDTH, MIX_WIDTH, LANE))
        (p_hy,) = proj_split(h, seg(9, 10), (3 * MIX_WIDTH,))
        gates = matmul(h, seg(10, 11), 512, 1024, out_dtype=BF16)

        b3 = lambda a: a.reshape(bsz, t_b, a.shape[-1])

        ys = []
        for dr in range(2):
            wts = s5_weights(s5_lambda_re[li, dr], s5_lambda_im[li, dr], s5_log_step[li, dr], s5_b_re[li, dr],
                             s5_b_im[li, dr], s5_c_re[li, dr], s5_c_im[li, dr], dr == 1)
            ys.append(s5_scan(b3(u_s5), wts, n_ctx, dr == 1).reshape(t_all, MIX_WIDTH))
        ya = s5_out(ys[0], ys[1], u_s5, s5_d[li], s5_glu_w[li])

        z_cm = _lat_colmajor(b3(z_ssd), n_ctx)
        xbc_cm = _lat_colmajor(b3(xbc_raw), n_ctx)
        dt_cm = _lat_colmajor(b3(dt_raw), n_ctx)
        xbc_act = dwconv_seq(xbc_cm, ssd_conv_w[li], ssd_conv_b[li], n_ctx, silu=True)
        bias_row = _pad_cols(ssd_dt_bias[li].reshape(1, -1), LANE)
        a_row = _pad_cols(-jnp.exp(ssd_a_log[li]).reshape(1, -1), LANE)
        yd_ssd = [y.reshape(t_all, MIX_WIDTH) for y in ssd_scan(xbc_act, dt_cm, bias_row, a_row, n_ctx)]
        d_row = jnp.repeat(ssd_d[li], SSD_HEAD_DIM)[None, :]
        yb_cm = ssd_out(yd_ssd[0], yd_ssd[1], xbc_act.reshape(t_all, SSD_CONV_CH), z_cm.reshape(t_all, MIX_WIDTH),
                        d_row, ssd_norm_w[li])
        yb = _lat_colmajor(b3(yb_cm), n_ctx, inverse=True).reshape(t_all, MIX_WIDTH)

        wg = jnp.stack([jnp.pad(gla_gate_w[li, dr], ((dr * GLA_GATE_RANK, LANE - (dr + 1) * GLA_GATE_RANK), (0, 0)))
                        for dr in range(2)])
        os_ = [o.reshape(t_all, MIX_WIDTH)
               for o in gla_scan(b3(q_g), b3(k_g), b3(v_g), b3(r_g), wg, gla_gate_b[li][:, None, :], n_ctx)]
        yc = gla_out(os_[0], os_[1], g_g, gla_norm_w[li])

        u_hy = dwconv_seq(b3(p_hy), hy_conv_w[li], hy_conv_b[li], n_ctx, silu=False)
        hy_p = (hy_w1[li], hy_b1[li], hy_freq1[li], hy_w2[li], hy_b2[li], hy_freq2[li], hy_w3[li])
        filts = [hyena_filters(n_lat, *hy_p)]
        segs = [(n_ctx, n_lat)]
        if not last:
            filts.append(hyena_filters(n_ctx, *hy_p))
            segs.append((0, n_ctx))
        spectra = hyena_spectrum(filts)
        zero_rows = (0, n_ctx) if last else None
        nt = MIX_WIDTH // LANE
        y1 = hyena_conv(u_hy, 0, u_hy, nt, hy_bias[li][0:1], 0, spectra, segs, zero_rows)
        yd = hyena_conv(y1, 0, u_hy, 2 * nt, hy_bias[li][1:2], 1, spectra, segs, zero_rows)
        yd = yd.reshape(t_all, MIX_WIDTH)

        merged = merge_branches((ya, yb, yc, yd), gates, w_branch[li].astype(BF16))
        xa = matmul_gated_residual(merged, w_out[li].astype(BF16), xa, tabs[2], tiles_pb)

        h2, route = moe_router(xa, tabs[3], tabs[4], moe_group_w[li], moe_group_b[li], moe_expert_w[li],
                               moe_expert_b[li], tiles_pb)
        n_tiles = -(-t_all // MOE_TM) + MOE_CLASSES
        pos, tsrc, texp, tact = moe_plan(route, n_tiles)
        xs = moe_scatter(h2, pos, n_tiles * MOE_TM)
        ys = moe_experts(xs, tsrc, texp, tact, moe_w_gate, moe_w_up, moe_w_down, li)
        xa = moe_gather_residual(ys, pos, xa, tabs[5], tiles_pb)

    return final_norm(xa.reshape(bsz, t_b, d), final_norm_w, n_ctx)
```

```python
import functools
import math

import numpy as np
import jax
import jax.numpy as jnp
from jax import lax
from jax.experimental import pallas as pl
from jax.experimental.pallas import tpu as pltpu

F32 = jnp.float32
BF16 = jnp.bfloat16
HI = lax.Precision.HIGHEST

D_MODEL = 2048
GRID_W = 64
NORM_EPS = 1e-6
N_BRANCH = 4
MIX_WIDTH = 768
S5_GROUP = 16
S5_GROUPS = MIX_WIDTH // S5_GROUP
S5_STATE = 64
SSD_HEAD_DIM = 64
SSD_HEADS = MIX_WIDTH // SSD_HEAD_DIM
SSD_GROUPS = 2
SSD_STATE = 64
SSD_CHUNK = 128
SSD_CONV_CH = MIX_WIDTH + 2 * SSD_GROUPS * SSD_STATE
GLA_HEADS = 6
GLA_KEY_WIDTH = MIX_WIDTH // 2
GLA_DK = GLA_KEY_WIDTH // GLA_HEADS
GLA_DV = MIX_WIDTH // GLA_HEADS
GLA_GATE_RANK = 16
GLA_TAU = 16.0
GLA_CHUNK = 64
HY_ORDER = 2
HY_FILTER_DIM = 64
HY_BANDS = 16
HY_EMB = 2 * HY_BANDS + 1
HY_MAX_DECAY = math.log(1e-2) / 0.3
HY_MIN_DECAY = math.log(1e-2) / 1.5
MOE_GROUPS = 4
MOE_PER_GROUP = 4
MOE_EXPERTS = MOE_GROUPS * MOE_PER_GROUP
MOE_FF = 1024

IN_SIZES = (MIX_WIDTH, MIX_WIDTH, SSD_CONV_CH, 2 * SSD_HEADS, GLA_KEY_WIDTH, GLA_KEY_WIDTH, MIX_WIDTH, MIX_WIDTH,
            2 * GLA_GATE_RANK, (HY_ORDER + 1) * MIX_WIDTH, N_BRANCH * D_MODEL)
IN_OFFS = tuple(int(v) for v in np.cumsum((0,) + IN_SIZES))

LANE = 128
ROW_TILE = 256
VMEM_LIMIT = 56 * 1024 * 1024

FFT_N1 = 64
FFT_N2 = 128
FFT_N = FFT_N1 * FFT_N2


def _cparams(sem):
    return pltpu.CompilerParams(dimension_semantics=sem, vmem_limit_bytes=VMEM_LIMIT)


def _split3(a):
    hi = a.astype(BF16)
    r = a - hi.astype(F32)
    mid = r.astype(BF16)
    lo = (r - mid.astype(F32)).astype(BF16)
    return hi, mid, lo


def _dot_sel(sel, x):
    s = sel.astype(BF16)
    return sum(jnp.dot(s, t, preferred_element_type=F32) for t in _split3(x))


def _dot_by_sel(x, sel):
    s = sel.astype(BF16)
    return sum(jnp.dot(t, s, preferred_element_type=F32) for t in _split3(x))


def _split2(a):
    hi = a.astype(BF16)
    return hi, (a - hi.astype(F32)).astype(BF16)


def _dot3(a2, b):
    b_hi, b_lo = _split2(b)
    return (jnp.dot(a2[0], b_hi, preferred_element_type=F32) + jnp.dot(a2[0], b_lo, preferred_element_type=F32)
            + jnp.dot(a2[1], b_hi, preferred_element_type=F32))


def _pad_cols(a, width):
    return jnp.pad(a, [(0, 0)] * (a.ndim - 1) + [(0, width - a.shape[-1])])


def _mm_kernel(x_ref, w_ref, o_ref, *, precision):
    o_ref[...] = jnp.dot(x_ref[...], w_ref[...], preferred_element_type=F32,
                         precision=precision).astype(o_ref.dtype)


def matmul(x, w, tm, tn, out_dtype=F32, precision=None):
    m, k = x.shape
    n = w.shape[1]
    return pl.pallas_call(
        functools.partial(_mm_kernel, precision=precision),
        grid=(n // tn, m // tm),
        in_specs=[pl.BlockSpec((tm, k), lambda j, i: (i, 0)), pl.BlockSpec((k, tn), lambda j, i: (0, j))],
        out_specs=pl.BlockSpec((tm, tn), lambda j, i: (i, j)),
        out_shape=jax.ShapeDtypeStruct((m, n), out_dtype),
        compiler_params=_cparams(("parallel", "parallel")),
    )(x, w)


def _proj_kernel(x_ref, w_ref, *o_refs):
    acc = jnp.dot(x_ref[...], w_ref[...], preferred_element_type=F32)
    off = 0
    for o_ref in o_refs:
        wd = o_ref.shape[-1]
        o_ref[...] = acc[:, off:off + wd].astype(o_ref.dtype)
        off += wd


def proj_split(x, w, widths, tm=512):
    m, k = x.shape
    n = w.shape[1]
    assert sum(widths) == n
    return pl.pallas_call(
        _proj_kernel,
        grid=(m // tm,),
        in_specs=[pl.BlockSpec((tm, k), lambda i: (i, 0)), pl.BlockSpec((k, n), lambda i: (0, 0))],
        out_specs=[pl.BlockSpec((tm, wd), lambda i: (i, 0)) for wd in widths],
        out_shape=[jax.ShapeDtypeStruct((m, wd), F32) for wd in widths],
        compiler_params=_cparams(("parallel",)),
    )(x, w)


def _adaln_kernel(c_ref, w_ref, b_ref, o_ref):
    c = c_ref[...]
    s = c * jax.nn.sigmoid(c)
    o_ref[...] = jnp.dot(s, w_ref[...], preferred_element_type=F32, precision=HI) + b_ref[...]


def adaln(cc, ada_w, ada_b, tn=1024):
    depth, d, n = ada_w.shape
    return pl.pallas_call(
        _adaln_kernel,
        grid=(depth, n // tn),
        in_specs=[pl.BlockSpec((8, d), lambda l, j: (0, 0)),
                  pl.BlockSpec((None, d, tn), lambda l, j: (l, 0, j)),
                  pl.BlockSpec((None, 1, tn), lambda l, j: (l, 0, j))],
        out_specs=pl.BlockSpec((None, 8, tn), lambda l, j: (l, 0, j)),
        out_shape=jax.ShapeDtypeStruct((depth, 8, n), F32),
        compiler_params=_cparams(("parallel", "parallel")),
    )(cc, ada_w, ada_b.reshape(depth, 1, n))


def _seg_index(i, tiles_per_batch):
    b = i // tiles_per_batch
    return 2 * b + jnp.where(i % tiles_per_batch == 0, 0, 1)


def _modnorm_kernel(x_ref, sh_ref, sc_ref, o_ref):
    x = x_ref[...]
    ms = jnp.mean(x * x, axis=-1, keepdims=True)
    o_ref[...] = (x * lax.rsqrt(ms + NORM_EPS) * (1.0 + sc_ref[...]) + sh_ref[...]).astype(o_ref.dtype)


def modnorm(x, shift, scale, tiles_per_batch):
    t, d = x.shape
    seg = lambda i: (_seg_index(i, tiles_per_batch), 0, 0)
    return pl.pallas_call(
        _modnorm_kernel,
        grid=(t // ROW_TILE,),
        in_specs=[pl.BlockSpec((ROW_TILE, d), lambda i: (i, 0)),
                  pl.BlockSpec((None, 1, d), seg), pl.BlockSpec((None, 1, d), seg)],
        out_specs=pl.BlockSpec((ROW_TILE, d), lambda i: (i, 0)),
        out_shape=jax.ShapeDtypeStruct((t, d), BF16),
        compiler_params=_cparams(("parallel",)),
    )(x, shift, scale)


def _dwconv_kernel(x_ref, w_ref, b_ref, o_ref, *, n_ctx, silu):
    x = x_ref[...]
    t = x.shape[0]
    rows = lax.broadcasted_iota(jnp.int32, x.shape, 0)
    prev = jnp.where((rows == 0) | (rows == n_ctx), 0.0, pltpu.roll(x, 1, 0))
    nxt = jnp.where((rows == n_ctx - 1) | (rows == t - 1), 0.0, pltpu.roll(x, t - 1, 0))
    y = w_ref[0:1, :] * prev + w_ref[1:2, :] * x + w_ref[2:3, :] * nxt + b_ref[...]
    if silu:
        y = y * jax.nn.sigmoid(y)
    o_ref[...] = y


def dwconv_seq(x, w, b, n_ctx, silu, tc=LANE):
    bsz, t, ch = x.shape
    return pl.pallas_call(
        functools.partial(_dwconv_kernel, n_ctx=n_ctx, silu=silu),
        grid=(bsz, ch // tc),
        in_specs=[pl.BlockSpec((None, t, tc), lambda b_, j: (b_, 0, j)),
                  pl.BlockSpec((3, tc), lambda b_, j: (0, j)),
                  pl.BlockSpec((1, tc), lambda b_, j: (0, j))],
        out_specs=pl.BlockSpec((None, t, tc), lambda b_, j: (b_, 0, j)),
        out_shape=jax.ShapeDtypeStruct((bsz, t, ch), F32),
        compiler_params=_cparams(("parallel", "parallel")),
    )(x, w, b.reshape(1, ch))


S5_Q = 8
S5_TG = LANE // S5_GROUP
S5_TS = S5_TG * S5_STATE
S5_NT = MIX_WIDTH // LANE


def _cmul(ar, ai, br, bi):
    return ar * br - ai * bi, ar * bi + ai * br


def s5_weights(lam_re, lam_im, log_step, b_re, b_im, c_re, c_im, reverse):
    g, p, q = S5_GROUPS, S5_STATE, S5_Q
    step = jnp.exp(log_step)[:, None]
    mag = jnp.exp(lam_re * step)
    ab_re = mag * jnp.cos(lam_im * step)
    ab_im = mag * jnp.sin(lam_im * step)
    den = lam_re * lam_re + lam_im * lam_im
    zr, zi = _cmul(ab_re - 1.0, ab_im, lam_re, -lam_im)
    zr, zi = zr / den, zi / den
    bb_re, bb_im = _cmul(zr[..., None], zi[..., None], b_re, b_im)
    pw_re, pw_im = [jnp.ones_like(ab_re)], [jnp.zeros_like(ab_im)]
    for _ in range(q):
        r, i = _cmul(pw_re[-1], pw_im[-1], ab_re, ab_im)
        pw_re.append(r)
        pw_im.append(i)
    pw_re, pw_im = jnp.stack(pw_re), jnp.stack(pw_im)
    cp_re = c_re[None] * pw_re[:, :, None, :] - c_im[None] * pw_im[:, :, None, :]
    cp_im = c_re[None] * pw_im[:, :, None, :] + c_im[None] * pw_re[:, :, None, :]
    kd = (jnp.einsum('dghp,gpk->dghk', cp_re[:q], bb_re, precision=HI)
          - jnp.einsum('dghp,gpk->dghk', cp_im[:q], bb_im, precision=HI))
    def group_blocks(tab, w):
        lead = tab.shape[:-3]
        rows = tab.shape[-2]
        expand = jnp.asarray(np.tile(np.eye(w, dtype=np.float32), (1, S5_TG)))
        wide = jnp.dot(tab, expand, precision=HI).reshape(lead + (S5_NT, S5_TG, rows, S5_TG * w))
        own = (np.arange(S5_TG * w)[None, None, :] // w) == np.arange(S5_TG)[:, None, None]
        return jnp.where(jnp.asarray(own), wide, 0.0).reshape(lead + (S5_NT, S5_TG * rows, S5_TG * w))

    kblk = group_blocks(jnp.swapaxes(kd, -1, -2), S5_GROUP)
    zero_blk = jnp.zeros_like(kblk[0])
    lag = (lambda s, r: s - r) if reverse else (lambda s, r: r - s)
    ktoep = jnp.concatenate(
        [jnp.concatenate([kblk[lag(s, r)] if lag(s, r) >= 0 else zero_blk for r in range(q)], axis=-1)
         for s in range(q)], axis=-2)
    e_of_s = (np.arange(q) if reverse else (q - 1 - np.arange(q)))
    pb_re = pw_re[e_of_s][..., None] * bb_re[None] - pw_im[e_of_s][..., None] * bb_im[None]
    pb_im = pw_re[e_of_s][..., None] * bb_im[None] + pw_im[e_of_s][..., None] * bb_re[None]
    pblk = [group_blocks(jnp.swapaxes(t, -1, -2), p) for t in (pb_re, pb_im)]
    bw = jnp.concatenate([jnp.concatenate([pblk[0][s], pblk[1][s]], axis=-1) for s in range(q)], axis=-2)
    f_of_r = (q - np.arange(q)) if reverse else (np.arange(q) + 1)
    cblk = [group_blocks(jnp.swapaxes(t[f_of_r], -1, -2), S5_GROUP) for t in (cp_re, -cp_im)]
    cw = jnp.concatenate([jnp.concatenate([cb[r] for r in range(q)], axis=-1) for cb in cblk], axis=-2)
    ar, ai = pw_re[q], pw_im[q]
    lv = []
    for _ in range(16):
        lv.append(jnp.concatenate([ar.reshape(S5_NT, S5_TS), ai.reshape(S5_NT, S5_TS)], axis=-1))
        ar, ai = _cmul(ar, ai, ar, ai)
    a2 = jnp.stack(lv, axis=1)
    return ktoep.astype(BF16), bw.astype(BF16), cw.astype(BF16), a2


def _s5_seg_scan(xr, xi, a2_ref, reverse):
    n = xr.shape[0]
    rows = lax.broadcasted_iota(jnp.int32, xr.shape, 0)
    k, shift = 0, 1
    while shift < n:
        ar = a2_ref[k:k + 1, :S5_TS]
        ai = a2_ref[k:k + 1, S5_TS:]
        if reverse:
            keep = rows < n - shift
            sr, si = pltpu.roll(xr, n - shift, 0), pltpu.roll(xi, n - shift, 0)
        else:
            keep = rows >= shift
            sr, si = pltpu.roll(xr, shift, 0), pltpu.roll(xi, shift, 0)
        sr = jnp.where(keep, sr, 0.0)
        si = jnp.where(keep, si, 0.0)
        xr, xi = xr + ar * sr - ai * si, xi + ar * si + ai * sr
        k += 1
        shift *= 2
    return xr, xi


def _s5_entering(sr, si, carry, a2_ref, reverse):
    n = sr.shape[0]
    rows = lax.broadcasted_iota(jnp.int32, sr.shape, 0)
    first = (n - 1) if reverse else 0
    if carry is not None:
        cr, ci = carry
        ar, ai = a2_ref[0:1, :S5_TS], a2_ref[0:1, S5_TS:]
        ir, ii = _cmul(ar, ai, cr, ci)
        sr = jnp.where(rows == first, sr + ir, sr)
        si = jnp.where(rows == first, si + ii, si)
    hr, hi = _s5_seg_scan(sr, si, a2_ref, reverse)
    if reverse:
        out = (hr[0:1], hi[0:1])
        er, ei = pltpu.roll(hr, n - 1, 0), pltpu.roll(hi, n - 1, 0)
    else:
        out = (hr[n - 1:n], hi[n - 1:n])
        er, ei = pltpu.roll(hr, 1, 0), pltpu.roll(hi, 1, 0)
    if carry is None:
        er = jnp.where(rows == first, 0.0, er)
        ei = jnp.where(rows == first, 0.0, ei)
    else:
        er = jnp.where(rows == first, cr, er)
        ei = jnp.where(rows == first, ci, ei)
    return er, ei, out


def _s5_kernel(u_ref, kt_ref, bw_ref, cw_ref, a2_ref, y_ref, *, n_ctx_rows, reverse):
    n = u_ref.shape[0] // S5_Q
    x = jnp.concatenate([u_ref[pl.ds(s, n, stride=S5_Q), :] for s in range(S5_Q)], axis=1).astype(BF16)
    y = jnp.dot(x, kt_ref[...], preferred_element_type=F32)
    st = jnp.dot(x, bw_ref[...], preferred_element_type=F32)
    sr, si = st[:, :S5_TS], st[:, S5_TS:]
    cr, ci, carry = _s5_entering(sr[:n_ctx_rows], si[:n_ctx_rows], None, a2_ref, reverse)
    lr, li, _ = _s5_entering(sr[n_ctx_rows:], si[n_ctx_rows:], carry, a2_ref, reverse)
    h = jnp.concatenate([jnp.concatenate([cr, lr], axis=0), jnp.concatenate([ci, li], axis=0)], axis=1)
    y = y + jnp.dot(h.astype(BF16), cw_ref[...], preferred_element_type=F32)
    for r in range(S5_Q):
        y_ref[pl.ds(r, n, stride=S5_Q), :] = y[:, r * LANE:(r + 1) * LANE]


def s5_scan(u, weights, n_ctx, reverse):
    bsz, t, ch = u.shape
    ktoep, bw, cw, a2 = weights
    qk = S5_Q * LANE
    return pl.pallas_call(
        functools.partial(_s5_kernel, n_ctx_rows=n_ctx // S5_Q, reverse=reverse),
        grid=(S5_NT, bsz),
        in_specs=[pl.BlockSpec((None, t, LANE), lambda j, b: (b, 0, j)),
                  pl.BlockSpec((None, qk, qk), lambda j, b: (j, 0, 0)),
                  pl.BlockSpec((None, qk, 2 * S5_TS), lambda j, b: (j, 0, 0)),
                  pl.BlockSpec((None, 2 * S5_TS, qk), lambda j, b: (j, 0, 0)),
                  pl.BlockSpec((None, 16, 2 * S5_TS), lambda j, b: (j, 0, 0))],
        out_specs=pl.BlockSpec((None, t, LANE), lambda j, b: (b, 0, j)),
        out_shape=jax.ShapeDtypeStruct((bsz, t, ch), F32),
        compiler_params=_cparams(("parallel", "parallel")),
    )(u, ktoep, bw, cw, a2)


def _s5_out_kernel(yf_ref, yb_ref, u_ref, d_ref, w_ref, o_ref):
    y = yf_ref[...] + yb_ref[...] + d_ref[...] * u_ref[...]
    g = jax.nn.gelu(y)
    z = jnp.dot(g.astype(BF16), w_ref[...], preferred_element_type=F32)
    o_ref[...] = (g * jax.nn.sigmoid(z)).astype(o_ref.dtype)


def s5_out(yf, yb, u, d_skip, glu_w, tm=512):
    t, ch = u.shape
    row = pl.BlockSpec((tm, ch), lambda i: (i, 0))
    return pl.pallas_call(
        _s5_out_kernel,
        grid=(t // tm,),
        in_specs=[row, row, row, pl.BlockSpec((1, ch), lambda i: (0, 0)), pl.BlockSpec((ch, ch), lambda i: (0, 0))],
        out_specs=row,
        out_shape=jax.ShapeDtypeStruct((t, ch), BF16),
        compiler_params=_cparams(("parallel",)),
    )(yf, yb, u, d_skip.reshape(1, ch), glu_w.astype(BF16))


def _softplus(x):
    return jnp.maximum(x, 0.0) + jnp.log1p(jnp.exp(-jnp.abs(x)))


def _ssd_kernel(xf_ref, dtf_ref, xb_ref, dtb_ref, bias_ref, a_ref, spread_ref, tri_ref, yf_ref, yb_ref, st_ref, *,
                bsz):
    @pl.when(pl.program_id(0) == 0)
    def _():
        st_ref[...] = jnp.zeros_like(st_ref)

    srcs = ((xf_ref, dtf_ref, yf_ref), (xb_ref, dtb_ref, yb_ref))
    chains = [(d, b) for d in range(2) for b in range(bsz)]
    results = [_ssd_step(srcs[d][0].at[b], srcs[d][1].at[b], bias_ref, a_ref, spread_ref.at[d], tri_ref.at[d],
                         st_ref[d, b], d=d, reverse=d == 1) for d, b in chains]
    for (d, b), (y, st_new) in zip(chains, results):
        srcs[d][2][b] = y
        st_ref[d, b] = st_new


def _ssd_head_spread():
    m = np.zeros((2, LANE, MIX_WIDTH), np.float32)
    for d in range(2):
        for h in range(SSD_HEADS):
            m[d, d * SSD_HEADS + h, h * SSD_HEAD_DIM:(h + 1) * SSD_HEAD_DIM] = 1.0
    return m


def _ssd_step(xbc_ref, dt_ref, bias_ref, a_ref, spread_ref, tri_ref, st, *, d, reverse):
    q = SSD_CHUNK
    hp = SSD_HEAD_DIM
    hg = SSD_HEADS // SSD_GROUPS
    dt = _softplus(dt_ref[...] + bias_ref[...])
    adt = dt * a_ref[...]
    cs = _dot_sel(tri_ref[...], adt)
    cs_t = cs.T
    spread = spread_ref[...]
    dt_x = _dot_by_sel(dt, spread)
    cs_x = _dot_by_sel(cs, spread)
    tot_x = cs_x[0:1, :] if reverse else cs_x[q - 1:q, :]
    ecs_x = jnp.exp(cs_x)
    dec_x = jnp.exp(tot_x - cs_x)
    etot_x = jnp.exp(tot_x)
    xdt = xbc_ref[:, :MIX_WIDTH] * dt_x
    xw = (xdt * dec_x).astype(BF16)
    xdt = xdt.astype(BF16)
    bm = xbc_ref[:, MIX_WIDTH:MIX_WIDTH + SSD_GROUPS * SSD_STATE]
    cm = xbc_ref[:, MIX_WIDTH + SSD_GROUPS * SSD_STATE:]
    bm_t = bm.T
    li = lax.broadcasted_iota(jnp.int32, (q, q), 0)
    si = lax.broadcasted_iota(jnp.int32, (q, q), 1)
    mask = (si >= li) if reverse else (si <= li)
    first_of_pair = lax.broadcasted_iota(jnp.int32, (q, 2 * hp), 1) < hp
    gw = hg * hp
    ys, states = [], []
    for g in range(SSD_GROUPS):
        cg = cm[:, g * SSD_STATE:(g + 1) * SSD_STATE].astype(BF16)
        bg = bm[:, g * SSD_STATE:(g + 1) * SSD_STATE].astype(BF16)
        gmat = lax.dot_general(cg, bg, (((1,), (1,)), ((), ())), preferred_element_type=F32)
        yoff = jnp.dot(cg, st[g].astype(BF16), preferred_element_type=F32) * ecs_x[:, g * gw:(g + 1) * gw]
        for pr in range(hg // 2):
            lanes = slice(g * gw + pr * 2 * hp, g * gw + (pr + 1) * 2 * hp)
            xp = xdt[:, lanes]
            yd = []
            for hl in (2 * pr, 2 * pr + 1):
                col = d * SSD_HEADS + g * hg + hl
                seg = cs[:, col:col + 1] - cs_t[col:col + 1, :]
                lmat = jnp.exp(jnp.where(mask, seg, -jnp.inf))
                yd.append(jnp.dot((gmat * lmat).astype(BF16), xp, preferred_element_type=F32))
            ys.append(jnp.where(first_of_pair, yd[0], yd[1]) + yoff[:, pr * 2 * hp:(pr + 1) * 2 * hp])
        new = jnp.dot(bm_t[g * SSD_STATE:(g + 1) * SSD_STATE, :].astype(BF16), xw[:, g * gw:(g + 1) * gw],
                      preferred_element_type=F32)
        states.append(st[g] * etot_x[:, g * gw:(g + 1) * gw] + new)
    return jnp.concatenate(ys, axis=1), jnp.stack(states)


def ssd_scan(xbc, dt_raw, dt_bias_row, a_row, n_ctx):
    bsz, t, _ = xbc.shape
    q = SSD_CHUNK
    orders = _chunk_order(t // q, n_ctx // q)
    blk = lambda w, cidx: pl.BlockSpec((bsz, q, w), lambda i: (0, cidx(i), 0))
    vec = pl.BlockSpec((1, LANE), lambda i: (0, 0))
    return pl.pallas_call(
        functools.partial(_ssd_kernel, bsz=bsz),
        grid=(t // q,),
        in_specs=[blk(w, c) for c in orders for w in (SSD_CONV_CH, LANE)]
        + [vec, vec, pl.BlockSpec((2, LANE, MIX_WIDTH), lambda i: (0, 0, 0)),
           pl.BlockSpec((2, q, q), lambda i: (0, 0, 0))],
        out_specs=[blk(MIX_WIDTH, c) for c in orders],
        out_shape=[jax.ShapeDtypeStruct((bsz, t, MIX_WIDTH), F32)] * 2,
        scratch_shapes=[pltpu.VMEM((2, bsz, SSD_GROUPS, SSD_STATE, (SSD_HEADS // SSD_GROUPS) * SSD_HEAD_DIM), F32)],
        compiler_params=_cparams(("arbitrary",)),
        name="ssd_scan",
    )(xbc, dt_raw, xbc, dt_raw, dt_bias_row, a_row, jnp.asarray(_ssd_head_spread()), _cumsum_mats(q))


def _ssd_out_kernel(yf_ref, yb_ref, xbc_ref, z_ref, d_ref, nw_ref, o_ref):
    z = z_ref[...]
    y = (d_ref[...] * xbc_ref[...] + yf_ref[...] + yb_ref[...]) * (z * jax.nn.sigmoid(z))
    ms = jnp.mean(y * y, axis=-1, keepdims=True)
    o_ref[...] = (y * lax.rsqrt(ms + NORM_EPS) * nw_ref[...]).astype(o_ref.dtype)


def ssd_out(yf, yb, xbc, z, d_row, norm_w, tm=512):
    t, ch = z.shape
    row = pl.BlockSpec((tm, ch), lambda i: (i, 0))
    vec = pl.BlockSpec((1, ch), lambda i: (0, 0))
    return pl.pallas_call(
        _ssd_out_kernel,
        grid=(t // tm,),
        in_specs=[row, row, pl.BlockSpec((tm, ch), lambda i: (i, 0)), row, vec, vec],
        out_specs=row,
        out_shape=jax.ShapeDtypeStruct((t, ch), BF16),
        compiler_params=_cparams(("parallel",)),
    )(yf, yb, xbc, z, d_row, norm_w.reshape(1, ch))


def _gla_kernel(*refs, bsz):
    ins, (wg_ref, gb_ref, tri_ref), outs, st_ref = refs[:8], refs[8:11], refs[11:13], refs[13]

    @pl.when(pl.program_id(0) == 0)
    def _():
        st_ref[...] = jnp.zeros_like(st_ref)

    chains = [(d, b) for d in range(2) for b in range(bsz)]
    results = []
    for d, b in chains:
        q_ref, k_ref, v_ref, r_ref = ins[4 * d:4 * d + 4]
        results.append(_gla_chain(q_ref[b], k_ref[b], v_ref[b], r_ref[b], wg_ref[d], gb_ref[d], tri_ref[d],
                                  st_ref[d, b], reverse=d == 1))
    for (d, b), (o, st_new) in zip(chains, results):
        outs[d][b] = o
        st_ref[d, b] = st_new


def _gla_chain(q, k, v, r, wg, gb, tri, st, *, reverse):
    qs = GLA_CHUNK
    zl = _dot3(_split2(r), wg) + gb
    la = (jnp.minimum(zl, 0.0) - jnp.log1p(jnp.exp(-jnp.abs(zl)))) / GLA_TAU
    bc = _dot_sel(tri, la)
    tot = bc[0:1, :] if reverse else bc[qs - 1:qs, :]
    qd = (q * (GLA_DK ** -0.5) * jnp.exp(bc)).astype(BF16)
    ki = (k * jnp.exp(-bc)).astype(BF16)
    ke = (k * jnp.exp(tot - bc)).astype(BF16)
    gam = jnp.exp(tot)
    ti = lax.broadcasted_iota(jnp.int32, (qs, qs), 0)
    si = lax.broadcasted_iota(jnp.int32, (qs, qs), 1)
    mask = (si >= ti) if reverse else (si <= ti)
    pair_w = 2 * GLA_DK
    first_q = lax.broadcasted_iota(jnp.int32, (qs, pair_w), 1) < GLA_DK
    first_s = lax.broadcasted_iota(jnp.int32, (GLA_DV, pair_w), 1) < GLA_DK
    nt_dims = (((1,), (1,)), ((), ()))
    outs, states = [], []
    for pr in range(GLA_HEADS // 2):
        lanes = slice(pr * pair_w, (pr + 1) * pair_w)
        qd_p, ki_p, ke_p, st_p = qd[:, lanes], ki[:, lanes], ke[:, lanes], st[:, lanes]
        st_b = st_p.astype(BF16)
        ds = []
        for j in range(2):
            h = 2 * pr + j
            vh = v[:, h * GLA_DV:(h + 1) * GLA_DV].astype(BF16)
            qm = jnp.where(first_q if j == 0 else jnp.logical_not(first_q), qd_p, jnp.zeros_like(qd_p))
            sc = lax.dot_general(qm, ki_p, nt_dims, preferred_element_type=F32)
            sc = jnp.where(mask, sc, 0.0).astype(BF16)
            o_intra = jnp.dot(sc, vh, preferred_element_type=F32)
            o_inter = lax.dot_general(qm, st_b, nt_dims, preferred_element_type=F32)
            outs.append(o_intra + o_inter)
            ds.append(lax.dot_general(vh, ke_p, (((0,), (0,)), ((), ())), preferred_element_type=F32))
        states.append(st_p * gam[:, lanes] + jnp.where(first_s, ds[0], ds[1]))
    return jnp.concatenate(outs, axis=1), jnp.concatenate(states, axis=1)


def _chunk_order(nchunk, nctx):
    fwd = lambda i: i
    bwd = lambda i: jnp.where(i < nctx, nctx - 1 - i, nchunk + nctx - 1 - i)
    return fwd, bwd


def _cumsum_mats(q):
    tri = np.tril(np.ones((q, q), np.float32))
    return jnp.asarray(np.stack([tri, tri.T]))


def gla_scan(q, k, v, r, wg, gb, n_ctx):
    bsz, t, _ = q.shape
    qs = GLA_CHUNK
    orders = _chunk_order(t // qs, n_ctx // qs)
    blk = lambda w, cidx: pl.BlockSpec((bsz, qs, w), lambda i: (0, cidx(i), 0))
    widths = (GLA_KEY_WIDTH, GLA_KEY_WIDTH, MIX_WIDTH, LANE)
    return pl.pallas_call(
        functools.partial(_gla_kernel, bsz=bsz),
        grid=(t // qs,),
        in_specs=[blk(w, c) for c in orders for w in widths]
        + [pl.BlockSpec((2, LANE, GLA_KEY_WIDTH), lambda i: (0, 0, 0)),
           pl.BlockSpec((2, 1, GLA_KEY_WIDTH), lambda i: (0, 0, 0)),
           pl.BlockSpec((2, qs, qs), lambda i: (0, 0, 0))],
        out_specs=[blk(MIX_WIDTH, c) for c in orders],
        out_shape=[jax.ShapeDtypeStruct((bsz, t, MIX_WIDTH), F32)] * 2,
        scratch_shapes=[pltpu.VMEM((2, bsz, GLA_DV, GLA_KEY_WIDTH), F32)],
        compiler_params=_cparams(("arbitrary",)),
        name="gla_scan",
    )(q, k, v, r, q, k, v, r, wg, gb, _cumsum_mats(qs))


def _gla_out_kernel(of_ref, ob_ref, g_ref, nw_ref, o_ref):
    g = g_ref[...]
    sg = g * jax.nn.sigmoid(g)
    for h in range(GLA_HEADS):
        vs = slice(h * GLA_DV, (h + 1) * GLA_DV)
        o = of_ref[:, vs] + ob_ref[:, vs]
        ms = jnp.mean(o * o, axis=-1, keepdims=True)
        o_ref[:, vs] = (o * lax.rsqrt(ms + NORM_EPS) * nw_ref[...] * sg[:, vs]).astype(o_ref.dtype)


def gla_out(of, ob, g, norm_w, tm=512):
    t, ch = g.shape
    row = pl.BlockSpec((tm, ch), lambda i: (i, 0))
    return pl.pallas_call(
        _gla_out_kernel,
        grid=(t // tm,),
        in_specs=[row, row, row, pl.BlockSpec((1, GLA_DV), lambda i: (0, 0))],
        out_specs=row,
        out_shape=jax.ShapeDtypeStruct((t, ch), BF16),
        compiler_params=_cparams(("parallel",)),
    )(of, ob, g, norm_w.reshape(1, GLA_DV))


def _hy_emb(n):
    t = np.linspace(0.0, 1.0, n)[:, None]
    freqs = np.linspace(1e-4, HY_BANDS - 1, HY_BANDS)
    ang = (2.0 * math.pi / n) * np.arange(n)[:, None] * freqs[None, :]
    emb = np.concatenate([t, np.cos(ang), -np.sin(ang)], axis=-1)
    return np.pad(emb, ((0, 0), (0, LANE - HY_EMB))).astype(np.float32)


def _hy_filter_kernel(emb_ref, w1_ref, b1_ref, f1_ref, w2_ref, b2_ref, f2_ref, w3_ref, dl_ref, o_ref):
    emb = emb_ref[...]
    h = jnp.sin(f1_ref[...] * (jnp.dot(emb, w1_ref[...], preferred_element_type=F32, precision=HI) + b1_ref[...]))
    h = jnp.sin(f2_ref[...] * (jnp.dot(h, w2_ref[...], preferred_element_type=F32, precision=HI) + b2_ref[...]))
    h = jnp.dot(h, w3_ref[...], preferred_element_type=F32, precision=HI)
    o_ref[...] = h * jnp.exp(-emb[:, 0:1] * dl_ref[...])


def hyena_filters(n, w1, b1, f1, w2, b2, f2, w3, tm=256):
    nf = w3.shape[1]
    emb = jnp.asarray(_hy_emb(n))
    deltas = np.abs(np.linspace(HY_MIN_DECAY, HY_MAX_DECAY, MIX_WIDTH)).astype(np.float32)
    dl = jnp.asarray(np.tile(deltas, nf // MIX_WIDTH)[None, :])
    w1p = jnp.pad(w1, ((0, LANE - HY_EMB), (0, 0)))
    fd = HY_FILTER_DIM
    vec = lambda w: pl.BlockSpec((1, w), lambda i: (0, 0))
    return pl.pallas_call(
        _hy_filter_kernel,
        grid=(n // tm,),
        in_specs=[pl.BlockSpec((tm, LANE), lambda i: (i, 0)),
                  pl.BlockSpec((LANE, fd), lambda i: (0, 0)), vec(fd), vec(fd),
                  pl.BlockSpec((fd, fd), lambda i: (0, 0)), vec(fd), vec(fd),
                  pl.BlockSpec((fd, nf), lambda i: (0, 0)), vec(nf)],
        out_specs=pl.BlockSpec((tm, nf), lambda i: (i, 0)),
        out_shape=jax.ShapeDtypeStruct((n, nf), F32),
        compiler_params=_cparams(("parallel",)),
    )(emb, w1p, b1.reshape(1, fd), f1.reshape(1, fd), w2, b2.reshape(1, fd), f2.reshape(1, fd), w3, dl)


class _HyDft:
    def __init__(self, n):
        big = 2 * n
        self.n1 = big // FFT_N2
        self.rows = n // FFT_N2
        self.kh = self.n1 // 2 + 1
        self.khp = -(-self.kh // 8) * 8
        k1 = np.arange(self.kh)
        ang = 2.0 * np.pi * np.outer(k1, np.arange(self.rows)) / self.n1
        fwd = np.zeros((2 * self.khp, self.rows))
        fwd[:self.kh] = np.cos(ang)
        fwd[self.khp:self.khp + self.kh] = -np.sin(ang)
        wgt = np.full(self.kh, 2.0)
        wgt[0] = wgt[-1] = 1.0
        inv = np.zeros((self.rows, 2 * self.khp))
        inv[:, :self.kh] = (wgt[:, None] * np.cos(ang)).T / big
        inv[:, self.khp:self.khp + self.kh] = -(wgt[:, None] * np.sin(ang)).T / big
        tw = 2.0 * np.pi * np.outer(k1, np.arange(FFT_N2)) / big
        f32 = lambda a: np.ascontiguousarray(a, dtype=np.float32)
        self.fwd, self.inv = f32(fwd), f32(inv)
        self.tw_re, self.tw_im = f32(np.cos(tw)[:, :, None]), f32(-np.sin(tw)[:, :, None])
        self.small = self.rows < 8


def _inner_dft_mats():
    k2 = np.arange(FFT_N2)
    f2 = np.exp(-2j * np.pi * np.outer(k2, k2) / FFT_N2)
    w2 = np.block([[f2.real, -f2.imag], [f2.imag, f2.real]])
    w2c = np.block([[f2.real, f2.imag], [-f2.imag, f2.real]])
    return w2.astype(np.float32), w2c.astype(np.float32)


_W2, _W2C = _inner_dft_mats()
HY_UNROLL = 4


def _sc_load(ref, idx):
    return jnp.concatenate([ref[w, idx, :] for w in range(ref.shape[0])], axis=1)


def _sc_store(ref, idx, val):
    for w in range(ref.shape[0]):
        ref[w, idx, :] = val[:, w * LANE:(w + 1) * LANE]


def _hy_outer_fwd(load_rows, cst, fwd_ref, xr_ref, xi_ref):
    if cst.small:
        blocks = [load_rows(n1 * FFT_N2, FFT_N2, None) for n1 in range(cst.rows)]
        for k1 in range(cst.kh):
            xr = sum(float(cst.fwd[k1, n1]) * blocks[n1] for n1 in range(cst.rows))
            xi = sum(float(cst.fwd[cst.khp + k1, n1]) * blocks[n1] for n1 in range(cst.rows))
            _sc_store(xr_ref, pl.ds(k1 * FFT_N2, FFT_N2), xr)
            _sc_store(xi_ref, pl.ds(k1 * FFT_N2, FFT_N2), xi)
        return

    fwd2 = _split2(fwd_ref[...])

    def body(n2, c):
        res = _dot3(fwd2, load_rows(n2, cst.rows, FFT_N2))
        _sc_store(xr_ref, pl.ds(n2, cst.khp, stride=FFT_N2), res[:cst.khp])
        _sc_store(xi_ref, pl.ds(n2, cst.khp, stride=FFT_N2), res[cst.khp:])
        return c
    lax.fori_loop(0, FFT_N2, body, 0, unroll=HY_UNROLL)


def _hy_inner_fwd(k1, xr_ref, xi_ref, twr_ref, twi_ref, w2_ref):
    start = k1 * FFT_N2
    rows = pl.ds(start if isinstance(k1, int) else pl.multiple_of(start, FFT_N2), FFT_N2)
    xr, xi = _sc_load(xr_ref, rows), _sc_load(xi_ref, rows)
    twr, twi = twr_ref[k1], twi_ref[k1]
    ar = xr * twr - xi * twi
    ai = xr * twi + xi * twr
    ys = jnp.dot(w2_ref[...], jnp.concatenate([ar, ai], axis=0).astype(BF16), preferred_element_type=F32)
    return rows, twr, twi, ys[:FFT_N2], ys[FFT_N2:]


def _hy_for_k1(cst, body):
    if cst.small:
        for k1 in range(cst.kh):
            body(k1, 0)
    else:
        lax.fori_loop(0, cst.kh, body, 0, unroll=3 if cst.kh % 3 == 0 else 1)


def _hy_spectrum_part(cst, hf_ref, hb_ref, fwd_ref, twr_ref, twi_ref, w2_ref, hr_ref, hi_ref, xr_ref, xi_ref):
    width = hf_ref.shape[1]

    def load_rows(start, count, stride):
        idx = pl.ds(start, count) if stride is None else pl.ds(start, count, stride=stride)
        return jnp.concatenate([hf_ref[idx, :], hb_ref[idx, :]], axis=1)
    _hy_outer_fwd(load_rows, cst, fwd_ref, xr_ref, xi_ref)
    hb0 = hb_ref[0:1, :]

    def body(k1, c):
        rows, _, _, yr, yi = _hy_inner_fwd(k1, xr_ref, xi_ref, twr_ref, twi_ref, w2_ref)
        hr_ref[rows, :] = yr[:, :width] + yr[:, width:] - hb0
        hi_ref[rows, :] = yi[:, :width] - yi[:, width:]
        return c
    _hy_for_k1(cst, body)


def _hy_spectrum_kernel(*refs, parts):
    w2_ref = refs[0]
    pos = 1
    n_in = 5 * len(parts)
    outs = refs[1 + n_in:1 + n_in + 2 * len(parts)]
    xr_ref, xi_ref = refs[-2:]
    for p, cst in enumerate(parts):
        hf_ref, hb_ref, fwd_ref, twr_ref, twi_ref = refs[pos:pos + 5]
        pos += 5
        _hy_spectrum_part(cst, hf_ref, hb_ref, fwd_ref, twr_ref, twi_ref, w2_ref, outs[2 * p], outs[2 * p + 1],
                          xr_ref, xi_ref)


def hyena_spectrum(filts):
    parts = tuple(_HyDft(f.shape[0]) for f in filts)
    nt = MIX_WIDTH // LANE
    width = HY_ORDER * MIX_WIDTH
    args = [jnp.asarray(_W2).astype(BF16)]
    in_specs = [pl.BlockSpec((2 * FFT_N2, 2 * FFT_N2), lambda t: (0, 0))]
    out_specs, out_shape = [], []
    for f, cst in zip(filts, parts):
        n = f.shape[0]
        args += [f, f, jnp.asarray(cst.fwd), jnp.asarray(cst.tw_re), jnp.asarray(cst.tw_im)]
        in_specs += [pl.BlockSpec((n, LANE), lambda t: (0, (t // nt) * 2 * nt + t % nt)),
                     pl.BlockSpec((n, LANE), lambda t: (0, (t // nt) * 2 * nt + nt + t % nt)),
                     pl.BlockSpec(cst.fwd.shape, lambda t: (0, 0)),
                     pl.BlockSpec(cst.tw_re.shape, lambda t: (0, 0, 0)),
                     pl.BlockSpec(cst.tw_im.shape, lambda t: (0, 0, 0))]
        out_specs += [pl.BlockSpec((cst.kh * FFT_N2, LANE), lambda t: (0, t))] * 2
        out_shape += [jax.ShapeDtypeStruct((cst.kh * FFT_N2, width), F32)] * 2
    rows = max(c.khp for c in parts) * FFT_N2
    outs = pl.pallas_call(
        functools.partial(_hy_spectrum_kernel, parts=parts),
        grid=(width // LANE,),
        in_specs=in_specs, out_specs=out_specs, out_shape=out_shape,
        scratch_shapes=[pltpu.VMEM((2, rows, LANE), F32)] * 2,
        compiler_params=_cparams(("parallel",)),
        name="hyena_spectrum",
    )(*args)
    return [(outs[2 * p], outs[2 * p + 1]) for p in range(len(parts))]


def _hy_conv_part(cst, row0, u_ref, g_ref, bias_ref, hr_ref, hi_ref, fwd_ref, inv_ref, twr_ref, twi_ref,
                  w2_ref, w2c_ref, o_ref, xr_ref, xi_ref):
    bsz = u_ref.shape[0]

    def load_rows(start, count, stride):
        idx = pl.ds(row0 + start, count) if stride is None else pl.ds(row0 + start, count, stride=stride)
        return jnp.concatenate([u_ref[b, idx, :] for b in range(bsz)], axis=1)
    _hy_outer_fwd(load_rows, cst, fwd_ref, xr_ref, xi_ref)

    def body(k1, c):
        rows, twr, twi, yr, yi = _hy_inner_fwd(k1, xr_ref, xi_ref, twr_ref, twi_ref, w2_ref)
        hr = jnp.concatenate([hr_ref[rows, :]] * bsz, axis=1)
        hi = jnp.concatenate([hi_ref[rows, :]] * bsz, axis=1)
        zr = yr * hr - yi * hi
        zi = yr * hi + yi * hr
        vs = jnp.dot(w2c_ref[...], jnp.concatenate([zr, zi], axis=0).astype(BF16), preferred_element_type=F32)
        vr, vi = vs[:FFT_N2], vs[FFT_N2:]
        _sc_store(xr_ref, rows, vr * twr + vi * twi)
        _sc_store(xi_ref, rows, vi * twr - vr * twi)
        return c
    _hy_for_k1(cst, body)

    bias = bias_ref[...]

    def emit(idx, y):
        for b in range(bsz):
            yb = y[:, b * LANE:(b + 1) * LANE]
            o_ref[b, idx, :] = (g_ref[b, idx, :] * (yb + bias * u_ref[b, idx, :])).astype(o_ref.dtype)

    if cst.small:
        for n1 in range(cst.rows):
            y = sum(float(cst.inv[n1, k1]) * _sc_load(xr_ref, pl.ds(k1 * FFT_N2, FFT_N2))
                    + float(cst.inv[n1, cst.khp + k1]) * _sc_load(xi_ref, pl.ds(k1 * FFT_N2, FFT_N2))
                    for k1 in range(cst.kh))
            emit(pl.ds(row0 + n1 * FFT_N2, FFT_N2), y)
        return

    inv2 = _split2(inv_ref[...])

    def out_body(n2, c):
        z = jnp.concatenate([_sc_load(xr_ref, pl.ds(n2, cst.khp, stride=FFT_N2)),
                             _sc_load(xi_ref, pl.ds(n2, cst.khp, stride=FFT_N2))], axis=0)
        y = _dot3(inv2, z)
        emit(pl.ds(row0 + n2, cst.rows, stride=FFT_N2), y)
        return c
    lax.fori_loop(0, FFT_N2, out_body, 0, unroll=HY_UNROLL)


def _hy_conv_kernel(*refs, parts, row0s, zero_rows):
    u_ref, g_ref, bias_ref, w2_ref, w2c_ref = refs[:5]
    o_ref, xr_ref, xi_ref = refs[-3:]
    pos = 5
    for cst, row0 in zip(parts, row0s):
        hr_ref, hi_ref, fwd_ref, inv_ref, twr_ref, twi_ref = refs[pos:pos + 6]
        pos += 6
        _hy_conv_part(cst, row0, u_ref, g_ref, bias_ref, hr_ref, hi_ref, fwd_ref, inv_ref, twr_ref, twi_ref,
                      w2_ref, w2c_ref, o_ref, xr_ref, xi_ref)
    if zero_rows is not None:
        lo, hi = zero_rows
        o_ref[:, lo:hi, :] = jnp.zeros((o_ref.shape[0], hi - lo, o_ref.shape[2]), o_ref.dtype)


def hyena_conv(u, u_blk, gate, g_blk, bias, order, spectra, segs, zero_rows):
    out_dtype = F32
    bsz, t, _ = u.shape
    nt = MIX_WIDTH // LANE
    parts = tuple(_HyDft(n) for _, n in segs)
    args = [u, gate, bias, jnp.asarray(_W2).astype(BF16), jnp.asarray(_W2C).astype(BF16)]
    mat = pl.BlockSpec((2 * FFT_N2, 2 * FFT_N2), lambda j: (0, 0))
    in_specs = [pl.BlockSpec((bsz, t, LANE), lambda j: (0, 0, u_blk + j)),
                pl.BlockSpec((bsz, t, LANE), lambda j: (0, 0, g_blk + j)),
                pl.BlockSpec((1, LANE), lambda j: (0, j)), mat, mat]
    for (hr, hi), cst in zip(spectra, parts):
        args += [hr, hi, jnp.asarray(cst.fwd), jnp.asarray(cst.inv), jnp.asarray(cst.tw_re), jnp.asarray(cst.tw_im)]
        hspec = pl.BlockSpec((cst.kh * FFT_N2, LANE), lambda j: (0, order * nt + j))
        in_specs += [hspec, hspec, pl.BlockSpec(cst.fwd.shape, lambda j: (0, 0)),
                     pl.BlockSpec(cst.inv.shape, lambda j: (0, 0)),
                     pl.BlockSpec(cst.tw_re.shape, lambda j: (0, 0, 0)),
                     pl.BlockSpec(cst.tw_im.shape, lambda j: (0, 0, 0))]
    rows = max(c.khp for c in parts) * FFT_N2
    return pl.pallas_call(
        functools.partial(_hy_conv_kernel, parts=parts, row0s=tuple(r for r, _ in segs), zero_rows=zero_rows),
        grid=(nt,),
        in_specs=in_specs,
        out_specs=pl.BlockSpec((bsz, t, LANE), lambda j: (0, 0, j)),
        out_shape=jax.ShapeDtypeStruct((bsz, t, MIX_WIDTH), out_dtype),
        scratch_shapes=[pltpu.VMEM((bsz, rows, LANE), F32)] * 2,
        compiler_params=_cparams(("parallel",)),
        name="hyena_conv",
    )(*args)


def _merge_kernel(ya_ref, yb_ref, yc_ref, yd_ref, ga_ref, gb_ref, gc_ref, gd_ref, w_ref, o_ref):
    acc = None
    for b, (y_ref, g_ref) in enumerate(((ya_ref, ga_ref), (yb_ref, gb_ref), (yc_ref, gc_ref), (yd_ref, gd_ref))):
        t = jax.nn.sigmoid(g_ref[...].astype(F32)) * jnp.dot(y_ref[...].astype(BF16), w_ref[b], preferred_element_type=F32)
        acc = t if acc is None else acc + t
    o_ref[...] = acc.astype(o_ref.dtype)


def merge_branches(ys, gates, w_branch, tm=512, tn=512):
    t, ch = ys[0].shape
    nj = D_MODEL // tn
    yspec = pl.BlockSpec((tm, ch), lambda j, i: (i, 0))
    gspecs = [pl.BlockSpec((tm, tn), functools.partial(lambda j, i, b: (i, b * nj + j), b=b)) for b in range(N_BRANCH)]
    return pl.pallas_call(
        _merge_kernel,
        grid=(nj, t // tm),
        in_specs=[yspec] * 4 + gspecs + [pl.BlockSpec((N_BRANCH, ch, tn), lambda j, i: (0, 0, j))],
        out_specs=pl.BlockSpec((tm, tn), lambda j, i: (i, j)),
        out_shape=jax.ShapeDtypeStruct((t, D_MODEL), BF16),
        compiler_params=_cparams(("parallel", "parallel")),
    )(*ys, gates, gates, gates, gates, w_branch)


def _mm_resid_kernel(a_ref, w_ref, x_ref, g0_ref, g1_ref, o_ref):
    y = jnp.dot(a_ref[...], w_ref[...], preferred_element_type=F32)
    o_ref[:ROW_TILE] = x_ref[:ROW_TILE] + g0_ref[...] * y[:ROW_TILE]
    o_ref[ROW_TILE:] = x_ref[ROW_TILE:] + g1_ref[...] * y[ROW_TILE:]


def matmul_gated_residual(a, w, x, gate, tiles_per_batch, tn=1024):
    t, k = a.shape
    n = w.shape[1]
    tm = 2 * ROW_TILE
    gspec = lambda h: pl.BlockSpec((None, 1, tn), lambda j, i: (_seg_index(2 * i + h, tiles_per_batch), 0, j))
    return pl.pallas_call(
        _mm_resid_kernel,
        grid=(n // tn, t // tm),
        in_specs=[pl.BlockSpec((tm, k), lambda j, i: (i, 0)),
                  pl.BlockSpec((k, tn), lambda j, i: (0, j)),
                  pl.BlockSpec((tm, tn), lambda j, i: (i, j)),
                  gspec(0), gspec(1)],
        out_specs=pl.BlockSpec((tm, tn), lambda j, i: (i, j)),
        out_shape=jax.ShapeDtypeStruct((t, n), F32),
        compiler_params=_cparams(("parallel", "parallel")),
    )(a, w, x, gate, gate)


ROUTE_OFF = MOE_GROUPS
TOK_S = D_MODEL // LANE
DSP_S = TOK_S + 8
MOE_TM = 512


def _store_token_major(ref, val, slab=TOK_S):
    rows = val.shape[0]
    for s in range(TOK_S):
        ref[pl.ds(s, rows, stride=slab), :] = val[:, s * LANE:(s + 1) * LANE]


def _load_token_major(ref, rows, slab=TOK_S):
    return jnp.concatenate([ref[pl.ds(s, rows, stride=slab), :] for s in range(TOK_S)], axis=1)


def _router_kernel(x_ref, sh_ref, sc_ref, wr_ref, br_ref, h_ref, comb_ref):
    x = x_ref[...]
    ms = jnp.mean(x * x, axis=-1, keepdims=True)
    h = x * lax.rsqrt(ms + NORM_EPS) * (1.0 + sc_ref[...]) + sh_ref[...]
    _store_token_major(h_ref, h, DSP_S)
    logits = _dot3(_split2(h), wr_ref[...]) + br_ref[...]
    lane = lax.broadcasted_iota(jnp.int32, logits.shape, 1).astype(F32)
    neg = -jnp.inf
    gmask = lane < MOE_GROUPS
    glog = jnp.where(gmask, logits, neg)
    gmax = jnp.max(glog, axis=-1, keepdims=True)
    g_idx = jnp.min(jnp.where(glog == gmax, lane, float(LANE)), axis=-1, keepdims=True)
    p_g = 1.0 / jnp.sum(jnp.exp(glog - gmax), axis=-1, keepdims=True)
    e_lo = ROUTE_OFF + g_idx * MOE_PER_GROUP
    emask = (lane >= e_lo) & (lane < e_lo + MOE_PER_GROUP)
    v1 = jnp.max(jnp.where(emask, logits, neg), axis=-1, keepdims=True)
    i1 = jnp.min(jnp.where(emask & (logits == v1), lane, float(LANE)), axis=-1, keepdims=True)
    emask2 = emask & (lane != i1)
    v2 = jnp.max(jnp.where(emask2, logits, neg), axis=-1, keepdims=True)
    i2 = jnp.min(jnp.where(emask2 & (logits == v2), lane, float(LANE)), axis=-1, keepdims=True)
    e21 = jnp.exp(v2 - v1)
    w1 = p_g / (1.0 + e21)
    w2 = p_g * e21 / (1.0 + e21)
    comb = jnp.where(lane == i1, w1, jnp.where(lane == i2, w2, 0.0))
    route = jnp.where(lane == 0, g_idx, comb)
    comb_ref[...] = route
    rows = x.shape[0]
    h_ref[pl.ds(TOK_S, rows, stride=DSP_S), :] = route
    for s in range(TOK_S + 1, DSP_S):
        h_ref[pl.ds(s, rows, stride=DSP_S), :] = jnp.zeros_like(route)


def moe_router(x, shift, scale, group_w, group_b, expert_w, expert_b, tiles_per_batch):
    t, d = x.shape
    wr = _pad_cols(jnp.concatenate([group_w, expert_w], axis=1), LANE)
    br = _pad_cols(jnp.concatenate([group_b, expert_b])[None, :], LANE)
    seg = lambda i: (_seg_index(i, tiles_per_batch), 0, 0)
    return pl.pallas_call(
        _router_kernel,
        grid=(t // ROW_TILE,),
        in_specs=[pl.BlockSpec((ROW_TILE, d), lambda i: (i, 0)),
                  pl.BlockSpec((None, 1, d), seg), pl.BlockSpec((None, 1, d), seg),
                  pl.BlockSpec((d, LANE), lambda i: (0, 0)), pl.BlockSpec((1, LANE), lambda i: (0, 0))],
        out_specs=[pl.BlockSpec((ROW_TILE * DSP_S, LANE), lambda i: (i, 0)),
                   pl.BlockSpec((ROW_TILE, LANE), lambda i: (i, 0))],
        out_shape=[jax.ShapeDtypeStruct((t * DSP_S, LANE), F32), jax.ShapeDtypeStruct((t, LANE), F32)],
        compiler_params=_cparams(("parallel",)),
    )(x, shift, scale, wr, br)


def moe_plan(route, n_tiles):
    g = route[:, 0].astype(jnp.int32)
    oh = (g[:, None] == jnp.arange(MOE_GROUPS, dtype=jnp.int32)[None, :]).astype(jnp.int32)
    cnt = jnp.sum(oh, axis=0)
    rank = jnp.sum((jnp.cumsum(oh, axis=0) - oh) * oh, axis=1)
    ptiles = (cnt + MOE_TM - 1) // MOE_TM
    ends = jnp.cumsum(ptiles)
    off = (ends - ptiles) * MOE_TM
    pos = jnp.sum(oh * off[None, :], axis=1) + rank
    n_act = ends[-1]
    tid = jnp.arange(n_tiles, dtype=jnp.int32)
    tsrc = jnp.minimum(tid, n_act - 1)
    tgrp = jnp.sum((tsrc[:, None] >= ends[None, :]).astype(jnp.int32), axis=1)
    tact = (tid < n_act).astype(jnp.int32)
    return pos, tsrc, tgrp, tact


def _slab(ref, row, size):
    return ref.at[pl.ds(pl.multiple_of(row * size, size), size)]


def _moe_scatter_kernel(pos_ref, src_ref, init_ref, dst_ref, sem):
    del init_ref
    base = pl.program_id(0) * ROW_TILE

    def issue(r, c):
        pltpu.make_async_copy(_slab(src_ref, r, DSP_S), _slab(dst_ref, pos_ref[base + r], DSP_S), sem).start()
        return c
    lax.fori_loop(0, ROW_TILE, issue, 0)
    pltpu.make_async_copy(src_ref, dst_ref.at[pl.ds(0, ROW_TILE * DSP_S)], sem).wait()


def moe_scatter(slabs, pos, n_slots):
    t = pos.shape[0]
    init = jnp.zeros((n_slots * DSP_S, LANE), F32)
    return pl.pallas_call(
        _moe_scatter_kernel,
        grid_spec=pltpu.PrefetchScalarGridSpec(
            num_scalar_prefetch=1, grid=(t // ROW_TILE,),
            in_specs=[pl.BlockSpec((ROW_TILE * DSP_S, LANE), lambda i, p: (i, 0)),
                      pl.BlockSpec(memory_space=pl.ANY)],
            out_specs=pl.BlockSpec(memory_space=pl.ANY),
            scratch_shapes=[pltpu.SemaphoreType.DMA(())]),
        out_shape=jax.ShapeDtypeStruct(init.shape, F32),
        input_output_aliases={2: 0},
        compiler_params=_cparams(("arbitrary",)),
    )(pos, slabs, init)


def _moe_expert_kernel(tsrc_ref, tgrp_ref, tact_ref, xs_ref, wg_ref, wu_ref, wd_ref, ys_ref, x_ref, acc_ref):
    i, e, f = pl.program_id(0), pl.program_id(1), pl.program_id(2)
    active = tact_ref[i] == 1
    first = (e == 0) & (f == 0)
    last = (e == pl.num_programs(1) - 1) & (f == pl.num_programs(2) - 1)

    @pl.when(active & first)
    def _():
        x_ref[...] = _load_token_major(xs_ref, MOE_TM, DSP_S).astype(BF16)
        acc_ref[...] = jnp.zeros_like(acc_ref)

    @pl.when(active)
    def _():
        x = x_ref[...]
        a = jnp.dot(x, wg_ref[...].astype(BF16), preferred_element_type=F32)
        u = jnp.dot(x, wu_ref[...].astype(BF16), preferred_element_type=F32)
        act = (a * jax.nn.sigmoid(a) * u).astype(BF16)
        route = xs_ref[pl.ds(TOK_S, MOE_TM, stride=DSP_S), :]
        lane = lax.broadcasted_iota(jnp.int32, route.shape, 1)
        mine = lane == ROUTE_OFF + tgrp_ref[i] * MOE_PER_GROUP + e
        w = jnp.sum(jnp.where(mine, route, 0.0), axis=-1, keepdims=True)
        acc_ref[...] += w * jnp.dot(act, wd_ref[...].astype(BF16), preferred_element_type=F32)

    @pl.when(active & last)
    def _():
        _store_token_major(ys_ref, acc_ref[...])

    @pl.when(jnp.logical_not(active) & first)
    def _():
        ys_ref[...] = jnp.zeros_like(ys_ref)


def moe_experts(xs, tsrc, tgrp, tact, w_gate, w_up, w_down, li, tf=512):
    n_tiles = tsrc.shape[0]
    _, _, d, ff = w_gate.shape
    wsel = lambda i, e, f, ts, tg, ta: (li, tg[i] * MOE_PER_GROUP + e, 0, f)
    rows = lambda i, e, f, ts, tg, ta: (ts[i], 0)
    return pl.pallas_call(
        _moe_expert_kernel,
        grid_spec=pltpu.PrefetchScalarGridSpec(
            num_scalar_prefetch=3, grid=(n_tiles, MOE_PER_GROUP, ff // tf),
            in_specs=[pl.BlockSpec((MOE_TM * DSP_S, LANE), rows),
                      pl.BlockSpec((None, None, d, tf), wsel),
                      pl.BlockSpec((None, None, d, tf), wsel),
                      pl.BlockSpec((None, None, tf, d),
                                   lambda i, e, f, ts, tg, ta: (li, tg[i] * MOE_PER_GROUP + e, f, 0))],
            out_specs=pl.BlockSpec((MOE_TM * TOK_S, LANE), lambda i, e, f, ts, tg, ta: (i, 0)),
            scratch_shapes=[pltpu.VMEM((MOE_TM, d), BF16), pltpu.VMEM((MOE_TM, d), F32)]),
        out_shape=jax.ShapeDtypeStruct((n_tiles * MOE_TM * TOK_S, LANE), F32),
        compiler_params=_cparams(("arbitrary", "arbitrary", "arbitrary")),
    )(tsrc, tgrp, tact, xs, w_gate, w_up, w_down)


def _moe_gather_kernel(pos_ref, ys_ref, x_ref, g_ref, o_ref, buf_ref, sem):
    base = pl.program_id(0) * ROW_TILE

    def issue(r, c):
        pltpu.make_async_copy(_slab(ys_ref, pos_ref[base + r], TOK_S), _slab(buf_ref, r, TOK_S), sem).start()
        return c
    lax.fori_loop(0, ROW_TILE, issue, 0)
    pltpu.make_async_copy(ys_ref.at[pl.ds(0, ROW_TILE * TOK_S)], buf_ref, sem).wait()
    o_ref[...] = x_ref[...] + g_ref[...] * _load_token_major(buf_ref, ROW_TILE)


def moe_gather_residual(ys, pos, x, gate, tiles_per_batch):
    t, d = x.shape
    return pl.pallas_call(
        _moe_gather_kernel,
        grid_spec=pltpu.PrefetchScalarGridSpec(
            num_scalar_prefetch=1, grid=(t // ROW_TILE,),
            in_specs=[pl.BlockSpec(memory_space=pl.ANY),
                      pl.BlockSpec((ROW_TILE, d), lambda i, p: (i, 0)),
                      pl.BlockSpec((None, 1, d), lambda i, p: (_seg_index(i, tiles_per_batch), 0, 0))],
            out_specs=pl.BlockSpec((ROW_TILE, d), lambda i, p: (i, 0)),
            scratch_shapes=[pltpu.VMEM((ROW_TILE * TOK_S, LANE), F32), pltpu.SemaphoreType.DMA(())]),
        out_shape=jax.ShapeDtypeStruct((t, d), F32),
        compiler_params=_cparams(("arbitrary",)),
    )(pos, ys, x, gate)


def _final_norm_kernel(x_ref, w_ref, o_ref):
    x = x_ref[...]
    ms = jnp.mean(x * x, axis=-1, keepdims=True)
    o_ref[...] = x * lax.rsqrt(ms + NORM_EPS) * w_ref[...]


def final_norm(x, w, n_ctx):
    bsz, t, d = x.shape
    off = n_ctx // ROW_TILE
    return pl.pallas_call(
        _final_norm_kernel,
        grid=(bsz, (t - n_ctx) // ROW_TILE),
        in_specs=[pl.BlockSpec((None, ROW_TILE, d), lambda b, i: (b, i + off, 0)),
                  pl.BlockSpec((1, d), lambda b, i: (0, 0))],
        out_specs=pl.BlockSpec((None, ROW_TILE, d), lambda b, i: (b, i, 0)),
        out_shape=jax.ShapeDtypeStruct((bsz, t - n_ctx, d), F32),
        compiler_params=_cparams(("parallel", "parallel")),
    )(x, w.reshape(1, d))


def _lat_colmajor(t, n_ctx, inverse=False):
    bsz, _, ch = t.shape
    lat = t[:, n_ctx:]
    rows = lat.shape[1] // GRID_W
    shp = (bsz, GRID_W, rows, ch) if inverse else (bsz, rows, GRID_W, ch)
    lat = lat.reshape(shp).transpose(0, 2, 1, 3).reshape(bsz, -1, ch)
    return jnp.concatenate([t[:, :n_ctx], lat], axis=1)


def _seg_table(ctx_vec, lat_mat):
    bsz = lat_mat.shape[0]
    tab = jnp.stack([jnp.broadcast_to(ctx_vec[None, :], lat_mat.shape), lat_mat], axis=1)
    return tab.reshape(2 * bsz, 1, -1)


def kernel(x, c, ctx, c_ctx, ada_w, ada_b, w_in, s5_lambda_re, s5_lambda_im, s5_log_step, s5_b_re, s5_b_im, s5_c_re, s5_c_im, s5_d, s5_glu_w, ssd_conv_w, ssd_conv_b, ssd_a_log, ssd_dt_bias, ssd_d, ssd_norm_w, gla_gate_w, gla_gate_b, gla_norm_w, hy_conv_w, hy_conv_b, hy_w1, hy_b1, hy_freq1, hy_w2, hy_b2, hy_freq2, hy_w3, hy_bias, w_branch, w_out, moe_group_w, moe_group_b, moe_expert_w, moe_expert_b, moe_w_gate, moe_w_up, moe_w_down, final_norm_w):
    bsz, n_lat, d = x.shape
    n_ctx = ctx.shape[1]
    depth = ada_w.shape[0]
    t_b = n_ctx + n_lat
    t_all = bsz * t_b
    tiles_pb = t_b // ROW_TILE
    assert n_ctx == ROW_TILE and n_lat % ROW_TILE == 0 and n_lat == FFT_N // 2 and bsz + 1 <= 8

    xa = jnp.concatenate([ctx, x], axis=1).reshape(t_all, d)
    cc = jnp.pad(jnp.concatenate([c, c_ctx[None, :]], axis=0), ((0, 8 - bsz - 1), (0, 0)))
    mod = adaln(cc, ada_w, ada_b)

    for li in range(depth):
        last = li == depth - 1
        m6 = mod[li].reshape(8, 6, d)
        tabs = [_seg_table(m6[bsz, k], m6[:bsz, k]) for k in range(6)]
        h = modnorm(xa, tabs[0], tabs[1], tiles_pb)

        wl = w_in[li].astype(BF16)
        seg = lambda k0, k1: wl[:, IN_OFFS[k0]:IN_OFFS[k1]]
        (u_s5,) = proj_split(h, seg(0, 1), (MIX_WIDTH,))
        w_ssd = jnp.concatenate([seg(1, 3), _pad_cols(seg(3, 4), LANE)], axis=1)
        z_ssd, xbc_raw, dt_raw = proj_split(h, w_ssd, (MIX_WIDTH, SSD_CONV_CH, LANE))
        w_gla = jnp.concatenate([seg(4, 8), _pad_cols(seg(8, 9), LANE)], axis=1)
        q_g, k_g, v_g, g_g, r_g = proj_split(h, w_gla, (GLA_KEY_WIDTH, GLA_KEY_WIDTH, MIX_WIDTH, MIX_WIDTH, LANE))
        (p_hy,) = proj_split(h, seg(9, 10), (3 * MIX_WIDTH,))
        gates = matmul(h, seg(10, 11), 512, 1024, out_dtype=BF16)

        b3 = lambda a: a.reshape(bsz, t_b, a.shape[-1])

        ys = []
        for dr in range(2):
            wts = s5_weights(s5_lambda_re[li, dr], s5_lambda_im[li, dr], s5_log_step[li, dr], s5_b_re[li, dr],
                             s5_b_im[li, dr], s5_c_re[li, dr], s5_c_im[li, dr], dr == 1)
            ys.append(s5_scan(b3(u_s5), wts, n_ctx, dr == 1).reshape(t_all, MIX_WIDTH))
        ya = s5_out(ys[0], ys[1], u_s5, s5_d[li], s5_glu_w[li])

        z_cm = _lat_colmajor(b3(z_ssd), n_ctx)
        xbc_cm = _lat_colmajor(b3(xbc_raw), n_ctx)
        dt_cm = _lat_colmajor(b3(dt_raw), n_ctx)
        xbc_act = dwconv_seq(xbc_cm, ssd_conv_w[li], ssd_conv_b[li], n_ctx, silu=True)
        bias_row = _pad_cols(ssd_dt_bias[li].reshape(1, -1), LANE)
        a_row = _pad_cols(-jnp.exp(ssd_a_log[li]).reshape(1, -1), LANE)
        yd_ssd = [y.reshape(t_all, MIX_WIDTH) for y in ssd_scan(xbc_act, dt_cm, bias_row, a_row, n_ctx)]
        d_row = jnp.repeat(ssd_d[li], SSD_HEAD_DIM)[None, :]
        yb_cm = ssd_out(yd_ssd[0], yd_ssd[1], xbc_act.reshape(t_all, SSD_CONV_CH), z_cm.reshape(t_all, MIX_WIDTH),
                        d_row, ssd_norm_w[li])
        yb = _lat_colmajor(b3(yb_cm), n_ctx, inverse=True).reshape(t_all, MIX_WIDTH)

        wg = jnp.stack([jnp.pad(gla_gate_w[li, dr], ((dr * GLA_GATE_RANK, LANE - (dr + 1) * GLA_GATE_RANK), (0, 0)))
                        for dr in range(2)])
        os_ = [o.reshape(t_all, MIX_WIDTH)
               for o in gla_scan(b3(q_g), b3(k_g), b3(v_g), b3(r_g), wg, gla_gate_b[li][:, None, :], n_ctx)]
        yc = gla_out(os_[0], os_[1], g_g, gla_norm_w[li])

        u_hy = dwconv_seq(b3(p_hy), hy_conv_w[li], hy_conv_b[li], n_ctx, silu=False)
        hy_p = (hy_w1[li], hy_b1[li], hy_freq1[li], hy_w2[li], hy_b2[li], hy_freq2[li], hy_w3[li])
        filts = [hyena_filters(n_lat, *hy_p)]
        segs = [(n_ctx, n_lat)]
        if not last:
            filts.append(hyena_filters(n_ctx, *hy_p))
            segs.append((0, n_ctx))
        spectra = hyena_spectrum(filts)
        zero_rows = (0, n_ctx) if last else None
        nt = MIX_WIDTH // LANE
        y1 = hyena_conv(u_hy, 0, u_hy, nt, hy_bias[li][0:1], 0, spectra, segs, zero_rows)
        yd = hyena_conv(y1, 0, u_hy, 2 * nt, hy_bias[li][1:2], 1, spectra, segs, zero_rows)
        yd = yd.reshape(t_all, MIX_WIDTH)

        merged = merge_branches((ya, yb, yc, yd), gates, w_branch[li].astype(BF16))
        xa = matmul_gated_residual(merged, w_out[li].astype(BF16), xa, tabs[2], tiles_pb)

        h2, route = moe_router(xa, tabs[3], tabs[4], moe_group_w[li], moe_group_b[li], moe_expert_w[li],
                               moe_expert_b[li], tiles_pb)
        n_tiles = -(-t_all // MOE_TM) + MOE_GROUPS
        pos, tsrc, tgrp, tact = moe_plan(route, n_tiles)
        xs = moe_scatter(h2, pos, n_tiles * MOE_TM)
        ys = moe_experts(xs, tsrc, tgrp, tact, moe_w_gate, moe_w_up, moe_w_down, li)
        xa = moe_gather_residual(ys, pos, xa, tabs[5], tiles_pb)

    return final_norm(xa.reshape(bsz, t_b, d), final_norm_w, n_ctx)
```
